```python
import math
import jax, jax.numpy as jnp
from jax import lax
import numpy as np

D_MODEL = 1024
BATCH = 8
SEQ = 4096
DEPTH = 2
DEC_BATCH = 32
DEC_SEQ = 1
PAST_LEN = 16384
PAGE_SIZE = 128

N_A_LAYERS = DEPTH // 2
N_B_LAYERS = DEPTH - N_A_LAYERS
SSM_EXPAND = 2
D_INNER = SSM_EXPAND * D_MODEL
SSM_HEADDIM = 64
SSM_HEADS = D_INNER // SSM_HEADDIM
SSM_GROUPS = 4
HEADS_PER_GROUP = SSM_HEADS // SSM_GROUPS
D_STATE = 128
D_CONV = 4
CONV_DIM = D_INNER + 2 * SSM_GROUPS * D_STATE
D_IN_PROJ = D_INNER + CONV_DIM + SSM_HEADS
SSD_CHUNK = 128
HEAD_DIM = 64
N_HEADS = D_MODEL // (2 * HEAD_DIM)
Q_BLOCK = 128
N_BUCKETS = 32
MAX_DISTANCE = 128
D_FF = ((8 * D_MODEL + 3 * 256 - 1) // (3 * 256)) * 256
EPS = 1e-6

kernel_name = 'yoco_mamba2_diffattn_step'


def rmsnorm(x, g):
    xf = x.astype(jnp.float32)
    y = xf * lax.rsqrt(jnp.mean(xf * xf, axis=-1, keepdims=True) + EPS)
    return (y * g.astype(jnp.float32)).astype(x.dtype)


def swiglu(h, w_gate_up, w_down):
    g, u = jnp.split(h @ w_gate_up, 2, axis=-1)
    return (jax.nn.silu(g) * u) @ w_down


def causal_dwconv(x, prev, conv_w, conv_b):
    xpad = jnp.concatenate([prev.astype(x.dtype), x], axis=1)
    y = lax.conv_general_dilated(xpad, conv_w[:, None, :].astype(x.dtype), window_strides=(1,),
                                 padding='VALID', dimension_numbers=('NWC', 'WIO', 'NWC'),
                                 feature_group_count=x.shape[-1])
    return y + conv_b.astype(x.dtype), xpad[:, -(D_CONV - 1):]


def ssd(x, dt, a, bm, cm, h0):
    bsz, L = x.shape[:2]
    cl = min(SSD_CHUNK, L)
    pad = (-L) % cl
    if pad:
        def pw(t):
            return jnp.pad(t, [(0, 0), (0, pad)] + [(0, 0)] * (t.ndim - 2))
        x, dt, bm, cm = pw(x), pw(dt), pw(bm), pw(cm)
    nc = (L + pad) // cl
    x = x.reshape(bsz, nc, cl, SSM_GROUPS, HEADS_PER_GROUP, SSM_HEADDIM)
    dt = dt.reshape(bsz, nc, cl, SSM_GROUPS, HEADS_PER_GROUP)
    bm = bm.reshape(bsz, nc, cl, SSM_GROUPS, D_STATE)
    cm = cm.reshape(bsz, nc, cl, SSM_GROUPS, D_STATE)
    xdt = x * dt[..., None]
    a_cum = jnp.cumsum(dt * a, axis=2)
    diff = a_cum[:, :, :, None] - a_cum[:, :, None, :]
    causal = (jnp.arange(cl)[:, None] >= jnp.arange(cl)[None, :])[None, None, :, :, None, None]
    decay = jnp.exp(jnp.where(causal, diff, -jnp.inf))
    cb = jnp.einsum('bclgn,bcsgn->bclsg', cm, bm)
    y_diag = jnp.einsum('bclsgr,bcsgrp->bclgrp', cb[..., None] * decay, xdt)
    decay_to_end = jnp.exp(a_cum[:, :, -1:] - a_cum)
    chunk_states = jnp.einsum('bclgn,bclgr,bclgrp->bcgrpn', bm, decay_to_end, xdt)
    chunk_decay = jnp.exp(a_cum[:, :, -1])

    def step(h, inp):
        s, dcy = inp
        return dcy[..., None, None] * h + s, h

    h_final, h_prev = lax.scan(step, h0, (jnp.moveaxis(chunk_states, 1, 0), jnp.moveaxis(chunk_decay, 1, 0)))
    h_prev = jnp.moveaxis(h_prev, 0, 1)
    y_off = jnp.einsum('bclgn,bcgrpn,bclgr->bclgrp', cm, h_prev, jnp.exp(a_cum))
    y = (y_diag + y_off).reshape(bsz, nc * cl, SSM_GROUPS, HEADS_PER_GROUP, SSM_HEADDIM)[:, :L]
    return y, h_final


def mamba_mixer(h, conv_prev, ssm_prev, w_in, conv_w, conv_b, dt_bias, a_log, d_skip, norm_g, w_out):
    f32 = jnp.float32
    bsz, L, _ = h.shape
    z, xbc, dt_raw = jnp.split(h @ w_in, [D_INNER, D_INNER + CONV_DIM], axis=-1)
    xbc, conv_new = causal_dwconv(xbc, conv_prev, conv_w, conv_b)
    xbc = jax.nn.silu(xbc)
    xs, bm, cm = jnp.split(xbc, [D_INNER, D_INNER + SSM_GROUPS * D_STATE], axis=-1)
    xs = xs.astype(f32).reshape(bsz, L, SSM_GROUPS, HEADS_PER_GROUP, SSM_HEADDIM)
    bm = bm.astype(f32).reshape(bsz, L, SSM_GROUPS, D_STATE)
    cm = cm.astype(f32).reshape(bsz, L, SSM_GROUPS, D_STATE)
    dt = jax.nn.softplus(dt_raw.astype(f32) + dt_bias.astype(f32)).reshape(bsz, L, SSM_GROUPS, HEADS_PER_GROUP)
    a = -jnp.exp(a_log.astype(f32)).reshape(SSM_GROUPS, HEADS_PER_GROUP)
    h0 = ssm_prev.astype(f32).reshape(bsz, SSM_GROUPS, HEADS_PER_GROUP, SSM_HEADDIM, D_STATE)
    y, h_new = ssd(xs, dt, a, bm, cm, h0)
    y = y + d_skip.astype(f32).reshape(SSM_GROUPS, HEADS_PER_GROUP)[..., None] * xs
    gated = (y.reshape(bsz, L, D_INNER) * jax.nn.silu(z.astype(f32))).reshape(bsz, L, SSM_GROUPS, D_INNER // SSM_GROUPS)
    gated = rmsnorm(gated, norm_g.reshape(SSM_GROUPS, D_INNER // SSM_GROUPS)).reshape(bsz, L, D_INNER)
    out = gated.astype(h.dtype) @ w_out
    return out, conv_new, h_new.reshape(bsz, SSM_HEADS, SSM_HEADDIM, D_STATE).astype(ssm_prev.dtype)


def rel_bucket(q_pos, k_pos):
    n = jnp.maximum(q_pos[:, None] - k_pos[None, :], 0)
    max_exact = N_BUCKETS // 2
    nf = jnp.maximum(n, 1).astype(jnp.float32)
    large = max_exact + (jnp.log(nf / max_exact) / math.log(MAX_DISTANCE / max_exact)
                         * (N_BUCKETS - max_exact)).astype(jnp.int32)
    large = jnp.minimum(large, N_BUCKETS - 1)
    return jnp.where(n < max_exact, n, large)


def diff_attend(q, k, v, q_pos, k_pos, rel_bias, lam):
    s = jnp.einsum('bqhmd,bkhmd->bhmqk', q, k).astype(jnp.float32) * (HEAD_DIM ** -0.5)
    bias = rel_bias[rel_bucket(q_pos, k_pos)].astype(jnp.float32)
    s = s + jnp.transpose(bias, (2, 0, 1))[None, :, None]
    visible = k_pos[None, :] <= q_pos[:, None]
    p = jax.nn.softmax(jnp.where(visible, s, -jnp.inf), axis=-1)
    w = p[:, :, 0] - lam * p[:, :, 1]
    return jnp.einsum('bhqk,bkhe->bqhe', w.astype(v.dtype), v)


def prompt_attention(q, k, v, rel_bias, lam):
    bsz, L = q.shape[:2]
    qb_len = min(Q_BLOCK, L)
    nqb = L // qb_len
    qb = jnp.moveaxis(q.reshape(bsz, nqb, qb_len, N_HEADS, 2, HEAD_DIM), 1, 0)
    starts = jnp.arange(nqb) * qb_len
    k_pos = jnp.arange(L)

    def blk(args):
        qi, st = args
        return diff_attend(qi, k, v, st + jnp.arange(qb_len), k_pos, rel_bias, lam)

    o = lax.map(blk, (qb, starts))
    return jnp.moveaxis(o, 0, 1).reshape(bsz, L, N_HEADS, 2 * HEAD_DIM)


def sample_attention(q, k_new, v_new, cache_k, cache_v, page_table, rel_bias, lam):
    past = page_table.shape[1] * PAGE_SIZE
    n_new = q.shape[1]
    q_pos = past + jnp.arange(n_new)
    k_pos = jnp.arange(past + n_new)

    def one(args):
        qi, kn, vn, pt = args
        kp = cache_k[pt].reshape(past, N_HEADS, 2, HEAD_DIM)
        vp = cache_v[pt].reshape(past, N_HEADS, 2 * HEAD_DIM)
        kall = jnp.concatenate([kp, kn.astype(kp.dtype)], axis=0)[None]
        vall = jnp.concatenate([vp, vn.astype(vp.dtype)], axis=0)[None]
        return diff_attend(qi[None].astype(kp.dtype), kall, vall, q_pos, k_pos, rel_bias, lam)[0]

    return lax.map(one, (q, k_new, v_new, page_table)).astype(q.dtype)


def shared_kv(s, norm_kv, w_kv, k_norm):
    k, v = jnp.split(rmsnorm(s, norm_kv) @ w_kv, 2, axis=-1)
    k = rmsnorm(k.reshape(s.shape[0], s.shape[1], N_HEADS, 2, HEAD_DIM), k_norm)
    return k, v.reshape(s.shape[0], s.shape[1], N_HEADS, 2 * HEAD_DIM)


def query(s, g_mix, w_q, q_norm):
    q = rmsnorm(s, g_mix) @ w_q
    return rmsnorm(q.reshape(s.shape[0], s.shape[1], N_HEADS, 2, HEAD_DIM), q_norm)


def diff_out(o, subln_g, lambda_init, w_o):
    o = rmsnorm(o, subln_g) * (1.0 - lambda_init)
    return o.reshape(o.shape[0], o.shape[1], D_MODEL) @ w_o


def setup_inputs(seed: int = 0) -> dict:
    key = jax.random.key(seed)
    ks = jax.random.split(key, 40)
    f32 = jnp.float32
    n_pages = PAST_LEN // PAGE_SIZE
    n_pool = (DEC_BATCH * n_pages * 5) // 4

    def nrm(k, shape, scale):
        return jax.random.normal(k, shape, f32) * scale

    def gain(k, shape):
        return 1.0 + 0.02 * jax.random.normal(k, shape, f32)

    dt0 = jnp.exp(jax.random.uniform(ks[10], (N_A_LAYERS, SSM_HEADS), f32, math.log(1e-3), math.log(1e-1)))
    return {
        'x_prompt': nrm(ks[0], (BATCH, SEQ, D_MODEL), 1.0),
        'x_sample': nrm(ks[1], (DEC_BATCH, DEC_SEQ, D_MODEL), 1.0),
        'state_conv': nrm(ks[2], (N_A_LAYERS, DEC_BATCH, D_CONV - 1, CONV_DIM), 1.0),
        'state_ssm': nrm(ks[3], (N_A_LAYERS, DEC_BATCH, SSM_HEADS, SSM_HEADDIM, D_STATE), 0.1),
        'cache_k': nrm(ks[4], (n_pool, PAGE_SIZE, N_HEADS, 2, HEAD_DIM), 1.0),
        'cache_v': nrm(ks[5], (n_pool, PAGE_SIZE, N_HEADS, 2 * HEAD_DIM), 1.0),
        'page_table': jax.random.permutation(ks[6], n_pool)[:DEC_BATCH * n_pages].reshape(DEC_BATCH, n_pages).astype(jnp.int32),
        'norm_mix': gain(ks[7], (DEPTH, D_MODEL)),
        'norm_ffn': gain(ks[8], (DEPTH, D_MODEL)),
        'w_in': nrm(ks[9], (N_A_LAYERS, D_MODEL, D_IN_PROJ), D_MODEL ** -0.5),
        'conv_w': nrm(ks[11], (N_A_LAYERS, D_CONV, CONV_DIM), D_CONV ** -0.5),
        'conv_b': nrm(ks[12], (N_A_LAYERS, CONV_DIM), 0.02),
        'dt_bias': dt0 + jnp.log(-jnp.expm1(-dt0)),
        'a_log': jnp.log(jax.random.uniform(ks[13], (N_A_LAYERS, SSM_HEADS), f32, 1.0, 16.0)),
        'd_skip': 1.0 + 0.1 * jax.random.normal(ks[14], (N_A_LAYERS, SSM_HEADS), f32),
        'ssm_norm': gain(ks[15], (N_A_LAYERS, D_INNER)),
        'w_out_ssm': nrm(ks[16], (N_A_LAYERS, D_INNER, D_MODEL), D_INNER ** -0.5),
        'norm_kv': gain(ks[17], (D_MODEL,)),
        'w_kv': nrm(ks[18], (D_MODEL, 2 * D_MODEL), D_MODEL ** -0.5),
        'k_norm': gain(ks[19], (HEAD_DIM,)),
        'w_q': nrm(ks[20], (N_B_LAYERS, D_MODEL, D_MODEL), D_MODEL ** -0.5),
        'q_norm': gain(ks[21], (N_B_LAYERS, HEAD_DIM)),
        'lambda_q1': nrm(ks[22], (N_B_LAYERS, HEAD_DIM), 0.1),
        'lambda_k1': nrm(ks[23], (N_B_LAYERS, HEAD_DIM), 0.1),
        'lambda_q2': nrm(ks[24], (N_B_LAYERS, HEAD_DIM), 0.1),
        'lambda_k2': nrm(ks[25], (N_B_LAYERS, HEAD_DIM), 0.1),
        'subln': gain(ks[26], (N_B_LAYERS, 2 * HEAD_DIM)),
        'w_o': nrm(ks[27], (N_B_LAYERS, D_MODEL, D_MODEL), D_MODEL ** -0.5),
        'rel_bias': nrm(ks[28], (N_BUCKETS, N_HEADS), 0.5),
        'w_gate_up': nrm(ks[29], (DEPTH, D_MODEL, 2 * D_FF), D_MODEL ** -0.5),
        'w_down': nrm(ks[30], (DEPTH, D_FF, D_MODEL), D_FF ** -0.5),
    }


def reference(x_prompt, x_sample, state_conv, state_ssm, cache_k, cache_v, page_table,
              norm_mix, norm_ffn, w_in, conv_w, conv_b, dt_bias, a_log, d_skip, ssm_norm, w_out_ssm,
              norm_kv, w_kv, k_norm, w_q, q_norm, lambda_q1, lambda_k1, lambda_q2, lambda_k2, subln, w_o,
              rel_bias, w_gate_up, w_down):
    xp, xs = x_prompt, x_sample
    conv_p, ssm_p, conv_s, ssm_s = [], [], [], []
    for l in range(DEPTH):
        if l < N_A_LAYERS:
            a_args = (w_in[l], conv_w[l], conv_b[l], dt_bias[l], a_log[l], d_skip[l], ssm_norm[l], w_out_ssm[l])
            zero_conv = jnp.zeros((xp.shape[0], D_CONV - 1, CONV_DIM), xp.dtype)
            zero_ssm = jnp.zeros((xp.shape[0], SSM_HEADS, SSM_HEADDIM, D_STATE), state_ssm.dtype)
            hp, cp, sp = mamba_mixer(rmsnorm(xp, norm_mix[l]), zero_conv, zero_ssm, *a_args)
            hs, cs, ss = mamba_mixer(rmsnorm(xs, norm_mix[l]), state_conv[l], state_ssm[l], *a_args)
            xp = xp + hp
            xs = xs + hs
            conv_p.append(cp)
            ssm_p.append(sp)
            conv_s.append(cs)
            ssm_s.append(ss)
        else:
            j = l - N_A_LAYERS
            if j == 0:
                k_p, v_p = shared_kv(xp, norm_kv, w_kv, k_norm)
                k_s, v_s = shared_kv(xs, norm_kv, w_kv, k_norm)
            lambda_init = 0.8 - 0.6 * math.exp(-0.3 * l)
            f32 = jnp.float32
            lam = (jnp.exp(jnp.sum(lambda_q1[j].astype(f32) * lambda_k1[j].astype(f32)))
                   - jnp.exp(jnp.sum(lambda_q2[j].astype(f32) * lambda_k2[j].astype(f32))) + lambda_init)
            q_p = query(xp, norm_mix[l], w_q[j], q_norm[j])
            q_s = query(xs, norm_mix[l], w_q[j], q_norm[j])
            o_p = prompt_attention(q_p, k_p, v_p, rel_bias, lam)
            o_s = sample_attention(q_s, k_s, v_s, cache_k, cache_v, page_table, rel_bias, lam)
            xp = xp + diff_out(o_p, subln[j], lambda_init, w_o[j])
            xs = xs + diff_out(o_s, subln[j], lambda_init, w_o[j])
        xp = xp + swiglu(rmsnorm(xp, norm_ffn[l]), w_gate_up[l], w_down[l])
        xs = xs + swiglu(rmsnorm(xs, norm_ffn[l]), w_gate_up[l], w_down[l])
    return (xp, xs, jnp.stack(conv_p), jnp.stack(ssm_p), k_p, v_p, jnp.stack(conv_s), jnp.stack(ssm_s), k_s, v_s)
```

```python
import functools
import math

import jax
import jax.numpy as jnp
from jax import lax
from jax.experimental import pallas as pl
from jax.experimental.pallas import tpu as pltpu

F32 = jnp.float32
BF16 = jnp.bfloat16

EPS = 1e-6
LANES = 128
HEAD_DIM = 64
PAIR = 2 * HEAD_DIM
SSD_CHUNK = 128
SSM_GROUPS = 4
CONV_TAPS = 4
HALO = 8
N_BUCKETS = 32
MAX_EXACT = N_BUCKETS // 2
MAX_DISTANCE = 128
PAGE_SIZE = 128
VMEM_LIMIT = 52 * 1024 * 1024


def _params(*semantics):
    return pltpu.CompilerParams(dimension_semantics=semantics, vmem_limit_bytes=VMEM_LIMIT)


def _row_tile(m, preferred):
    if m <= preferred:
        return m
    t = preferred
    while m % t or t % 16:
        t -= 1
    return t


def _nt_dot(a, b):
    return lax.dot_general(a, b, (((1,), (1,)), ((), ())), preferred_element_type=F32)


def _tn_dot(a, b):
    return lax.dot_general(a, b, (((0,), (0,)), ((), ())), preferred_element_type=F32)


def _rms_scale(x):
    return x * lax.rsqrt(jnp.mean(x * x, axis=-1, keepdims=True) + EPS)


def _softplus(x):
    return jnp.maximum(x, 0.0) + jnp.log1p(jnp.exp(-jnp.abs(x)))


def _lane_pair(arr, j, lo):
    return jnp.where(lo, arr[:, 2 * j:2 * j + 1], arr[:, 2 * j + 1:2 * j + 2])


def _bucket(n):
    n = jnp.maximum(n, 0)
    nf = jnp.maximum(n, 1).astype(F32)
    large = MAX_EXACT + (jnp.log(nf / MAX_EXACT) / math.log(MAX_DISTANCE / MAX_EXACT)
                         * (N_BUCKETS - MAX_EXACT)).astype(jnp.int32)
    large = jnp.minimum(large, N_BUCKETS - 1)
    return jnp.where(n < MAX_EXACT, n, large)


def _in_proj_kernel(x_ref, g_ref, w_ref, wdt_ref, zx_ref, dt_ref, xn_ref):
    @pl.when(pl.program_id(1) == 0)
    def _():
        xn_ref[...] = (_rms_scale(x_ref[...]) * g_ref[...]).astype(BF16)
        dt_ref[...] = jnp.dot(xn_ref[...], wdt_ref[...], preferred_element_type=F32)

    zx_ref[...] = jnp.dot(xn_ref[...], w_ref[...], preferred_element_type=F32)


def _in_proj(x, g, w_zx, w_dt):
    m, d = x.shape
    n = w_zx.shape[1]
    tm = _row_tile(m, 1024)
    tn = 1024
    return pl.pallas_call(
        _in_proj_kernel,
        grid=(m // tm, n // tn),
        in_specs=[pl.BlockSpec((tm, d), lambda i, j: (i, 0)),
                  pl.BlockSpec((1, d), lambda i, j: (0, 0)),
                  pl.BlockSpec((d, tn), lambda i, j: (0, j)),
                  pl.BlockSpec((d, LANES), lambda i, j: (0, 0))],
        out_specs=[pl.BlockSpec((tm, tn), lambda i, j: (i, j)),
                   pl.BlockSpec((tm, LANES), lambda i, j: (i, 0))],
        out_shape=[jax.ShapeDtypeStruct((m, n), F32), jax.ShapeDtypeStruct((m, LANES), F32)],
        scratch_shapes=[pltpu.VMEM((tm, d), BF16)],
        compiler_params=_params("parallel", "arbitrary"),
        name="in_proj",
    )(x, g, w_zx, w_dt)


def _ssd_kernel(z_ref, xs_ref, bc_ref, dt_ref, cw_ref, cb_ref, dtb_ref, alog_ref, dsk_ref, ng_ref,
                yg_ref, conv_ref, h_ref, xpad_ref, bcpad_ref, *, cl, d_inner, n_state):
    c = pl.program_id(1)
    n_pairs = d_inner // PAIR
    pairs_per_group = n_pairs // SSM_GROUPS
    gb = SSM_GROUPS * n_state

    @pl.when(c == 0)
    def _():
        xpad_ref[0:HALO, :] = jnp.zeros((HALO, d_inner), F32)
        bcpad_ref[0:HALO, :] = jnp.zeros((HALO, 2 * gb), F32)
        h_ref[...] = jnp.zeros_like(h_ref)

    xpad_ref[HALO:HALO + cl, :] = xs_ref[...]
    bcpad_ref[HALO:HALO + cl, :] = bc_ref[...]

    def conv(pad_ref, c0, width):
        acc = cb_ref[:, c0:c0 + width]
        for k in range(CONV_TAPS):
            lo = HALO - (CONV_TAPS - 1) + k
            acc = acc + cw_ref[k:k + 1, c0:c0 + width] * pad_ref[lo:lo + cl, :]
        return jax.nn.silu(acc)

    xc = conv(xpad_ref, 0, d_inner)
    bcc = conv(bcpad_ref, d_inner, 2 * gb)
    xpad_ref[0:HALO, :] = xpad_ref[cl:cl + HALO, :]
    bcpad_ref[0:HALO, :] = bcpad_ref[cl:cl + HALO, :]

    @pl.when(c == pl.num_programs(1) - 1)
    def _():
        conv_ref[0, :, 0:d_inner] = xs_ref[cl - (CONV_TAPS - 1):cl, :]
        conv_ref[0, :, d_inner:d_inner + 2 * gb] = bc_ref[cl - (CONV_TAPS - 1):cl, :]

    dt = _softplus(dt_ref[...] + dtb_ref[...])
    dta = dt * (-jnp.exp(alog_ref[...]))
    rows = lax.broadcasted_iota(jnp.int32, (cl, LANES), 0)
    a_cum = dta
    shift = 1
    while shift < cl:
        a_cum = a_cum + jnp.where(rows >= shift, pltpu.roll(a_cum, shift, 0), 0.0)
        shift *= 2
    a_cum_t = a_cum.T
    ea = jnp.exp(a_cum)
    dte = jnp.exp(a_cum[cl - 1:cl, :] - a_cum) * dt
    cd_col = jnp.exp(a_cum_t[:, cl - 1:cl])

    lo = lax.broadcasted_iota(jnp.int32, (1, PAIR), 1) < HEAD_DIM
    row_lo = lax.broadcasted_iota(jnp.int32, (PAIR, 1), 0) < HEAD_DIM
    causal = (lax.broadcasted_iota(jnp.int32, (cl, cl), 0)
              >= lax.broadcasted_iota(jnp.int32, (cl, cl), 1))

    for g in range(SSM_GROUPS):
        b_g = bcc[:, g * n_state:(g + 1) * n_state].astype(BF16)
        c_g = bcc[:, gb + g * n_state:gb + (g + 1) * n_state].astype(BF16)
        cb_g = _nt_dot(c_g, b_g)
        gated = []
        ssq = jnp.zeros((cl, 1), F32)
        for jj in range(pairs_per_group):
            j = g * pairs_per_group + jj
            sl = slice(j * PAIR, (j + 1) * PAIR)
            x_p = xc[:, sl]
            xdt = x_p * _lane_pair(dt, j, lo)
            y = None
            for half in range(2):
                r = 2 * j + half
                diff = a_cum[:, r:r + 1] - a_cum_t[r:r + 1, :]
                w = (cb_g * jnp.exp(jnp.where(causal, diff, -jnp.inf))).astype(BF16)
                keep = lo if half == 0 else jnp.logical_not(lo)
                part = jnp.dot(w, jnp.where(keep, xdt, 0.0).astype(BF16), preferred_element_type=F32)
                y = part if y is None else y + part
            h_p = h_ref[0, sl, :]
            y = y + _nt_dot(c_g, h_p.astype(BF16)) * _lane_pair(ea, j, lo)
            s_new = _tn_dot((x_p * _lane_pair(dte, j, lo)).astype(BF16), b_g)
            d_col = jnp.where(row_lo, cd_col[2 * j:2 * j + 1, :], cd_col[2 * j + 1:2 * j + 2, :])
            h_ref[0, sl, :] = d_col * h_p + s_new
            y = y + dsk_ref[:, sl] * x_p
            gy = y * jax.nn.silu(z_ref[:, sl])
            ssq = ssq + jnp.sum(gy * gy, axis=-1, keepdims=True)
            gated.append(gy)
        scale = lax.rsqrt(ssq / (pairs_per_group * PAIR) + EPS)
        for jj in range(pairs_per_group):
            sl = slice((g * pairs_per_group + jj) * PAIR, (g * pairs_per_group + jj + 1) * PAIR)
            yg_ref[:, sl] = (gated[jj] * scale * ng_ref[:, sl]).astype(BF16)


def _ssd_prompt(zx, dt_raw, conv_w, conv_b, dt_bias, a_log, d_skip, norm_g, bsz, seq, d_inner, n_state):
    cl = SSD_CHUNK
    assert seq % cl == 0
    nc = seq // cl
    gb = SSM_GROUPS * n_state
    conv_dim = d_inner + 2 * gb
    xs_blk = d_inner // d_inner
    bc_blk = (2 * d_inner) // (2 * gb)
    assert (2 * d_inner) % (2 * gb) == 0
    kern = functools.partial(_ssd_kernel, cl=cl, d_inner=d_inner, n_state=n_state)
    row = lambda b, c: b * nc + c
    const = lambda b, c: (0, 0)
    return pl.pallas_call(
        kern,
        grid=(bsz, nc),
        in_specs=[pl.BlockSpec((cl, d_inner), lambda b, c: (row(b, c), 0)),
                  pl.BlockSpec((cl, d_inner), lambda b, c: (row(b, c), xs_blk)),
                  pl.BlockSpec((cl, 2 * gb), lambda b, c: (row(b, c), bc_blk)),
                  pl.BlockSpec((cl, LANES), lambda b, c: (row(b, c), 0)),
                  pl.BlockSpec((CONV_TAPS, conv_dim), const),
                  pl.BlockSpec((1, conv_dim), const),
                  pl.BlockSpec((1, LANES), const),
                  pl.BlockSpec((1, LANES), const),
                  pl.BlockSpec((1, d_inner), const),
                  pl.BlockSpec((1, d_inner), const)],
        out_specs=[pl.BlockSpec((cl, d_inner), lambda b, c: (row(b, c), 0)),
                   pl.BlockSpec((1, CONV_TAPS - 1, conv_dim), lambda b, c: (b, 0, 0)),
                   pl.BlockSpec((1, d_inner, n_state), lambda b, c: (b, 0, 0))],
        out_shape=[jax.ShapeDtypeStruct((bsz * seq, d_inner), BF16),
                   jax.ShapeDtypeStruct((bsz, CONV_TAPS - 1, conv_dim), F32),
                   jax.ShapeDtypeStruct((bsz, d_inner, n_state), F32)],
        scratch_shapes=[pltpu.VMEM((HALO + cl, d_inner), F32), pltpu.VMEM((HALO + cl, 2 * gb), F32)],
        compiler_params=_params("parallel", "arbitrary"),
        name="ssd_prompt",
    )(zx, zx, zx, dt_raw, conv_w, conv_b, dt_bias, a_log, d_skip, norm_g)


def _ssd_step_kernel(zx_ref, dt_ref, cs_ref, h_ref, cw_ref, cb_ref, dtb_ref, alog_ref, dsk_ref, ng_ref,
                     yg_ref, cso_ref, ho_ref, *, d_inner, n_state):
    n_pairs = d_inner // PAIR
    pairs_per_group = n_pairs // SSM_GROUPS
    gb = SSM_GROUPS * n_state
    zx = zx_ref[0]
    z = zx[:, 0:d_inner]
    xbc = zx[:, d_inner:]
    prev = cs_ref[0]
    acc = cb_ref[...] + cw_ref[CONV_TAPS - 1:CONV_TAPS, :] * xbc
    for k in range(CONV_TAPS - 1):
        acc = acc + cw_ref[k:k + 1, :] * prev[k:k + 1, :]
    cso_ref[0, 0:CONV_TAPS - 2, :] = prev[1:CONV_TAPS - 1, :]
    cso_ref[0, CONV_TAPS - 2:CONV_TAPS - 1, :] = xbc
    act = jax.nn.silu(acc)
    xs = act[:, 0:d_inner]
    dt = _softplus(dt_ref[0] + dtb_ref[...])
    da = jnp.exp(dt * (-jnp.exp(alog_ref[...])))

    lo = lax.broadcasted_iota(jnp.int32, (1, PAIR), 1) < HEAD_DIM
    row_lo = lax.broadcasted_iota(jnp.int32, (PAIR, 1), 0) < HEAD_DIM
    eye = (lax.broadcasted_iota(jnp.int32, (PAIR, PAIR), 0)
           == lax.broadcasted_iota(jnp.int32, (PAIR, PAIR), 1))

    for g in range(SSM_GROUPS):
        b_g = act[:, d_inner + g * n_state:d_inner + (g + 1) * n_state]
        c_g = act[:, d_inner + gb + g * n_state:d_inner + gb + (g + 1) * n_state]
        gated = []
        ssq = jnp.zeros((1, 1), F32)
        for jj in range(pairs_per_group):
            j = g * pairs_per_group + jj
            sl = slice(j * PAIR, (j + 1) * PAIR)
            x_p = xs[:, sl]
            xdt = x_p * _lane_pair(dt, j, lo)
            x_col = jnp.sum(jnp.where(eye, jnp.broadcast_to(xdt, (PAIR, PAIR)), 0.0), axis=1, keepdims=True)
            d_col = jnp.where(row_lo, da[:, 2 * j:2 * j + 1], da[:, 2 * j + 1:2 * j + 2])
            h_new = d_col * h_ref[0, sl, :] + x_col * b_g
            ho_ref[0, sl, :] = h_new
            y_col = jnp.sum(h_new * c_g, axis=1, keepdims=True)
            y = jnp.sum(jnp.where(eye, jnp.broadcast_to(y_col, (PAIR, PAIR)), 0.0), axis=0, keepdims=True)
            y = y + dsk_ref[:, sl] * x_p
            gy = y * jax.nn.silu(z[:, sl])
            ssq = ssq + jnp.sum(gy * gy, axis=-1, keepdims=True)
            gated.append(gy)
        scale = lax.rsqrt(ssq / (pairs_per_group * PAIR) + EPS)
        for jj in range(pairs_per_group):
            sl = slice((g * pairs_per_group + jj) * PAIR, (g * pairs_per_group + jj + 1) * PAIR)
            yg_ref[0, :, sl] = gated[jj] * scale * ng_ref[:, sl]


def _ssd_step(zx, dt_raw, state_conv, state_ssm, conv_w, conv_b, dt_bias, a_log, d_skip, norm_g,
              d_inner, n_state):
    bsz = zx.shape[0]
    conv_dim = conv_w.shape[1]
    kern = functools.partial(_ssd_step_kernel, d_inner=d_inner, n_state=n_state)
    const = lambda b: (0, 0)
    per_b = lambda b: (b, 0, 0)
    return pl.pallas_call(
        kern,
        grid=(bsz,),
        in_specs=[pl.BlockSpec((1, 1, d_inner + conv_dim), per_b),
                  pl.BlockSpec((1, 1, LANES), per_b),
                  pl.BlockSpec((1, CONV_TAPS - 1, conv_dim), per_b),
                  pl.BlockSpec((1, d_inner, n_state), per_b),
                  pl.BlockSpec((CONV_TAPS, conv_dim), const),
                  pl.BlockSpec((1, conv_dim), const),
                  pl.BlockSpec((1, LANES), const),
                  pl.BlockSpec((1, LANES), const),
                  pl.BlockSpec((1, d_inner), const),
                  pl.BlockSpec((1, d_inner), const)],
        out_specs=[pl.BlockSpec((1, 1, d_inner), per_b),
                   pl.BlockSpec((1, CONV_TAPS - 1, conv_dim), per_b),
                   pl.BlockSpec((1, d_inner, n_state), per_b)],
        out_shape=[jax.ShapeDtypeStruct((bsz, 1, d_inner), F32),
                   jax.ShapeDtypeStruct((bsz, CONV_TAPS - 1, conv_dim), F32),
                   jax.ShapeDtypeStruct((bsz, d_inner, n_state), F32)],
        compiler_params=_params("parallel"),
        name="ssd_step",
    )(zx.reshape(bsz, 1, -1), dt_raw.reshape(bsz, 1, LANES), state_conv,
      state_ssm.reshape(bsz, d_inner, n_state), conv_w, conv_b, dt_bias, a_log, d_skip, norm_g)


def _linear_res_kernel(a_ref, w_ref, r_ref, o_ref):
    o_ref[...] = r_ref[...] + jnp.dot(a_ref[...].astype(BF16), w_ref[...], preferred_element_type=F32)


def _linear_res(a, w, res):
    m, k = a.shape
    n = w.shape[1]
    tm = _row_tile(m, 512)
    return pl.pallas_call(
        _linear_res_kernel,
        grid=(m // tm,),
        in_specs=[pl.BlockSpec((tm, k), lambda i: (i, 0)),
                  pl.BlockSpec((k, n), lambda i: (0, 0)),
                  pl.BlockSpec((tm, n), lambda i: (i, 0))],
        out_specs=pl.BlockSpec((tm, n), lambda i: (i, 0)),
        out_shape=jax.ShapeDtypeStruct((m, n), F32),
        compiler_params=_params("parallel"),
        name="linear_res",
    )(a, w, res)


def _ffn_kernel(x_ref, g_ref, wg_ref, wu_ref, wd_ref, o_ref, xn_ref):
    @pl.when(pl.program_id(1) == 0)
    def _():
        x = x_ref[...]
        xn_ref[...] = (_rms_scale(x) * g_ref[...]).astype(BF16)
        o_ref[...] = x

    xn = xn_ref[...]
    gate = jnp.dot(xn, wg_ref[...], preferred_element_type=F32)
    up = jnp.dot(xn, wu_ref[...], preferred_element_type=F32)
    act = (jax.nn.silu(gate) * up).astype(BF16)
    o_ref[...] += jnp.dot(act, wd_ref[...], preferred_element_type=F32)


def _ffn(x, g, w_gate_up, w_down):
    m, d = x.shape
    d_ff = w_down.shape[0]
    tm = _row_tile(m, 1024)
    tf = 256
    assert d_ff % tf == 0
    nf = d_ff // tf
    return pl.pallas_call(
        _ffn_kernel,
        grid=(m // tm, nf),
        in_specs=[pl.BlockSpec((tm, d), lambda i, j: (i, 0)),
                  pl.BlockSpec((1, d), lambda i, j: (0, 0)),
                  pl.BlockSpec((d, tf), lambda i, j: (0, j)),
                  pl.BlockSpec((d, tf), lambda i, j: (0, nf + j)),
                  pl.BlockSpec((tf, d), lambda i, j: (j, 0))],
        out_specs=pl.BlockSpec((tm, d), lambda i, j: (i, 0)),
        out_shape=jax.ShapeDtypeStruct((m, d), F32),
        scratch_shapes=[pltpu.VMEM((tm, d), BF16)],
        compiler_params=_params("parallel", "arbitrary"),
        name="ffn",
    )(x, g, w_gate_up, w_gate_up, w_down)


def _qkv_kernel(x_ref, gq_ref, gkv_ref, w_ref, qn_ref, kn_ref, knc_ref, q_ref, k_ref, v4_ref, vd_ref,
                xq_ref, xkv_ref, *, n_heads, k_transposed):
    j = pl.program_id(1)
    lo = lax.broadcasted_iota(jnp.int32, (1, PAIR), 1) < HEAD_DIM

    def qk_norm(y, gain_ref, out_ref, post):
        for h in range(n_heads):
            yh = y[:, h * PAIR:(h + 1) * PAIR]
            sq = yh * yh
            s_lo = jnp.sum(jnp.where(lo, sq, 0.0), axis=-1, keepdims=True)
            s_hi = jnp.sum(jnp.where(lo, 0.0, sq), axis=-1, keepdims=True)
            r = jnp.where(lo, lax.rsqrt(s_lo / HEAD_DIM + EPS), lax.rsqrt(s_hi / HEAD_DIM + EPS))
            out_ref[:, h * PAIR:(h + 1) * PAIR] = (yh * r * gain_ref[...] * post).astype(out_ref.dtype)

    @pl.when(j == 0)
    def _():
        xh = _rms_scale(x_ref[...])
        xq_ref[...] = (xh * gq_ref[...]).astype(BF16)
        xkv_ref[...] = (xh * gkv_ref[...]).astype(BF16)
        qk_norm(jnp.dot(xq_ref[...], w_ref[0], preferred_element_type=F32), qn_ref, q_ref,
                HEAD_DIM ** -0.5)

    @pl.when(j == 1)
    def _():
        if k_transposed:
            kt = _nt_dot(w_ref[0], xkv_ref[...])
            for g in range(2 * n_heads):
                rows = slice(g * HEAD_DIM, (g + 1) * HEAD_DIM)
                blk = kt[rows, :]
                r = lax.rsqrt(jnp.mean(blk * blk, axis=0, keepdims=True) + EPS)
                k_ref[0, rows, :] = blk * r * knc_ref[...]
        else:
            qk_norm(jnp.dot(xkv_ref[...], w_ref[0], preferred_element_type=F32), kn_ref, k_ref, 1.0)

    @pl.when(j == 2)
    def _():
        v = jnp.dot(xkv_ref[...], w_ref[0], preferred_element_type=F32)
        vd_ref[...] = v.astype(BF16)
        for h in range(n_heads):
            v4_ref[:, h, :] = v[:, h * PAIR:(h + 1) * PAIR]


def _qkv(x, g_q, g_kv, w_q, w_k, w_v, q_norm, k_norm, seq=None):
    m, d = x.shape
    n_heads = d // PAIR
    k_transposed = seq is not None
    tm = _row_tile(m if seq is None else seq, 512)
    w = jnp.stack([w_q, w_k.T if k_transposed else w_k, w_v]).astype(BF16)
    pair_gain = jnp.tile(k_norm.astype(F32), 2).reshape(1, PAIR)
    kern = functools.partial(_qkv_kernel, n_heads=n_heads, k_transposed=k_transposed)
    const = lambda i, j: (0, 0)
    rowblk = lambda i, j: (i, 0)
    if k_transposed:
        per_seq = seq // tm
        k_spec = pl.BlockSpec((1, d, tm), lambda i, j: (i // per_seq, 0, i % per_seq))
        k_shape = jax.ShapeDtypeStruct((m // seq, d, seq), F32)
    else:
        k_spec = pl.BlockSpec((tm, d), rowblk)
        k_shape = jax.ShapeDtypeStruct((m, d), F32)
    return pl.pallas_call(
        kern,
        grid=(m // tm, 3),
        in_specs=[pl.BlockSpec((tm, d), rowblk),
                  pl.BlockSpec((1, d), const),
                  pl.BlockSpec((1, d), const),
                  pl.BlockSpec((1, d, d), lambda i, j: (j, 0, 0)),
                  pl.BlockSpec((1, PAIR), const),
                  pl.BlockSpec((1, PAIR), const),
                  pl.BlockSpec((HEAD_DIM, 1), const)],
        out_specs=[pl.BlockSpec((tm, d), rowblk), k_spec,
                   pl.BlockSpec((tm, n_heads, PAIR), lambda i, j: (i, 0, 0)),
                   pl.BlockSpec((tm, d), rowblk)],
        out_shape=[jax.ShapeDtypeStruct((m, d), BF16), k_shape,
                   jax.ShapeDtypeStruct((m, n_heads, PAIR), F32),
                   jax.ShapeDtypeStruct((m, d), BF16)],
        scratch_shapes=[pltpu.VMEM((tm, d), BF16), pltpu.VMEM((tm, d), BF16)],
        compiler_params=_params("parallel", "arbitrary"),
        name="qkv_proj",
    )(x, g_q, g_kv, w, jnp.tile(q_norm.astype(F32), 2).reshape(1, PAIR), pair_gain,
      k_norm.astype(F32).reshape(HEAD_DIM, 1))


def _bias_tiles_kernel(rb_ref, o_ref, *, t):
    h = pl.program_id(0)
    i = lax.broadcasted_iota(jnp.int32, (t, t), 0)
    j = lax.broadcasted_iota(jnp.int32, (t, t), 1)
    for off in range(2):
        n = i - j + off * t
        bucket = _bucket(n)
        tile = jnp.zeros((t, t), F32)
        for b in range(N_BUCKETS):
            tile = jnp.where(bucket == b, rb_ref[b, h], tile)
        o_ref[0, off] = jnp.where(n >= 0, tile, -jnp.inf)


def _bias_tiles(rel_bias, t):
    n_heads = rel_bias.shape[1]
    return pl.pallas_call(
        functools.partial(_bias_tiles_kernel, t=t),
        grid=(n_heads,),
        in_specs=[pl.BlockSpec(memory_space=pltpu.SMEM)],
        out_specs=pl.BlockSpec((1, 2, t, t), lambda h: (h, 0, 0, 0)),
        out_shape=jax.ShapeDtypeStruct((n_heads, 2, t, t), F32),
        compiler_params=_params("parallel"),
        name="bias_tiles",
    )(rel_bias)


def _lambda_value(lam_ref, lambda_init):
    s1 = jnp.sum(lam_ref[0:1, :] * lam_ref[1:2, :], axis=-1, keepdims=True)
    s2 = jnp.sum(lam_ref[2:3, :] * lam_ref[3:4, :], axis=-1, keepdims=True)
    return jnp.exp(s1) - jnp.exp(s2) + lambda_init


def _attn_kernel(q_ref, k_ref, v_ref, bias_ref, rb_ref, lam_ref, sg_ref, o_ref,
                 kb_ref, m_ref, l_ref, acc_ref, *, t, nq, lambda_init):
    h = pl.program_id(1)
    qi = pl.program_id(2)

    @pl.when(qi == 0)
    def _():
        for ki in range(nq):
            kb_ref[ki] = k_ref[0, :, ki * t:(ki + 1) * t].astype(BF16)

    q = q_ref[0]
    lo = lax.broadcasted_iota(jnp.int32, (t, PAIR), 1) < HEAD_DIM
    zero = jnp.zeros_like(q)
    q12 = jnp.concatenate([jnp.where(lo, q, zero), jnp.where(lo, zero, q)], axis=0)

    m_ref[...] = jnp.full(m_ref.shape, -jnp.inf, F32)
    l_ref[...] = jnp.zeros(l_ref.shape, F32)
    acc_ref[...] = jnp.zeros(acc_ref.shape, F32)

    def step(ki, bias):
        vb = v_ref[0, pl.ds(pl.multiple_of(ki * t, t), t), :]
        s = jnp.dot(q12, kb_ref[ki], preferred_element_type=F32) + bias
        m_prev = m_ref[...]
        m_new = jnp.maximum(m_prev, jnp.max(s, axis=-1, keepdims=True))
        alpha = jnp.exp(m_prev - m_new)
        p = jnp.exp(s - m_new)
        l_ref[...] = alpha * l_ref[...] + jnp.sum(p, axis=-1, keepdims=True)
        acc_ref[...] = alpha * acc_ref[...] + jnp.dot(p.astype(BF16), vb, preferred_element_type=F32)
        m_ref[...] = m_new

    far_bias = rb_ref[N_BUCKETS - 1, h]

    def far_body(ki, carry):
        step(ki, far_bias)
        return carry

    lax.fori_loop(0, jnp.maximum(qi - 1, 0), far_body, 0)

    @pl.when(qi >= 1)
    def _():
        tile = bias_ref[0, 1]
        step(qi - 1, jnp.concatenate([tile, tile], axis=0))

    tile = bias_ref[0, 0]
    step(qi, jnp.concatenate([tile, tile], axis=0))

    o1 = acc_ref[0:t, :] / l_ref[0:t, :]
    o2 = acc_ref[t:2 * t, :] / l_ref[t:2 * t, :]
    o = o1 - _lambda_value(lam_ref, lambda_init) * o2
    o_ref[0] = (_rms_scale(o) * sg_ref[...] * (1.0 - lambda_init)).astype(o_ref.dtype)


def _prompt_attention(q, k_t, v, rel_bias, lam_rows, subln, lambda_init, bsz, seq):
    d = q.shape[-1]
    n_heads = d // PAIR
    t = min(seq, 512)
    assert seq % t == 0 and t >= MAX_DISTANCE
    nq = seq // t
    tiles = _bias_tiles(rel_bias, t)
    kern = functools.partial(_attn_kernel, t=t, nq=nq, lambda_init=lambda_init)
    const = lambda b, h, i: (0, 0)
    return pl.pallas_call(
        kern,
        grid=(bsz, n_heads, nq),
        in_specs=[pl.BlockSpec((1, t, PAIR), lambda b, h, i: (b, i, h)),
                  pl.BlockSpec((1, PAIR, seq), lambda b, h, i: (b, h, 0)),
                  pl.BlockSpec((1, seq, PAIR), lambda b, h, i: (b, 0, h)),
                  pl.BlockSpec((1, 2, t, t), lambda b, h, i: (h, 0, 0, 0)),
                  pl.BlockSpec(memory_space=pltpu.SMEM),
                  pl.BlockSpec((4, LANES), const),
                  pl.BlockSpec((1, PAIR), const)],
        out_specs=pl.BlockSpec((1, t, PAIR), lambda b, h, i: (b, i, h)),
        out_shape=jax.ShapeDtypeStruct((bsz, seq, d), BF16),
        scratch_shapes=[pltpu.VMEM((nq, PAIR, t), BF16),
                        pltpu.VMEM((2 * t, 1), F32), pltpu.VMEM((2 * t, 1), F32),
                        pltpu.VMEM((2 * t, PAIR), F32)],
        compiler_params=_params("parallel", "parallel", "arbitrary"),
        name="prompt_attention",
    )(q.reshape(bsz, seq, d), k_t, v.reshape(bsz, seq, d), tiles, rel_bias, lam_rows, subln)


def _sample_attn_kernel(pt_ref, q_ref, kn_ref, vn_ref, *rest, pages_per_step, past, n_heads, lambda_init):
    k_refs = rest[0:pages_per_step]
    v_refs = rest[pages_per_step:2 * pages_per_step]
    rb_ref, lam_ref, sg_ref, o_ref, qb_ref, m_ref, l_ref, acc_ref = rest[2 * pages_per_step:]
    step_id = pl.program_id(1)
    nrow = 2 * n_heads
    d = n_heads * PAIR

    @pl.when(step_id == 0)
    def _():
        q = q_ref[0].astype(F32)
        row = lax.broadcasted_iota(jnp.int32, (nrow, d), 0)
        col = lax.broadcasted_iota(jnp.int32, (nrow, d), 1)
        qblk = jnp.where(jnp.right_shift(col, 6) == row, jnp.broadcast_to(q, (nrow, d)), 0.0)
        qb_ref[...] = qblk.astype(BF16)
        m_ref[...] = jnp.sum(qblk * kn_ref[0], axis=-1, keepdims=True) + rb_ref[:, 0:1]
        l_ref[...] = jnp.ones(l_ref.shape, F32)
        vn = vn_ref[0].astype(F32)
        for h in range(n_heads):
            acc_ref[2 * h:2 * h + 2, :] = jnp.broadcast_to(vn[:, h * PAIR:(h + 1) * PAIR], (2, PAIR))

    key = lax.broadcasted_iota(jnp.int32, (nrow, PAGE_SIZE), 1)
    row_head = jnp.right_shift(lax.broadcasted_iota(jnp.int32, (nrow, PAIR), 0), 1)
    for g in range(pages_per_step):
        page = step_id * pages_per_step + g
        bucket = _bucket(past - (page * PAGE_SIZE + key))
        bias = jnp.zeros((nrow, PAGE_SIZE), F32)
        for b in range(N_BUCKETS):
            bias = jnp.where(bucket == b, rb_ref[:, b:b + 1], bias)
        s = jnp.dot(qb_ref[...], k_refs[g][0].astype(BF16), preferred_element_type=F32) + bias
        m_prev = m_ref[...]
        m_new = jnp.maximum(m_prev, jnp.max(s, axis=-1, keepdims=True))
        alpha = jnp.exp(m_prev - m_new)
        p = jnp.exp(s - m_new)
        l_ref[...] = alpha * l_ref[...] + jnp.sum(p, axis=-1, keepdims=True)
        pb = p.astype(BF16)
        pv = jnp.zeros((nrow, PAIR), F32)
        for h in range(n_heads):
            v_h = v_refs[g][0, :, h, :].astype(BF16)
            pv = jnp.where(row_head == h, jnp.dot(pb, v_h, preferred_element_type=F32), pv)
        acc_ref[...] = alpha * acc_ref[...] + pv
        m_ref[...] = m_new

    @pl.when(step_id == pl.num_programs(1) - 1)
    def _():
        rowc = lax.broadcasted_iota(jnp.int32, (nrow, 1), 0)
        lam = _lambda_value(lam_ref, lambda_init)
        coef = jnp.where(jnp.bitwise_and(rowc, 1) == 0, 1.0, -lam) / l_ref[...]
        scaled = acc_ref[...] * coef
        o = scaled + pltpu.roll(scaled, nrow - 1, 0)
        on = _rms_scale(o) * sg_ref[...] * (1.0 - lambda_init)
        for h in range(n_heads):
            o_ref[0, :, h * PAIR:(h + 1) * PAIR] = on[2 * h:2 * h + 1, :]


def _sample_attention(q, k_new, v_new, cache_k, cache_v, page_table, rel_rows, lam_rows, subln, lambda_init):
    bsz, d = q.shape
    n_heads = d // PAIR
    n_pages = page_table.shape[1]
    past = n_pages * PAGE_SIZE
    pages_per_step = 4 if n_pages % 4 == 0 else 1
    n_pool = cache_k.shape[0]
    kern = functools.partial(_sample_attn_kernel, pages_per_step=pages_per_step, past=past,
                             n_heads=n_heads, lambda_init=lambda_init)
    per_b = lambda b, s, pt: (b, 0, 0)
    const = lambda b, s, pt: (0, 0)

    def k_spec(g):
        return pl.BlockSpec((1, d, PAGE_SIZE), lambda b, s, pt: (pt[b, s * pages_per_step + g], 0, 0))

    def v_spec(g):
        return pl.BlockSpec((1, PAGE_SIZE, n_heads, PAIR),
                            lambda b, s, pt: (pt[b, s * pages_per_step + g], 0, 0, 0))

    grid_spec = pltpu.PrefetchScalarGridSpec(
        num_scalar_prefetch=1,
        grid=(bsz, n_pages // pages_per_step),
        in_specs=([pl.BlockSpec((1, 1, d), per_b)] * 3
                  + [k_spec(g) for g in range(pages_per_step)]
                  + [v_spec(g) for g in range(pages_per_step)]
                  + [pl.BlockSpec((2 * n_heads, N_BUCKETS), const),
                     pl.BlockSpec((4, LANES), const),
                     pl.BlockSpec((1, PAIR), const)]),
        out_specs=pl.BlockSpec((1, 1, d), per_b),
        scratch_shapes=[pltpu.VMEM((2 * n_heads, d), BF16),
                        pltpu.VMEM((2 * n_heads, 1), F32), pltpu.VMEM((2 * n_heads, 1), F32),
                        pltpu.VMEM((2 * n_heads, PAIR), F32)],
    )
    ck = jnp.transpose(cache_k, (0, 2, 3, 4, 1)).reshape(n_pool, d, PAGE_SIZE)
    cv = cache_v
    return pl.pallas_call(
        kern,
        grid_spec=grid_spec,
        out_shape=jax.ShapeDtypeStruct((bsz, 1, d), F32),
        compiler_params=_params("parallel", "arbitrary"),
        name="sample_attention",
    )(page_table, q.reshape(bsz, 1, d), k_new.reshape(bsz, 1, d), v_new.reshape(bsz, 1, d),
      *([ck] * pages_per_step), *([cv] * pages_per_step), rel_rows, lam_rows, subln).reshape(bsz, d)


def _pad_lanes(v):
    return jnp.pad(v.astype(F32), (0, LANES - v.shape[0])).reshape(1, LANES)


def kernel(x_prompt, x_sample, state_conv, state_ssm, cache_k, cache_v, page_table, norm_mix, norm_ffn, w_in, conv_w, conv_b, dt_bias, a_log, d_skip, ssm_norm, w_out_ssm, norm_kv, w_kv, k_norm, w_q, q_norm, lambda_q1, lambda_k1, lambda_q2, lambda_k2, subln, w_o, rel_bias, w_gate_up, w_down):
    bsz, seq, d = x_prompt.shape
    dec = x_sample.shape[0]
    assert x_sample.shape[1] == 1
    n_ssm_layers, ssm_heads = dt_bias.shape
    depth = norm_mix.shape[0]
    assert n_ssm_layers == 1 and depth == 2
    d_inner = w_out_ssm.shape[1]
    n_state = state_ssm.shape[-1]
    assert d_inner == ssm_heads * HEAD_DIM and n_state == LANES and ssm_heads <= LANES
    n_heads = d // PAIR

    xp = x_prompt.reshape(bsz * seq, d)
    xs = x_sample.reshape(dec, d)
    row = lambda v: v.astype(F32).reshape(1, -1)

    zx_cols = 2 * d_inner + 2 * SSM_GROUPS * n_state
    w_zx = w_in[0][:, :zx_cols].astype(BF16)
    w_dt = jnp.pad(w_in[0][:, zx_cols:], ((0, 0), (0, LANES - ssm_heads))).astype(BF16)
    g_mix0 = row(norm_mix[0])
    ssd_args = (conv_w[0], row(conv_b[0]), _pad_lanes(dt_bias[0]), _pad_lanes(a_log[0]),
                row(jnp.repeat(d_skip[0], HEAD_DIM)), row(ssm_norm[0]))
    w_out = w_out_ssm[0].astype(BF16)
    w_gu0, w_dn0 = w_gate_up[0].astype(BF16), w_down[0].astype(BF16)

    zx_p, dt_p = _in_proj(xp, g_mix0, w_zx, w_dt)
    yg_p, conv_p, ssm_p = _ssd_prompt(zx_p, dt_p, *ssd_args, bsz, seq, d_inner, n_state)
    xp = _linear_res(yg_p, w_out, xp)
    xp = _ffn(xp, row(norm_ffn[0]), w_gu0, w_dn0)

    zx_s, dt_s = _in_proj(xs, g_mix0, w_zx, w_dt)
    yg_s, conv_s, ssm_s = _ssd_step(zx_s, dt_s, state_conv[0], state_ssm[0], *ssd_args, d_inner, n_state)
    xs = _linear_res(yg_s.reshape(dec, d_inner), w_out, xs)
    xs = _ffn(xs, row(norm_ffn[0]), w_gu0, w_dn0)

    lambda_init = 0.8 - 0.6 * math.exp(-0.3 * 1)
    qkv_args = (row(norm_mix[1]), row(norm_kv), w_q[0], w_kv[:, :d], w_kv[:, d:], q_norm[0], k_norm)
    lam_rows = jnp.pad(jnp.stack([lambda_q1[0], lambda_k1[0], lambda_q2[0], lambda_k2[0]]).astype(F32),
                       ((0, 0), (0, LANES - HEAD_DIM)))
    sg = row(subln[0])
    w_oo = w_o[0].astype(BF16)
    w_gu1, w_dn1 = w_gate_up[1].astype(BF16), w_down[1].astype(BF16)

    q_p, kt_p, v_p, vd_p = _qkv(xp, *qkv_args, seq=seq)
    o_p = _prompt_attention(q_p, kt_p, vd_p, rel_bias.astype(F32), lam_rows, sg, lambda_init, bsz, seq)
    xp = _linear_res(o_p.reshape(bsz * seq, d), w_oo, xp)
    xp = _ffn(xp, row(norm_ffn[1]), w_gu1, w_dn1)

    q_s, k_s, v_s, vd_s = _qkv(xs, *qkv_args)
    rel_rows = jnp.repeat(rel_bias.astype(F32).T, 2, axis=0)
    o_s = _sample_attention(q_s.astype(F32), k_s, vd_s.astype(F32), cache_k, cache_v, page_table,
                            rel_rows, lam_rows, sg, lambda_init)
    xs = _linear_res(o_s, w_oo, xs)
    xs = _ffn(xs, row(norm_ffn[1]), w_gu1, w_dn1)

    k_p = jnp.transpose(kt_p.reshape(bsz, n_heads, 2, HEAD_DIM, seq), (0, 4, 1, 2, 3))
    return (xp.reshape(bsz, seq, d), xs.reshape(dec, 1, d),
            conv_p[None], ssm_p.reshape(1, bsz, ssm_heads, HEAD_DIM, n_state),
            k_p, v_p.reshape(bsz, seq, n_heads, PAIR),
            conv_s[None], ssm_s.reshape(1, dec, ssm_heads, HEAD_DIM, n_state),
            k_s.reshape(dec, 1, n_heads, 2, HEAD_DIM), v_s.reshape(dec, 1, n_heads, PAIR))
```

```python
import functools
import math

import jax
import jax.numpy as jnp
from jax import lax
from jax.experimental import pallas as pl
from jax.experimental.pallas import tpu as pltpu

F32 = jnp.float32
BF16 = jnp.bfloat16

EPS = 1e-6
LANES = 128
HEAD_DIM = 64
PAIR = 2 * HEAD_DIM
SSD_CHUNK = 128
SSM_GROUPS = 4
CONV_TAPS = 4
HALO = 8
N_BUCKETS = 32
MAX_EXACT = N_BUCKETS // 2
MAX_DISTANCE = 128
PAGE_SIZE = 128
VMEM_LIMIT = 52 * 1024 * 1024


def _params(*semantics):
    return pltpu.CompilerParams(dimension_semantics=semantics, vmem_limit_bytes=VMEM_LIMIT)


def _row_tile(m, preferred):
    if m <= preferred:
        return m
    t = preferred
    while m % t or t % 16:
        t -= 1
    return t


def _nt_dot(a, b):
    return lax.dot_general(a, b, (((1,), (1,)), ((), ())), preferred_element_type=F32)


def _tn_dot(a, b):
    return lax.dot_general(a, b, (((0,), (0,)), ((), ())), preferred_element_type=F32)


def _rms_scale(x):
    return x * lax.rsqrt(jnp.mean(x * x, axis=-1, keepdims=True) + EPS)


def _softplus(x):
    return jnp.maximum(x, 0.0) + jnp.log1p(jnp.exp(-jnp.abs(x)))


def _lane_pair(arr, j, lo):
    return jnp.where(lo, arr[:, 2 * j:2 * j + 1], arr[:, 2 * j + 1:2 * j + 2])


def _bucket(n):
    n = jnp.maximum(n, 0)
    nf = jnp.maximum(n, 1).astype(F32)
    large = MAX_EXACT + (jnp.log(nf / MAX_EXACT) / math.log(MAX_DISTANCE / MAX_EXACT)
                         * (N_BUCKETS - MAX_EXACT)).astype(jnp.int32)
    large = jnp.minimum(large, N_BUCKETS - 1)
    return jnp.where(n < MAX_EXACT, n, large)


def _in_proj_kernel(x_ref, g_ref, w_ref, wdt_ref, zx_ref, dt_ref, xn_ref):
    @pl.when(pl.program_id(1) == 0)
    def _():
        xn_ref[...] = (_rms_scale(x_ref[...]) * g_ref[...]).astype(BF16)
        dt_ref[...] = jnp.dot(xn_ref[...], wdt_ref[...], preferred_element_type=F32)

    zx_ref[...] = jnp.dot(xn_ref[...], w_ref[...], preferred_element_type=F32)


def _in_proj(x, g, w_zx, w_dt):
    m, d = x.shape
    n = w_zx.shape[1]
    tm = _row_tile(m, 1024)
    tn = 1024
    return pl.pallas_call(
        _in_proj_kernel,
        grid=(m // tm, n // tn),
        in_specs=[pl.BlockSpec((tm, d), lambda i, j: (i, 0)),
                  pl.BlockSpec((1, d), lambda i, j: (0, 0)),
                  pl.BlockSpec((d, tn), lambda i, j: (0, j)),
                  pl.BlockSpec((d, LANES), lambda i, j: (0, 0))],
        out_specs=[pl.BlockSpec((tm, tn), lambda i, j: (i, j)),
                   pl.BlockSpec((tm, LANES), lambda i, j: (i, 0))],
        out_shape=[jax.ShapeDtypeStruct((m, n), F32), jax.ShapeDtypeStruct((m, LANES), F32)],
        scratch_shapes=[pltpu.VMEM((tm, d), BF16)],
        compiler_params=_params("parallel", "arbitrary"),
        name="in_proj",
    )(x, g, w_zx, w_dt)


def _ssd_kernel(z_ref, xs_ref, bc_ref, dt_ref, cw_ref, cb_ref, dtb_ref, alog_ref, dsk_ref, ng_ref,
                yg_ref, conv_ref, h_ref, xtail_ref, bctail_ref, ht_ref, *, cl, d_inner, n_state):
    c = pl.program_id(1)
    n_pairs = d_inner // PAIR
    pairs_per_group = n_pairs // SSM_GROUPS
    gb = SSM_GROUPS * n_state

    @pl.when(c == 0)
    def _():
        xtail_ref[...] = jnp.zeros((HALO, d_inner), F32)
        bctail_ref[...] = jnp.zeros((HALO, 2 * gb), F32)
        ht_ref[...] = jnp.zeros_like(ht_ref)

    row8 = lax.broadcasted_iota(jnp.int32, (HALO, 1), 0)

    def conv(in_ref, tail_ref, c0, width):
        x = in_ref[...]
        tail = tail_ref[...]
        acc = cb_ref[:, c0:c0 + width] + cw_ref[CONV_TAPS - 1:CONV_TAPS, c0:c0 + width] * x
        for back in range(1, CONV_TAPS):
            shifted = pltpu.roll(x, back, 0)
            head = jnp.where(row8 < back, pltpu.roll(tail, back, 0), shifted[0:HALO, :])
            shifted = jnp.concatenate([head, shifted[HALO:, :]], axis=0)
            k = CONV_TAPS - 1 - back
            acc = acc + cw_ref[k:k + 1, c0:c0 + width] * shifted
        tail_ref[...] = x[cl - HALO:cl, :]
        return jax.nn.silu(acc)

    xc = conv(xs_ref, xtail_ref, 0, d_inner)
    bcc = conv(bc_ref, bctail_ref, d_inner, 2 * gb)

    @pl.when(c == pl.num_programs(1) - 1)
    def _():
        conv_ref[0, :, 0:d_inner] = xs_ref[cl - (CONV_TAPS - 1):cl, :]
        conv_ref[0, :, d_inner:d_inner + 2 * gb] = bc_ref[cl - (CONV_TAPS - 1):cl, :]

    dt = _softplus(dt_ref[...] + dtb_ref[...])
    dta = dt * (-jnp.exp(alog_ref[...]))
    causal = (lax.broadcasted_iota(jnp.int32, (cl, cl), 0)
              >= lax.broadcasted_iota(jnp.int32, (cl, cl), 1))
    tri = jnp.where(causal, 1.0, 0.0).astype(BF16)
    a_cum = jnp.zeros((cl, LANES), F32)
    rest = dta
    for _ in range(3):
        term = rest.astype(BF16)
        a_cum = a_cum + jnp.dot(tri, term, preferred_element_type=F32)
        rest = rest - term.astype(F32)
    a_cum_t = a_cum.T
    dt_t = dt.T
    dte_t = jnp.exp(a_cum_t[:, cl - 1:cl] - a_cum_t) * dt_t
    cd_row = jnp.exp(a_cum[cl - 1:cl, :])

    lo = lax.broadcasted_iota(jnp.int32, (1, PAIR), 1) < HEAD_DIM

    for g in range(SSM_GROUPS):
        b_f = bcc[:, g * n_state:(g + 1) * n_state]
        bt_g = b_f.T
        c_g = bcc[:, gb + g * n_state:gb + (g + 1) * n_state].astype(BF16)
        cb_g = _nt_dot(c_g, b_f.astype(BF16))
        gated = []
        ssq = jnp.zeros((cl, 1), F32)
        for jj in range(pairs_per_group):
            j = g * pairs_per_group + jj
            sl = slice(j * PAIR, (j + 1) * PAIR)
            x_p = xc[:, sl]
            x_half = (jnp.where(lo, x_p, 0.0).astype(BF16), jnp.where(lo, 0.0, x_p).astype(BF16))
            y = None
            st = None
            ea = []
            for half in range(2):
                r = 2 * j + half
                col = jnp.broadcast_to(a_cum[:, r:r + 1], (cl, cl))
                decay = jnp.exp(jnp.where(causal, col - a_cum_t[r:r + 1, :], -jnp.inf))
                w = (cb_g * decay * dt_t[r:r + 1, :]).astype(BF16)
                part = jnp.dot(w, x_half[half], preferred_element_type=F32)
                y = part if y is None else y + part
                part = jnp.dot((bt_g * dte_t[r:r + 1, :]).astype(BF16), x_half[half],
                               preferred_element_type=F32)
                st = part if st is None else st + part
                ea.append(jnp.exp(col))
            ht_p = ht_ref[:, sl]
            y = y + jnp.dot(c_g, ht_p.astype(BF16), preferred_element_type=F32) * jnp.where(lo, ea[0], ea[1])
            ht_ref[:, sl] = ht_p * _lane_pair(cd_row, j, lo) + st
            y = y + dsk_ref[:, sl] * x_p
            gy = y * jax.nn.silu(z_ref[:, sl])
            ssq = ssq + jnp.sum(gy * gy, axis=-1, keepdims=True)
            gated.append(gy)
        scale = lax.rsqrt(ssq / (pairs_per_group * PAIR) + EPS)
        for jj in range(pairs_per_group):
            sl = slice((g * pairs_per_group + jj) * PAIR, (g * pairs_per_group + jj + 1) * PAIR)
            yg_ref[:, sl] = (gated[jj] * scale * ng_ref[:, sl]).astype(BF16)

    @pl.when(c == pl.num_programs(1) - 1)
    def _():
        for j in range(n_pairs):
            sl = slice(j * PAIR, (j + 1) * PAIR)
            h_ref[0, sl, :] = ht_ref[:, sl].T


def _ssd_prompt(zx, dt_raw, conv_w, conv_b, dt_bias, a_log, d_skip, norm_g, bsz, seq, d_inner, n_state):
    cl = SSD_CHUNK
    assert seq % cl == 0
    nc = seq // cl
    gb = SSM_GROUPS * n_state
    conv_dim = d_inner + 2 * gb
    xs_blk = d_inner // d_inner
    bc_blk = (2 * d_inner) // (2 * gb)
    assert (2 * d_inner) % (2 * gb) == 0
    kern = functools.partial(_ssd_kernel, cl=cl, d_inner=d_inner, n_state=n_state)
    row = lambda b, c: b * nc + c
    const = lambda b, c: (0, 0)
    return pl.pallas_call(
        kern,
        grid=(bsz, nc),
        in_specs=[pl.BlockSpec((cl, d_inner), lambda b, c: (row(b, c), 0)),
                  pl.BlockSpec((cl, d_inner), lambda b, c: (row(b, c), xs_blk)),
                  pl.BlockSpec((cl, 2 * gb), lambda b, c: (row(b, c), bc_blk)),
                  pl.BlockSpec((cl, LANES), lambda b, c: (row(b, c), 0)),
                  pl.BlockSpec((CONV_TAPS, conv_dim), const),
                  pl.BlockSpec((1, conv_dim), const),
                  pl.BlockSpec((1, LANES), const),
                  pl.BlockSpec((1, LANES), const),
                  pl.BlockSpec((1, d_inner), const),
                  pl.BlockSpec((1, d_inner), const)],
        out_specs=[pl.BlockSpec((cl, d_inner), lambda b, c: (row(b, c), 0)),
                   pl.BlockSpec((1, CONV_TAPS - 1, conv_dim), lambda b, c: (b, 0, 0)),
                   pl.BlockSpec((1, d_inner, n_state), lambda b, c: (b, 0, 0))],
        out_shape=[jax.ShapeDtypeStruct((bsz * seq, d_inner), BF16),
                   jax.ShapeDtypeStruct((bsz, CONV_TAPS - 1, conv_dim), F32),
                   jax.ShapeDtypeStruct((bsz, d_inner, n_state), F32)],
        scratch_shapes=[pltpu.VMEM((HALO, d_inner), F32), pltpu.VMEM((HALO, 2 * gb), F32),
                        pltpu.VMEM((n_state, d_inner), F32)],
        compiler_params=_params("parallel", "arbitrary"),
        name="ssd_prompt",
    )(zx, zx, zx, dt_raw, conv_w, conv_b, dt_bias, a_log, d_skip, norm_g)


def _ssd_step_kernel(zx_ref, dt_ref, cs_ref, h_ref, cw_ref, cb_ref, dtb_ref, alog_ref, dsk_ref, ng_ref,
                     yg_ref, cso_ref, ho_ref, *, d_inner, n_state):
    n_pairs = d_inner // PAIR
    pairs_per_group = n_pairs // SSM_GROUPS
    gb = SSM_GROUPS * n_state
    zx = zx_ref[0]
    z = zx[:, 0:d_inner]
    xbc = zx[:, d_inner:]
    prev = cs_ref[0]
    acc = cb_ref[...] + cw_ref[CONV_TAPS - 1:CONV_TAPS, :] * xbc
    for k in range(CONV_TAPS - 1):
        acc = acc + cw_ref[k:k + 1, :] * prev[k:k + 1, :]
    cso_ref[0, 0:CONV_TAPS - 2, :] = prev[1:CONV_TAPS - 1, :]
    cso_ref[0, CONV_TAPS - 2:CONV_TAPS - 1, :] = xbc
    act = jax.nn.silu(acc)
    xs = act[:, 0:d_inner]
    dt = _softplus(dt_ref[0] + dtb_ref[...])
    da = jnp.exp(dt * (-jnp.exp(alog_ref[...])))

    lo = lax.broadcasted_iota(jnp.int32, (1, PAIR), 1) < HEAD_DIM
    row_lo = lax.broadcasted_iota(jnp.int32, (PAIR, 1), 0) < HEAD_DIM
    eye = (lax.broadcasted_iota(jnp.int32, (PAIR, PAIR), 0)
           == lax.broadcasted_iota(jnp.int32, (PAIR, PAIR), 1))

    for g in range(SSM_GROUPS):
        b_g = act[:, d_inner + g * n_state:d_inner + (g + 1) * n_state]
        c_g = act[:, d_inner + gb + g * n_state:d_inner + gb + (g + 1) * n_state]
        gated = []
        ssq = jnp.zeros((1, 1), F32)
        for jj in range(pairs_per_group):
            j = g * pairs_per_group + jj
            sl = slice(j * PAIR, (j + 1) * PAIR)
            x_p = xs[:, sl]
            xdt = x_p * _lane_pair(dt, j, lo)
            x_col = jnp.sum(jnp.where(eye, jnp.broadcast_to(xdt, (PAIR, PAIR)), 0.0), axis=1, keepdims=True)
            d_col = jnp.where(row_lo, da[:, 2 * j:2 * j + 1], da[:, 2 * j + 1:2 * j + 2])
            h_new = d_col * h_ref[0, sl, :] + x_col * b_g
            ho_ref[0, sl, :] = h_new
            y_col = jnp.sum(h_new * c_g, axis=1, keepdims=True)
            y = jnp.sum(jnp.where(eye, jnp.broadcast_to(y_col, (PAIR, PAIR)), 0.0), axis=0, keepdims=True)
            y = y + dsk_ref[:, sl] * x_p
            gy = y * jax.nn.silu(z[:, sl])
            ssq = ssq + jnp.sum(gy * gy, axis=-1, keepdims=True)
            gated.append(gy)
        scale = lax.rsqrt(ssq / (pairs_per_group * PAIR) + EPS)
        for jj in range(pairs_per_group):
            sl = slice((g * pairs_per_group + jj) * PAIR, (g * pairs_per_group + jj + 1) * PAIR)
            yg_ref[0, :, sl] = gated[jj] * scale * ng_ref[:, sl]


def _ssd_step(zx, dt_raw, state_conv, state_ssm, conv_w, conv_b, dt_bias, a_log, d_skip, norm_g,
              d_inner, n_state):
    bsz = zx.shape[0]
    conv_dim = conv_w.shape[1]
    kern = functools.partial(_ssd_step_kernel, d_inner=d_inner, n_state=n_state)
    const = lambda b: (0, 0)
    per_b = lambda b: (b, 0, 0)
    return pl.pallas_call(
        kern,
        grid=(bsz,),
        in_specs=[pl.BlockSpec((1, 1, d_inner + conv_dim), per_b),
                  pl.BlockSpec((1, 1, LANES), per_b),
                  pl.BlockSpec((1, CONV_TAPS - 1, conv_dim), per_b),
                  pl.BlockSpec((1, d_inner, n_state), per_b),
                  pl.BlockSpec((CONV_TAPS, conv_dim), const),
                  pl.BlockSpec((1, conv_dim), const),
                  pl.BlockSpec((1, LANES), const),
                  pl.BlockSpec((1, LANES), const),
                  pl.BlockSpec((1, d_inner), const),
                  pl.BlockSpec((1, d_inner), const)],
        out_specs=[pl.BlockSpec((1, 1, d_inner), per_b),
                   pl.BlockSpec((1, CONV_TAPS - 1, conv_dim), per_b),
                   pl.BlockSpec((1, d_inner, n_state), per_b)],
        out_shape=[jax.ShapeDtypeStruct((bsz, 1, d_inner), F32),
                   jax.ShapeDtypeStruct((bsz, CONV_TAPS - 1, conv_dim), F32),
                   jax.ShapeDtypeStruct((bsz, d_inner, n_state), F32)],
        compiler_params=_params("parallel"),
        name="ssd_step",
    )(zx.reshape(bsz, 1, -1), dt_raw.reshape(bsz, 1, LANES), state_conv,
      state_ssm.reshape(bsz, d_inner, n_state), conv_w, conv_b, dt_bias, a_log, d_skip, norm_g)


def _linear_res_kernel(a_ref, w_ref, r_ref, o_ref):
    o_ref[...] = r_ref[...] + jnp.dot(a_ref[...].astype(BF16), w_ref[...], preferred_element_type=F32)


def _linear_res(a, w, res):
    m, k = a.shape
    n = w.shape[1]
    tm = _row_tile(m, 512)
    return pl.pallas_call(
        _linear_res_kernel,
        grid=(m // tm,),
        in_specs=[pl.BlockSpec((tm, k), lambda i: (i, 0)),
                  pl.BlockSpec((k, n), lambda i: (0, 0)),
                  pl.BlockSpec((tm, n), lambda i: (i, 0))],
        out_specs=pl.BlockSpec((tm, n), lambda i: (i, 0)),
        out_shape=jax.ShapeDtypeStruct((m, n), F32),
        compiler_params=_params("parallel"),
        name="linear_res",
    )(a, w, res)


def _ffn_kernel(x_ref, g_ref, wg_ref, wu_ref, wd_ref, o_ref, xn_ref):
    @pl.when(pl.program_id(1) == 0)
    def _():
        x = x_ref[...]
        xn_ref[...] = (_rms_scale(x) * g_ref[...]).astype(BF16)
        o_ref[...] = x

    xn = xn_ref[...]
    gate = jnp.dot(xn, wg_ref[...], preferred_element_type=F32)
    up = jnp.dot(xn, wu_ref[...], preferred_element_type=F32)
    act = (jax.nn.silu(gate) * up).astype(BF16)
    o_ref[...] += jnp.dot(act, wd_ref[...], preferred_element_type=F32)


def _ffn(x, g, w_gate_up, w_down):
    m, d = x.shape
    d_ff = w_down.shape[0]
    tm = _row_tile(m, 1024)
    tf = 256
    assert d_ff % tf == 0
    nf = d_ff // tf
    return pl.pallas_call(
        _ffn_kernel,
        grid=(m // tm, nf),
        in_specs=[pl.BlockSpec((tm, d), lambda i, j: (i, 0)),
                  pl.BlockSpec((1, d), lambda i, j: (0, 0)),
                  pl.BlockSpec((d, tf), lambda i, j: (0, j)),
                  pl.BlockSpec((d, tf), lambda i, j: (0, nf + j)),
                  pl.BlockSpec((tf, d), lambda i, j: (j, 0))],
        out_specs=pl.BlockSpec((tm, d), lambda i, j: (i, 0)),
        out_shape=jax.ShapeDtypeStruct((m, d), F32),
        scratch_shapes=[pltpu.VMEM((tm, d), BF16)],
        compiler_params=_params("parallel", "arbitrary"),
        name="ffn",
    )(x, g, w_gate_up, w_gate_up, w_down)


def _qkv_kernel(x_ref, gq_ref, gkv_ref, w_ref, qn_ref, kn_ref, knc_ref, q_ref, k_ref, v4_ref, vd_ref,
                xq_ref, xkv_ref, *, n_heads, k_transposed):
    j = pl.program_id(1)
    lo = lax.broadcasted_iota(jnp.int32, (1, PAIR), 1) < HEAD_DIM

    def qk_norm(y, gain_ref, out_ref, post):
        for h in range(n_heads):
            yh = y[:, h * PAIR:(h + 1) * PAIR]
            sq = yh * yh
            s_lo = jnp.sum(jnp.where(lo, sq, 0.0), axis=-1, keepdims=True)
            s_hi = jnp.sum(jnp.where(lo, 0.0, sq), axis=-1, keepdims=True)
            r = jnp.where(lo, lax.rsqrt(s_lo / HEAD_DIM + EPS), lax.rsqrt(s_hi / HEAD_DIM + EPS))
            out_ref[:, h * PAIR:(h + 1) * PAIR] = (yh * r * gain_ref[...] * post).astype(out_ref.dtype)

    @pl.when(j == 0)
    def _():
        xh = _rms_scale(x_ref[...])
        xq_ref[...] = (xh * gq_ref[...]).astype(BF16)
        xkv_ref[...] = (xh * gkv_ref[...]).astype(BF16)
        qk_norm(jnp.dot(xq_ref[...], w_ref[0], preferred_element_type=F32), qn_ref, q_ref,
                HEAD_DIM ** -0.5)

    @pl.when(j == 1)
    def _():
        if k_transposed:
            kt = _nt_dot(w_ref[0], xkv_ref[...])
            for g in range(2 * n_heads):
                rows = slice(g * HEAD_DIM, (g + 1) * HEAD_DIM)
                blk = kt[rows, :]
                r = lax.rsqrt(jnp.mean(blk * blk, axis=0, keepdims=True) + EPS)
                k_ref[0, rows, :] = blk * r * knc_ref[...]
        else:
            qk_norm(jnp.dot(xkv_ref[...], w_ref[0], preferred_element_type=F32), kn_ref, k_ref, 1.0)

    @pl.when(j == 2)
    def _():
        v = jnp.dot(xkv_ref[...], w_ref[0], preferred_element_type=F32)
        vd_ref[...] = v.astype(BF16)
        for h in range(n_heads):
            v4_ref[:, h, :] = v[:, h * PAIR:(h + 1) * PAIR]


def _qkv(x, g_q, g_kv, w_q, w_k, w_v, q_norm, k_norm, seq=None):
    m, d = x.shape
    n_heads = d // PAIR
    k_transposed = seq is not None
    tm = _row_tile(m if seq is None else seq, 512)
    w = jnp.stack([w_q, w_k.T if k_transposed else w_k, w_v]).astype(BF16)
    pair_gain = jnp.tile(k_norm.astype(F32), 2).reshape(1, PAIR)
    kern = functools.partial(_qkv_kernel, n_heads=n_heads, k_transposed=k_transposed)
    const = lambda i, j: (0, 0)
    rowblk = lambda i, j: (i, 0)
    if k_transposed:
        per_seq = seq // tm
        k_spec = pl.BlockSpec((1, d, tm), lambda i, j: (i // per_seq, 0, i % per_seq))
        k_shape = jax.ShapeDtypeStruct((m // seq, d, seq), F32)
    else:
        k_spec = pl.BlockSpec((tm, d), rowblk)
        k_shape = jax.ShapeDtypeStruct((m, d), F32)
    return pl.pallas_call(
        kern,
        grid=(m // tm, 3),
        in_specs=[pl.BlockSpec((tm, d), rowblk),
                  pl.BlockSpec((1, d), const),
                  pl.BlockSpec((1, d), const),
                  pl.BlockSpec((1, d, d), lambda i, j: (j, 0, 0)),
                  pl.BlockSpec((1, PAIR), const),
                  pl.BlockSpec((1, PAIR), const),
                  pl.BlockSpec((HEAD_DIM, 1), const)],
        out_specs=[pl.BlockSpec((tm, d), rowblk), k_spec,
                   pl.BlockSpec((tm, n_heads, PAIR), lambda i, j: (i, 0, 0)),
                   pl.BlockSpec((tm, d), rowblk)],
        out_shape=[jax.ShapeDtypeStruct((m, d), BF16), k_shape,
                   jax.ShapeDtypeStruct((m, n_heads, PAIR), F32),
                   jax.ShapeDtypeStruct((m, d), BF16)],
        scratch_shapes=[pltpu.VMEM((tm, d), BF16), pltpu.VMEM((tm, d), BF16)],
        compiler_params=_params("parallel", "arbitrary"),
        name="qkv_proj",
    )(x, g_q, g_kv, w, jnp.tile(q_norm.astype(F32), 2).reshape(1, PAIR), pair_gain,
      k_norm.astype(F32).reshape(HEAD_DIM, 1))


def _bias_tiles_kernel(rb_ref, o_ref, *, t):
    h = pl.program_id(0)
    i = lax.broadcasted_iota(jnp.int32, (t, t), 0)
    j = lax.broadcasted_iota(jnp.int32, (t, t), 1)
    for off in range(2):
        n = i - j + off * t
        bucket = _bucket(n)
        tile = jnp.zeros((t, t), F32)
        for b in range(N_BUCKETS):
            tile = jnp.where(bucket == b, rb_ref[b, h], tile)
        o_ref[0, off] = jnp.where(n >= 0, tile, -jnp.inf)


def _bias_tiles(rel_bias, t):
    n_heads = rel_bias.shape[1]
    return pl.pallas_call(
        functools.partial(_bias_tiles_kernel, t=t),
        grid=(n_heads,),
        in_specs=[pl.BlockSpec(memory_space=pltpu.SMEM)],
        out_specs=pl.BlockSpec((1, 2, t, t), lambda h: (h, 0, 0, 0)),
        out_shape=jax.ShapeDtypeStruct((n_heads, 2, t, t), F32),
        compiler_params=_params("parallel"),
        name="bias_tiles",
    )(rel_bias)


def _lambda_value(lam_ref, lambda_init):
    s1 = jnp.sum(lam_ref[0:1, :] * lam_ref[1:2, :], axis=-1, keepdims=True)
    s2 = jnp.sum(lam_ref[2:3, :] * lam_ref[3:4, :], axis=-1, keepdims=True)
    return jnp.exp(s1) - jnp.exp(s2) + lambda_init


def _attn_kernel(q_ref, k_ref, v_ref, bias_ref, rb_ref, lam_ref, sg_ref, o_ref,
                 kb_ref, q12_ref, m_ref, l_ref, acc_ref, *, t, rc, nq, lambda_init):
    h = pl.program_id(1)
    qi = pl.program_id(2)
    nlt = t // LANES

    @pl.when(qi == 0)
    def _():
        for ki in range(nq):
            kb_ref[ki] = k_ref[0, :, ki * t:(ki + 1) * t].astype(BF16)

    q = q_ref[0]
    lo = lax.broadcasted_iota(jnp.int32, (t, PAIR), 1) < HEAD_DIM
    zero = jnp.zeros_like(q)
    q12_ref[0:t, :] = jnp.where(lo, q, zero)
    q12_ref[t:2 * t, :] = jnp.where(lo, zero, q)

    m_ref[...] = jnp.full(m_ref.shape, -jnp.inf, F32)
    l_ref[...] = jnp.zeros(l_ref.shape, F32)
    acc_ref[...] = jnp.zeros(acc_ref.shape, F32)

    def step(ki, bias_of, cols_of):
        base = pl.multiple_of(ki * t, t)
        for c in range(2 * t // rc):
            rows = slice(c * rc, (c + 1) * rc)
            ncols = cols_of(c)
            s = jnp.dot(q12_ref[rows, :], kb_ref[ki, :, 0:ncols], preferred_element_type=F32) + bias_of(c, ncols)
            m_prev = m_ref[rows, :]
            part = s[:, 0:LANES]
            for k in range(1, ncols // LANES):
                part = jnp.maximum(part, s[:, k * LANES:(k + 1) * LANES])
            m_new = jnp.maximum(m_prev, jnp.max(part, axis=-1, keepdims=True))
            alpha = jnp.exp(m_prev - m_new)
            ps = [jnp.exp(s[:, k * LANES:(k + 1) * LANES] - m_new) for k in range(ncols // LANES)]
            psum = ps[0]
            for pk in ps[1:]:
                psum = psum + pk
            l_ref[rows, :] = alpha * l_ref[rows, :] + psum
            pv = jnp.dot(jnp.concatenate(ps, axis=1).astype(BF16), v_ref[0, pl.ds(base, ncols), :],
                         preferred_element_type=F32)
            acc_ref[rows, :] = alpha * acc_ref[rows, :] + pv
            m_ref[rows, :] = m_new

    far_bias = rb_ref[N_BUCKETS - 1, h]

    def far_body(ki, carry):
        step(ki, lambda c, n: far_bias, lambda c: t)
        return carry

    lax.fori_loop(0, jnp.maximum(qi - 1, 0), far_body, 0)

    tile_row = lambda c: (c * rc) % t

    @pl.when(qi >= 1)
    def _():
        step(qi - 1, lambda c, n: bias_ref[0, 1, tile_row(c):tile_row(c) + rc, 0:n], lambda c: t)

    step(qi, lambda c, n: bias_ref[0, 0, tile_row(c):tile_row(c) + rc, 0:n], lambda c: tile_row(c) + rc)

    l = jnp.sum(l_ref[...], axis=-1, keepdims=True)
    o1 = acc_ref[0:t, :] / l[0:t, :]
    o2 = acc_ref[t:2 * t, :] / l[t:2 * t, :]
    o = o1 - _lambda_value(lam_ref, lambda_init) * o2
    o_ref[0] = (_rms_scale(o) * sg_ref[...] * (1.0 - lambda_init)).astype(o_ref.dtype)


def _prompt_attention(q, k_t, v, rel_bias, lam_rows, subln, lambda_init, bsz, seq):
    d = q.shape[-1]
    n_heads = d // PAIR
    t = min(seq, 512)
    assert seq % t == 0 and t >= MAX_DISTANCE
    nq = seq // t
    tiles = _bias_tiles(rel_bias, t)
    rc = min(t, 256)
    assert t % rc == 0 and rc % LANES == 0
    kern = functools.partial(_attn_kernel, t=t, rc=rc, nq=nq, lambda_init=lambda_init)
    const = lambda b, h, i: (0, 0)
    return pl.pallas_call(
        kern,
        grid=(bsz, n_heads, nq),
        in_specs=[pl.BlockSpec((1, t, PAIR), lambda b, h, i: (b, i, h)),
                  pl.BlockSpec((1, PAIR, seq), lambda b, h, i: (b, h, 0)),
                  pl.BlockSpec((1, seq, PAIR), lambda b, h, i: (b, 0, h)),
                  pl.BlockSpec((1, 2, t, t), lambda b, h, i: (h, 0, 0, 0)),
                  pl.BlockSpec(memory_space=pltpu.SMEM),
                  pl.BlockSpec((4, LANES), const),
                  pl.BlockSpec((1, PAIR), const)],
        out_specs=pl.BlockSpec((1, t, PAIR), lambda b, h, i: (b, i, h)),
        out_shape=jax.ShapeDtypeStruct((bsz, seq, d), BF16),
        scratch_shapes=[pltpu.VMEM((nq, PAIR, t), BF16), pltpu.VMEM((2 * t, PAIR), BF16),
                        pltpu.VMEM((2 * t, LANES), F32), pltpu.VMEM((2 * t, LANES), F32),
                        pltpu.VMEM((2 * t, PAIR), F32)],
        compiler_params=_params("parallel", "parallel", "arbitrary"),
        name="prompt_attention",
    )(q.reshape(bsz, seq, d), k_t, v.reshape(bsz, seq, d), tiles, rel_bias, lam_rows, subln)


def _sample_attn_kernel(pt_ref, q_ref, kn_ref, vn_ref, *rest, pages_per_step, past, n_heads, lambda_init):
    k_refs = rest[0:pages_per_step]
    v_refs = rest[pages_per_step:2 * pages_per_step]
    rb_ref, lam_ref, sg_ref, o_ref, qb_ref, bias_ref, m_ref, l_ref, acc_ref = rest[2 * pages_per_step:]
    step_id = pl.program_id(1)
    last = pl.num_programs(1) - 1
    nrow = 2 * n_heads
    d = n_heads * PAIR
    keys = pages_per_step * PAGE_SIZE

    @pl.when(step_id == 0)
    def _():
        bias_ref[...] = jnp.broadcast_to(rb_ref[:, N_BUCKETS - 1:N_BUCKETS], (nrow, keys))

    @pl.when(step_id == last)
    def _():
        key = lax.broadcasted_iota(jnp.int32, (nrow, keys), 1)
        bucket = _bucket(past - (step_id * keys + key))
        bias = jnp.zeros((nrow, keys), F32)
        for b in range(N_BUCKETS):
            bias = jnp.where(bucket == b, rb_ref[:, b:b + 1], bias)
        bias_ref[...] = bias

    @pl.when(step_id == 0)
    def _():
        q = q_ref[0].astype(F32)
        row = lax.broadcasted_iota(jnp.int32, (nrow, d), 0)
        col = lax.broadcasted_iota(jnp.int32, (nrow, d), 1)
        qblk = jnp.where(jnp.right_shift(col, 6) == row, jnp.broadcast_to(q, (nrow, d)), 0.0)
        qb_ref[...] = qblk.astype(BF16)
        m_ref[...] = jnp.sum(qblk * kn_ref[0], axis=-1, keepdims=True) + rb_ref[:, 0:1]
        l_ref[...] = jnp.ones(l_ref.shape, F32)
        vn = vn_ref[0].astype(F32)
        for h in range(n_heads):
            acc_ref[2 * h:2 * h + 2, :] = jnp.broadcast_to(vn[:, h * PAIR:(h + 1) * PAIR], (2, PAIR))

    qb = qb_ref[...]
    s = jnp.concatenate([jnp.dot(qb, k_refs[g][0].astype(BF16), preferred_element_type=F32)
                         for g in range(pages_per_step)], axis=1) + bias_ref[...]
    m_prev = m_ref[...]
    m_new = jnp.maximum(m_prev, jnp.max(s, axis=-1, keepdims=True))
    alpha = jnp.exp(m_prev - m_new)
    p = jnp.exp(s - m_new)
    l_ref[...] = alpha * l_ref[...] + jnp.sum(p, axis=-1, keepdims=True)
    pb = p.astype(BF16)
    row_head = jnp.right_shift(lax.broadcasted_iota(jnp.int32, (nrow, PAIR), 0), 1)
    pv = jnp.zeros((nrow, PAIR), F32)
    for h in range(n_heads):
        v_h = jnp.concatenate([v_refs[g][0, pl.ds(h, PAGE_SIZE, stride=n_heads), :].astype(BF16)
                               for g in range(pages_per_step)], axis=0)
        pv = jnp.where(row_head == h, jnp.dot(pb, v_h, preferred_element_type=F32), pv)
    acc_ref[...] = alpha * acc_ref[...] + pv
    m_ref[...] = m_new

    @pl.when(step_id == last)
    def _():
        rowc = lax.broadcasted_iota(jnp.int32, (nrow, 1), 0)
        lam = _lambda_value(lam_ref, lambda_init)
        coef = jnp.where(jnp.bitwise_and(rowc, 1) == 0, 1.0, -lam) / l_ref[...]
        scaled = acc_ref[...] * coef
        o = scaled + pltpu.roll(scaled, nrow - 1, 0)
        on = _rms_scale(o) * sg_ref[...] * (1.0 - lambda_init)
        for h in range(n_heads):
            o_ref[0, :, h * PAIR:(h + 1) * PAIR] = on[2 * h:2 * h + 1, :]


def _sample_attention(q, k_new, v_new, cache_k, cache_v, page_table, rel_rows, lam_rows, subln, lambda_init):
    bsz, d = q.shape
    n_heads = d // PAIR
    n_pages = page_table.shape[1]
    past = n_pages * PAGE_SIZE
    pages_per_step = max(g for g in (8, 4, 2, 1) if n_pages % g == 0)
    n_pool = cache_k.shape[0]
    kern = functools.partial(_sample_attn_kernel, pages_per_step=pages_per_step, past=past,
                             n_heads=n_heads, lambda_init=lambda_init)
    per_b = lambda b, s, pt: (b, 0, 0)
    const = lambda b, s, pt: (0, 0)

    def k_spec(g):
        return pl.BlockSpec((1, d, PAGE_SIZE), lambda b, s, pt: (pt[b, s * pages_per_step + g], 0, 0))

    def v_spec(g):
        return pl.BlockSpec((1, PAGE_SIZE * n_heads, PAIR),
                            lambda b, s, pt: (pt[b, s * pages_per_step + g], 0, 0))

    grid_spec = pltpu.PrefetchScalarGridSpec(
        num_scalar_prefetch=1,
        grid=(bsz, n_pages // pages_per_step),
        in_specs=([pl.BlockSpec((1, 1, d), per_b)] * 3
                  + [k_spec(g) for g in range(pages_per_step)]
                  + [v_spec(g) for g in range(pages_per_step)]
                  + [pl.BlockSpec((2 * n_heads, N_BUCKETS), const),
                     pl.BlockSpec((4, LANES), const),
                     pl.BlockSpec((1, PAIR), const)]),
        out_specs=pl.BlockSpec((1, 1, d), per_b),
        scratch_shapes=[pltpu.VMEM((2 * n_heads, d), BF16),
                        pltpu.VMEM((2 * n_heads, pages_per_step * PAGE_SIZE), F32),
                        pltpu.VMEM((2 * n_heads, 1), F32), pltpu.VMEM((2 * n_heads, 1), F32),
                        pltpu.VMEM((2 * n_heads, PAIR), F32)],
    )
    ck = jnp.transpose(cache_k, (0, 2, 3, 4, 1)).reshape(n_pool, d, PAGE_SIZE)
    cv = cache_v.reshape(n_pool, PAGE_SIZE * n_heads, PAIR)
    return pl.pallas_call(
        kern,
        grid_spec=grid_spec,
        out_shape=jax.ShapeDtypeStruct((bsz, 1, d), F32),
        compiler_params=_params("parallel", "arbitrary"),
        name="sample_attention",
    )(page_table, q.reshape(bsz, 1, d), k_new.reshape(bsz, 1, d), v_new.reshape(bsz, 1, d),
      *([ck] * pages_per_step), *([cv] * pages_per_step), rel_rows, lam_rows, subln).reshape(bsz, d)


def _pad_lanes(v):
    return jnp.pad(v.astype(F32), (0, LANES - v.shape[0])).reshape(1, LANES)


def kernel(x_prompt, x_sample, state_conv, state_ssm, cache_k, cache_v, page_table, norm_mix, norm_ffn, w_in, conv_w, conv_b, dt_bias, a_log, d_skip, ssm_norm, w_out_ssm, norm_kv, w_kv, k_norm, w_q, q_norm, lambda_q1, lambda_k1, lambda_q2, lambda_k2, subln, w_o, rel_bias, w_gate_up, w_down):
    bsz, seq, d = x_prompt.shape
    dec = x_sample.shape[0]
    assert x_sample.shape[1] == 1
    n_ssm_layers, ssm_heads = dt_bias.shape
    depth = norm_mix.shape[0]
    assert n_ssm_layers == 1 and depth == 2
    d_inner = w_out_ssm.shape[1]
    n_state = state_ssm.shape[-1]
    assert d_inner == ssm_heads * HEAD_DIM and n_state == LANES and ssm_heads <= LANES
    n_heads = d // PAIR

    xp = x_prompt.reshape(bsz * seq, d)
    xs = x_sample.reshape(dec, d)
    row = lambda v: v.astype(F32).reshape(1, -1)

    zx_cols = 2 * d_inner + 2 * SSM_GROUPS * n_state
    w_zx = w_in[0][:, :zx_cols].astype(BF16)
    w_dt = jnp.pad(w_in[0][:, zx_cols:], ((0, 0), (0, LANES - ssm_heads))).astype(BF16)
    g_mix0 = row(norm_mix[0])
    ssd_args = (conv_w[0], row(conv_b[0]), _pad_lanes(dt_bias[0]), _pad_lanes(a_log[0]),
                row(jnp.repeat(d_skip[0], HEAD_DIM)), row(ssm_norm[0]))
    w_out = w_out_ssm[0].astype(BF16)
    w_gu0, w_dn0 = w_gate_up[0].astype(BF16), w_down[0].astype(BF16)

    zx_p, dt_p = _in_proj(xp, g_mix0, w_zx, w_dt)
    yg_p, conv_p, ssm_p = _ssd_prompt(zx_p, dt_p, *ssd_args, bsz, seq, d_inner, n_state)
    xp = _linear_res(yg_p, w_out, xp)
    xp = _ffn(xp, row(norm_ffn[0]), w_gu0, w_dn0)

    zx_s, dt_s = _in_proj(xs, g_mix0, w_zx, w_dt)
    yg_s, conv_s, ssm_s = _ssd_step(zx_s, dt_s, state_conv[0], state_ssm[0], *ssd_args, d_inner, n_state)
    xs = _linear_res(yg_s.reshape(dec, d_inner), w_out, xs)
    xs = _ffn(xs, row(norm_ffn[0]), w_gu0, w_dn0)

    lambda_init = 0.8 - 0.6 * math.exp(-0.3 * 1)
    qkv_args = (row(norm_mix[1]), row(norm_kv), w_q[0], w_kv[:, :d], w_kv[:, d:], q_norm[0], k_norm)
    lam_rows = jnp.pad(jnp.stack([lambda_q1[0], lambda_k1[0], lambda_q2[0], lambda_k2[0]]).astype(F32),
                       ((0, 0), (0, LANES - HEAD_DIM)))
    sg = row(subln[0])
    w_oo = w_o[0].astype(BF16)
    w_gu1, w_dn1 = w_gate_up[1].astype(BF16), w_down[1].astype(BF16)

    q_p, kt_p, v_p, vd_p = _qkv(xp, *qkv_args, seq=seq)
    o_p = _prompt_attention(q_p, kt_p, vd_p, rel_bias.astype(F32), lam_rows, sg, lambda_init, bsz, seq)
    xp = _linear_res(o_p.reshape(bsz * seq, d), w_oo, xp)
    xp = _ffn(xp, row(norm_ffn[1]), w_gu1, w_dn1)

    q_s, k_s, v_s, vd_s = _qkv(xs, *qkv_args)
    rel_rows = jnp.repeat(rel_bias.astype(F32).T, 2, axis=0)
    o_s = _sample_attention(q_s.astype(F32), k_s, vd_s.astype(F32), cache_k, cache_v, page_table,
                            rel_rows, lam_rows, sg, lambda_init)
    xs = _linear_res(o_s, w_oo, xs)
    xs = _ffn(xs, row(norm_ffn[1]), w_gu1, w_dn1)

    k_p = jnp.transpose(kt_p.reshape(bsz, n_heads, 2, HEAD_DIM, seq), (0, 4, 1, 2, 3))
    return (xp.reshape(bsz, seq, d), xs.reshape(dec, 1, d),
            conv_p[None], ssm_p.reshape(1, bsz, ssm_heads, HEAD_DIM, n_state),
            k_p, v_p.reshape(bsz, seq, n_heads, PAIR),
            conv_s[None], ssm_s.reshape(1, dec, ssm_heads, HEAD_DIM, n_state),
            k_s.reshape(dec, 1, n_heads, 2, HEAD_DIM), v_s.reshape(dec, 1, n_heads, PAIR))
```

```python
import functools
import math

import jax
import jax.numpy as jnp
from jax import lax
from jax.experimental import pallas as pl
from jax.experimental.pallas import tpu as pltpu

F32 = jnp.float32
BF16 = jnp.bfloat16

EPS = 1e-6
LANES = 128
HEAD_DIM = 64
PAIR = 2 * HEAD_DIM
SSD_CHUNK = 128
SSM_GROUPS = 4
CONV_TAPS = 4
HALO = 8
N_BUCKETS = 32
MAX_EXACT = N_BUCKETS // 2
MAX_DISTANCE = 128
PAGE_SIZE = 128
LOG2E = math.log2(math.e)
MXU_WIDTH = 256
VMEM_LIMIT = 52 * 1024 * 1024


def _params(*semantics):
    return pltpu.CompilerParams(dimension_semantics=semantics, vmem_limit_bytes=VMEM_LIMIT)


def _row_tile(m, preferred):
    if m <= preferred:
        return m
    t = preferred
    while m % t or t % 16:
        t -= 1
    return t


def _nt_dot(a, b):
    return lax.dot_general(a, b, (((1,), (1,)), ((), ())), preferred_element_type=F32)


def _tn_dot(a, b):
    return lax.dot_general(a, b, (((0,), (0,)), ((), ())), preferred_element_type=F32)


def _rms_scale(x):
    return x * lax.rsqrt(jnp.mean(x * x, axis=-1, keepdims=True) + EPS)


def _softplus(x):
    return jnp.maximum(x, 0.0) + jnp.log1p(jnp.exp(-jnp.abs(x)))


def _lane_pair(arr, j, lo):
    return jnp.where(lo, arr[:, 2 * j:2 * j + 1], arr[:, 2 * j + 1:2 * j + 2])


def _bucket(n):
    n = jnp.maximum(n, 0)
    nf = jnp.maximum(n, 1).astype(F32)
    large = MAX_EXACT + (jnp.log(nf / MAX_EXACT) / math.log(MAX_DISTANCE / MAX_EXACT)
                         * (N_BUCKETS - MAX_EXACT)).astype(jnp.int32)
    large = jnp.minimum(large, N_BUCKETS - 1)
    return jnp.where(n < MAX_EXACT, n, large)


def _in_proj_kernel(x_ref, g_ref, w_ref, wdt_ref, zx_ref, dt_ref, xn_ref, *, tn):
    @pl.when(pl.program_id(1) == 0)
    def _():
        xn_ref[...] = (_rms_scale(x_ref[...]) * g_ref[...]).astype(BF16)
        dt_ref[...] = jnp.dot(xn_ref[...], wdt_ref[...], preferred_element_type=F32)

    xn = xn_ref[...]
    for c0 in range(0, tn, 2 * MXU_WIDTH):
        cols = slice(c0, min(c0 + 2 * MXU_WIDTH, tn))
        zx_ref[:, cols] = jnp.dot(xn, w_ref[:, cols], preferred_element_type=F32)


def _in_proj(x, g, w_zx, w_dt):
    m, d = x.shape
    n = w_zx.shape[1]
    tm = _row_tile(m, 1024)
    tn = n // 2 if n % (2 * LANES) == 0 else n
    return pl.pallas_call(
        functools.partial(_in_proj_kernel, tn=tn),
        grid=(m // tm, n // tn),
        in_specs=[pl.BlockSpec((tm, d), lambda i, j: (i, 0)),
                  pl.BlockSpec((1, d), lambda i, j: (0, 0)),
                  pl.BlockSpec((d, tn), lambda i, j: (0, j)),
                  pl.BlockSpec((d, LANES), lambda i, j: (0, 0))],
        out_specs=[pl.BlockSpec((tm, tn), lambda i, j: (i, j)),
                   pl.BlockSpec((tm, LANES), lambda i, j: (i, 0))],
        out_shape=[jax.ShapeDtypeStruct((m, n), F32), jax.ShapeDtypeStruct((m, LANES), F32)],
        scratch_shapes=[pltpu.VMEM((tm, d), BF16)],
        compiler_params=_params("parallel", "arbitrary"),
        name="in_proj",
    )(x, g, w_zx, w_dt)


def _ssd_kernel(z_ref, xs_ref, bc_ref, dt_ref, cw_ref, cb_ref, dtb_ref, alog_ref, dsk_ref, ng_ref,
                yg_ref, conv_ref, h_ref, xtail_ref, bctail_ref, ht_ref, *, cl, d_inner, n_state):
    c = pl.program_id(1)
    n_pairs = d_inner // PAIR
    pairs_per_group = n_pairs // SSM_GROUPS
    gb = SSM_GROUPS * n_state

    @pl.when(c == 0)
    def _():
        xtail_ref[...] = jnp.zeros((HALO, d_inner), F32)
        bctail_ref[...] = jnp.zeros((HALO, 2 * gb), F32)
        ht_ref[...] = jnp.zeros_like(ht_ref)

    row8 = lax.broadcasted_iota(jnp.int32, (HALO, 1), 0)

    def conv(in_ref, tail_ref, c0, width):
        x = in_ref[...]
        tail = tail_ref[...]
        acc = cb_ref[:, c0:c0 + width] + cw_ref[CONV_TAPS - 1:CONV_TAPS, c0:c0 + width] * x
        for back in range(1, CONV_TAPS):
            shifted = pltpu.roll(x, back, 0)
            head = jnp.where(row8 < back, pltpu.roll(tail, back, 0), shifted[0:HALO, :])
            shifted = jnp.concatenate([head, shifted[HALO:, :]], axis=0)
            k = CONV_TAPS - 1 - back
            acc = acc + cw_ref[k:k + 1, c0:c0 + width] * shifted
        tail_ref[...] = x[cl - HALO:cl, :]
        return jax.nn.silu(acc)

    xc = conv(xs_ref, xtail_ref, 0, d_inner)
    bcc = conv(bc_ref, bctail_ref, d_inner, 2 * gb)

    @pl.when(c == pl.num_programs(1) - 1)
    def _():
        conv_ref[0, :, 0:d_inner] = xs_ref[cl - (CONV_TAPS - 1):cl, :]
        conv_ref[0, :, d_inner:d_inner + 2 * gb] = bc_ref[cl - (CONV_TAPS - 1):cl, :]

    dt = _softplus(dt_ref[...] + dtb_ref[...])
    dta = dt * (-jnp.exp(alog_ref[...]))
    causal = (lax.broadcasted_iota(jnp.int32, (cl, cl), 0)
              >= lax.broadcasted_iota(jnp.int32, (cl, cl), 1))
    tri = jnp.where(causal, 1.0, 0.0).astype(BF16)
    a_cum = jnp.zeros((cl, LANES), F32)
    rest = dta
    for _ in range(3):
        term = rest.astype(BF16)
        a_cum = a_cum + jnp.dot(tri, term, preferred_element_type=F32)
        rest = rest - term.astype(F32)
    a_cum_t = a_cum.T
    dt_t = dt.T
    dte_t = jnp.exp(a_cum_t[:, cl - 1:cl] - a_cum_t) * dt_t
    cd_row = jnp.exp(a_cum[cl - 1:cl, :])

    lo = lax.broadcasted_iota(jnp.int32, (1, PAIR), 1) < HEAD_DIM

    for g in range(SSM_GROUPS):
        b_f = bcc[:, g * n_state:(g + 1) * n_state]
        bt_g = b_f.T
        c_g = bcc[:, gb + g * n_state:gb + (g + 1) * n_state].astype(BF16)
        cb_g = _nt_dot(c_g, b_f.astype(BF16))
        gated = []
        ssq = jnp.zeros((cl, 1), F32)
        for jj in range(pairs_per_group):
            j = g * pairs_per_group + jj
            sl = slice(j * PAIR, (j + 1) * PAIR)
            x_p = xc[:, sl]
            x_half = (jnp.where(lo, x_p, 0.0).astype(BF16), jnp.where(lo, 0.0, x_p).astype(BF16))
            y = None
            st = None
            ea = []
            for half in range(2):
                r = 2 * j + half
                col = jnp.broadcast_to(a_cum[:, r:r + 1], (cl, cl))
                decay = jnp.exp(jnp.where(causal, col - a_cum_t[r:r + 1, :], -jnp.inf))
                w = (cb_g * decay * dt_t[r:r + 1, :]).astype(BF16)
                part = jnp.dot(w, x_half[half], preferred_element_type=F32)
                y = part if y is None else y + part
                part = jnp.dot((bt_g * dte_t[r:r + 1, :]).astype(BF16), x_half[half],
                               preferred_element_type=F32)
                st = part if st is None else st + part
                ea.append(jnp.exp(col))
            ht_p = ht_ref[:, sl]
            y = y + jnp.dot(c_g, ht_p.astype(BF16), preferred_element_type=F32) * jnp.where(lo, ea[0], ea[1])
            ht_ref[:, sl] = ht_p * _lane_pair(cd_row, j, lo) + st
            y = y + dsk_ref[:, sl] * x_p
            gy = y * jax.nn.silu(z_ref[:, sl])
            ssq = ssq + jnp.sum(gy * gy, axis=-1, keepdims=True)
            gated.append(gy)
        scale = lax.rsqrt(ssq / (pairs_per_group * PAIR) + EPS)
        for jj in range(pairs_per_group):
            sl = slice((g * pairs_per_group + jj) * PAIR, (g * pairs_per_group + jj + 1) * PAIR)
            yg_ref[:, sl] = (gated[jj] * scale * ng_ref[:, sl]).astype(BF16)

    @pl.when(c == pl.num_programs(1) - 1)
    def _():
        for j in range(n_pairs):
            sl = slice(j * PAIR, (j + 1) * PAIR)
            h_ref[0, sl, :] = ht_ref[:, sl].T


def _ssd_prompt(zx, dt_raw, conv_w, conv_b, dt_bias, a_log, d_skip, norm_g, bsz, seq, d_inner, n_state):
    cl = SSD_CHUNK
    assert seq % cl == 0
    nc = seq // cl
    gb = SSM_GROUPS * n_state
    conv_dim = d_inner + 2 * gb
    xs_blk = d_inner // d_inner
    bc_blk = (2 * d_inner) // (2 * gb)
    assert (2 * d_inner) % (2 * gb) == 0
    kern = functools.partial(_ssd_kernel, cl=cl, d_inner=d_inner, n_state=n_state)
    row = lambda b, c: b * nc + c
    const = lambda b, c: (0, 0)
    return pl.pallas_call(
        kern,
        grid=(bsz, nc),
        in_specs=[pl.BlockSpec((cl, d_inner), lambda b, c: (row(b, c), 0)),
                  pl.BlockSpec((cl, d_inner), lambda b, c: (row(b, c), xs_blk)),
                  pl.BlockSpec((cl, 2 * gb), lambda b, c: (row(b, c), bc_blk)),
                  pl.BlockSpec((cl, LANES), lambda b, c: (row(b, c), 0)),
                  pl.BlockSpec((CONV_TAPS, conv_dim), const),
                  pl.BlockSpec((1, conv_dim), const),
                  pl.BlockSpec((1, LANES), const),
                  pl.BlockSpec((1, LANES), const),
                  pl.BlockSpec((1, d_inner), const),
                  pl.BlockSpec((1, d_inner), const)],
        out_specs=[pl.BlockSpec((cl, d_inner), lambda b, c: (row(b, c), 0)),
                   pl.BlockSpec((1, CONV_TAPS - 1, conv_dim), lambda b, c: (b, 0, 0)),
                   pl.BlockSpec((1, d_inner, n_state), lambda b, c: (b, 0, 0))],
        out_shape=[jax.ShapeDtypeStruct((bsz * seq, d_inner), BF16),
                   jax.ShapeDtypeStruct((bsz, CONV_TAPS - 1, conv_dim), F32),
                   jax.ShapeDtypeStruct((bsz, d_inner, n_state), F32)],
        scratch_shapes=[pltpu.VMEM((HALO, d_inner), F32), pltpu.VMEM((HALO, 2 * gb), F32),
                        pltpu.VMEM((n_state, d_inner), F32)],
        compiler_params=_params("parallel", "arbitrary"),
        name="ssd_prompt",
    )(zx, zx, zx, dt_raw, conv_w, conv_b, dt_bias, a_log, d_skip, norm_g)


def _ssd_step_kernel(zx_ref, dt_ref, cs_ref, h_ref, cw_ref, cb_ref, dtb_ref, alog_ref, dsk_ref, ng_ref,
                     yg_ref, cso_ref, ho_ref, *, d_inner, n_state):
    n_pairs = d_inner // PAIR
    pairs_per_group = n_pairs // SSM_GROUPS
    gb = SSM_GROUPS * n_state
    zx = zx_ref[0]
    z = zx[:, 0:d_inner]
    xbc = zx[:, d_inner:]
    prev = cs_ref[0]
    acc = cb_ref[...] + cw_ref[CONV_TAPS - 1:CONV_TAPS, :] * xbc
    for k in range(CONV_TAPS - 1):
        acc = acc + cw_ref[k:k + 1, :] * prev[k:k + 1, :]
    cso_ref[0, 0:CONV_TAPS - 2, :] = prev[1:CONV_TAPS - 1, :]
    cso_ref[0, CONV_TAPS - 2:CONV_TAPS - 1, :] = xbc
    act = jax.nn.silu(acc)
    xs = act[:, 0:d_inner]
    dt = _softplus(dt_ref[0] + dtb_ref[...])
    da = jnp.exp(dt * (-jnp.exp(alog_ref[...])))

    lo = lax.broadcasted_iota(jnp.int32, (1, PAIR), 1) < HEAD_DIM
    row_lo = lax.broadcasted_iota(jnp.int32, (PAIR, 1), 0) < HEAD_DIM
    eye = (lax.broadcasted_iota(jnp.int32, (PAIR, PAIR), 0)
           == lax.broadcasted_iota(jnp.int32, (PAIR, PAIR), 1))

    for g in range(SSM_GROUPS):
        b_g = act[:, d_inner + g * n_state:d_inner + (g + 1) * n_state]
        c_g = act[:, d_inner + gb + g * n_state:d_inner + gb + (g + 1) * n_state]
        gated = []
        ssq = jnp.zeros((1, 1), F32)
        for jj in range(pairs_per_group):
            j = g * pairs_per_group + jj
            sl = slice(j * PAIR, (j + 1) * PAIR)
            x_p = xs[:, sl]
            xdt = x_p * _lane_pair(dt, j, lo)
            x_col = jnp.sum(jnp.where(eye, jnp.broadcast_to(xdt, (PAIR, PAIR)), 0.0), axis=1, keepdims=True)
            d_col = jnp.where(row_lo, da[:, 2 * j:2 * j + 1], da[:, 2 * j + 1:2 * j + 2])
            h_new = d_col * h_ref[0, sl, :] + x_col * b_g
            ho_ref[0, sl, :] = h_new
            y_col = jnp.sum(h_new * c_g, axis=1, keepdims=True)
            y = jnp.sum(jnp.where(eye, jnp.broadcast_to(y_col, (PAIR, PAIR)), 0.0), axis=0, keepdims=True)
            y = y + dsk_ref[:, sl] * x_p
            gy = y * jax.nn.silu(z[:, sl])
            ssq = ssq + jnp.sum(gy * gy, axis=-1, keepdims=True)
            gated.append(gy)
        scale = lax.rsqrt(ssq / (pairs_per_group * PAIR) + EPS)
        for jj in range(pairs_per_group):
            sl = slice((g * pairs_per_group + jj) * PAIR, (g * pairs_per_group + jj + 1) * PAIR)
            yg_ref[0, :, sl] = gated[jj] * scale * ng_ref[:, sl]


def _ssd_step(zx, dt_raw, state_conv, state_ssm, conv_w, conv_b, dt_bias, a_log, d_skip, norm_g,
              d_inner, n_state):
    bsz = zx.shape[0]
    conv_dim = conv_w.shape[1]
    kern = functools.partial(_ssd_step_kernel, d_inner=d_inner, n_state=n_state)
    const = lambda b: (0, 0)
    per_b = lambda b: (b, 0, 0)
    return pl.pallas_call(
        kern,
        grid=(bsz,),
        in_specs=[pl.BlockSpec((1, 1, d_inner + conv_dim), per_b),
                  pl.BlockSpec((1, 1, LANES), per_b),
                  pl.BlockSpec((1, CONV_TAPS - 1, conv_dim), per_b),
                  pl.BlockSpec((1, d_inner, n_state), per_b),
                  pl.BlockSpec((CONV_TAPS, conv_dim), const),
                  pl.BlockSpec((1, conv_dim), const),
                  pl.BlockSpec((1, LANES), const),
                  pl.BlockSpec((1, LANES), const),
                  pl.BlockSpec((1, d_inner), const),
                  pl.BlockSpec((1, d_inner), const)],
        out_specs=[pl.BlockSpec((1, 1, d_inner), per_b),
                   pl.BlockSpec((1, CONV_TAPS - 1, conv_dim), per_b),
                   pl.BlockSpec((1, d_inner, n_state), per_b)],
        out_shape=[jax.ShapeDtypeStruct((bsz, 1, d_inner), F32),
                   jax.ShapeDtypeStruct((bsz, CONV_TAPS - 1, conv_dim), F32),
                   jax.ShapeDtypeStruct((bsz, d_inner, n_state), F32)],
        compiler_params=_params("parallel"),
        name="ssd_step",
    )(zx.reshape(bsz, 1, -1), dt_raw.reshape(bsz, 1, LANES), state_conv,
      state_ssm.reshape(bsz, d_inner, n_state), conv_w, conv_b, dt_bias, a_log, d_skip, norm_g)


def _linear_res_kernel(a_ref, w_ref, r_ref, o_ref):
    o_ref[...] = r_ref[...] + jnp.dot(a_ref[...].astype(BF16), w_ref[...], preferred_element_type=F32)


def _linear_res(a, w, res):
    m, k = a.shape
    n = w.shape[1]
    tm = _row_tile(m, 512)
    return pl.pallas_call(
        _linear_res_kernel,
        grid=(m // tm,),
        in_specs=[pl.BlockSpec((tm, k), lambda i: (i, 0)),
                  pl.BlockSpec((k, n), lambda i: (0, 0)),
                  pl.BlockSpec((tm, n), lambda i: (i, 0))],
        out_specs=pl.BlockSpec((tm, n), lambda i: (i, 0)),
        out_shape=jax.ShapeDtypeStruct((m, n), F32),
        compiler_params=_params("parallel"),
        name="linear_res",
    )(a, w, res)


def _ffn_kernel(x_ref, g_ref, wg_ref, wu_ref, wd_ref, o_ref, xn_ref, *, tf):
    @pl.when(pl.program_id(1) == 0)
    def _():
        x = x_ref[...]
        xn_ref[...] = (_rms_scale(x) * g_ref[...]).astype(BF16)
        o_ref[...] = x

    xn = xn_ref[...]
    acts = []
    for c0 in range(0, tf, MXU_WIDTH):
        cols = slice(c0, min(c0 + MXU_WIDTH, tf))
        gate = jnp.dot(xn, wg_ref[:, cols], preferred_element_type=F32)
        up = jnp.dot(xn, wu_ref[:, cols], preferred_element_type=F32)
        acts.append((jax.nn.silu(gate) * up).astype(BF16))
    o_ref[...] += jnp.dot(jnp.concatenate(acts, axis=1), wd_ref[...], preferred_element_type=F32)


def _ffn(x, g, w_gate_up, w_down):
    m, d = x.shape
    d_ff = w_down.shape[0]
    tm = _row_tile(m, 1024)
    tf = d_ff // 2 if d_ff % (2 * LANES) == 0 else d_ff
    nf = d_ff // tf
    return pl.pallas_call(
        functools.partial(_ffn_kernel, tf=tf),
        grid=(m // tm, nf),
        in_specs=[pl.BlockSpec((tm, d), lambda i, j: (i, 0)),
                  pl.BlockSpec((1, d), lambda i, j: (0, 0)),
                  pl.BlockSpec((d, tf), lambda i, j: (0, j)),
                  pl.BlockSpec((d, tf), lambda i, j: (0, nf + j)),
                  pl.BlockSpec((tf, d), lambda i, j: (j, 0))],
        out_specs=pl.BlockSpec((tm, d), lambda i, j: (i, 0)),
        out_shape=jax.ShapeDtypeStruct((m, d), F32),
        scratch_shapes=[pltpu.VMEM((tm, d), BF16)],
        compiler_params=_params("parallel", "arbitrary"),
        name="ffn",
    )(x, g, w_gate_up, w_gate_up, w_down)


def _qkv_kernel(x_ref, gq_ref, gkv_ref, w_ref, qn_ref, kn_ref, knc_ref, q_ref, k_ref, v4_ref, vd_ref,
                xq_ref, xkv_ref, *, n_heads, k_transposed):
    j = pl.program_id(1)
    lo = lax.broadcasted_iota(jnp.int32, (1, PAIR), 1) < HEAD_DIM

    def qk_norm(y, gain_ref, out_ref, post):
        for h in range(n_heads):
            yh = y[:, h * PAIR:(h + 1) * PAIR]
            sq = yh * yh
            s_lo = jnp.sum(jnp.where(lo, sq, 0.0), axis=-1, keepdims=True)
            s_hi = jnp.sum(jnp.where(lo, 0.0, sq), axis=-1, keepdims=True)
            r = jnp.where(lo, lax.rsqrt(s_lo / HEAD_DIM + EPS), lax.rsqrt(s_hi / HEAD_DIM + EPS))
            out_ref[:, h * PAIR:(h + 1) * PAIR] = (yh * r * gain_ref[...] * post).astype(out_ref.dtype)

    @pl.when(j == 0)
    def _():
        xh = _rms_scale(x_ref[...])
        xq_ref[...] = (xh * gq_ref[...]).astype(BF16)
        xkv_ref[...] = (xh * gkv_ref[...]).astype(BF16)
        qk_norm(jnp.dot(xq_ref[...], w_ref[0], preferred_element_type=F32), qn_ref, q_ref,
                HEAD_DIM ** -0.5 * LOG2E)

    @pl.when(j == 1)
    def _():
        if k_transposed:
            kt = _nt_dot(w_ref[0], xkv_ref[...])
            for g in range(2 * n_heads):
                rows = slice(g * HEAD_DIM, (g + 1) * HEAD_DIM)
                blk = kt[rows, :]
                r = lax.rsqrt(jnp.mean(blk * blk, axis=0, keepdims=True) + EPS)
                k_ref[0, rows, :] = blk * r * knc_ref[...]
        else:
            qk_norm(jnp.dot(xkv_ref[...], w_ref[0], preferred_element_type=F32), kn_ref, k_ref, 1.0)

    @pl.when(j == 2)
    def _():
        v = jnp.dot(xkv_ref[...], w_ref[0], preferred_element_type=F32)
        vd_ref[...] = v.astype(BF16)
        for h in range(n_heads):
            v4_ref[:, h, :] = v[:, h * PAIR:(h + 1) * PAIR]


def _qkv(x, g_q, g_kv, w_q, w_k, w_v, q_norm, k_norm, seq=None):
    m, d = x.shape
    n_heads = d // PAIR
    k_transposed = seq is not None
    tm = _row_tile(m if seq is None else seq, 512)
    w = jnp.stack([w_q, w_k.T if k_transposed else w_k, w_v]).astype(BF16)
    pair_gain = jnp.tile(k_norm.astype(F32), 2).reshape(1, PAIR)
    kern = functools.partial(_qkv_kernel, n_heads=n_heads, k_transposed=k_transposed)
    const = lambda i, j: (0, 0)
    rowblk = lambda i, j: (i, 0)
    if k_transposed:
        per_seq = seq // tm
        k_spec = pl.BlockSpec((1, d, tm), lambda i, j: (i // per_seq, 0, i % per_seq))
        k_shape = jax.ShapeDtypeStruct((m // seq, d, seq), F32)
    else:
        k_spec = pl.BlockSpec((tm, d), rowblk)
        k_shape = jax.ShapeDtypeStruct((m, d), F32)
    return pl.pallas_call(
        kern,
        grid=(m // tm, 3),
        in_specs=[pl.BlockSpec((tm, d), rowblk),
                  pl.BlockSpec((1, d), const),
                  pl.BlockSpec((1, d), const),
                  pl.BlockSpec((1, d, d), lambda i, j: (j, 0, 0)),
                  pl.BlockSpec((1, PAIR), const),
                  pl.BlockSpec((1, PAIR), const),
                  pl.BlockSpec((HEAD_DIM, 1), const)],
        out_specs=[pl.BlockSpec((tm, d), rowblk), k_spec,
                   pl.BlockSpec((tm, n_heads, PAIR), lambda i, j: (i, 0, 0)),
                   pl.BlockSpec((tm, d), rowblk)],
        out_shape=[jax.ShapeDtypeStruct((m, d), BF16), k_shape,
                   jax.ShapeDtypeStruct((m, n_heads, PAIR), F32),
                   jax.ShapeDtypeStruct((m, d), BF16)],
        scratch_shapes=[pltpu.VMEM((tm, d), BF16), pltpu.VMEM((tm, d), BF16)],
        compiler_params=_params("parallel", "arbitrary"),
        name="qkv_proj",
    )(x, g_q, g_kv, w, jnp.tile(q_norm.astype(F32), 2).reshape(1, PAIR), pair_gain,
      k_norm.astype(F32).reshape(HEAD_DIM, 1))


def _bias_tiles_kernel(rb_ref, o_ref, *, t):
    h = pl.program_id(0)
    i = lax.broadcasted_iota(jnp.int32, (t, t), 0)
    j = lax.broadcasted_iota(jnp.int32, (t, t), 1)
    for off in range(2):
        n = i - j + off * t
        bucket = _bucket(n)
        tile = jnp.zeros((t, t), F32)
        for b in range(N_BUCKETS):
            tile = jnp.where(bucket == b, rb_ref[b, h], tile)
        o_ref[0, off] = jnp.where(n >= 0, (tile - rb_ref[N_BUCKETS - 1, h]) * LOG2E, -jnp.inf)


def _bias_tiles(rel_bias, t):
    n_heads = rel_bias.shape[1]
    return pl.pallas_call(
        functools.partial(_bias_tiles_kernel, t=t),
        grid=(n_heads,),
        in_specs=[pl.BlockSpec(memory_space=pltpu.SMEM)],
        out_specs=pl.BlockSpec((1, 2, t, t), lambda h: (h, 0, 0, 0)),
        out_shape=jax.ShapeDtypeStruct((n_heads, 2, t, t), F32),
        compiler_params=_params("parallel"),
        name="bias_tiles",
    )(rel_bias)


def _lambda_value(lam_ref, lambda_init):
    s1 = jnp.sum(lam_ref[0:1, :] * lam_ref[1:2, :], axis=-1, keepdims=True)
    s2 = jnp.sum(lam_ref[2:3, :] * lam_ref[3:4, :], axis=-1, keepdims=True)
    return jnp.exp(s1) - jnp.exp(s2) + lambda_init


def _attn_kernel(q_ref, k_ref, v_ref, bias_ref, rb_ref, lam_ref, sg_ref, o_ref,
                 kb_ref, kb2_ref, q12_ref, m_ref, l_ref, acc_ref, *, t, rc_far, rc_near, nq, lambda_init):
    h = pl.program_id(1)
    qi = pl.program_id(2)

    @pl.when(qi == 0)
    def _():
        for ki in range(nq):
            kb_ref[ki] = k_ref[0, :, ki * t:(ki + 1) * t].astype(BF16)
        for kp in range(nq // 2):
            kb2_ref[kp] = k_ref[0, :, kp * 2 * t:(kp + 1) * 2 * t].astype(BF16)

    q = q_ref[0]
    lo = lax.broadcasted_iota(jnp.int32, (t, PAIR), 1) < HEAD_DIM
    zero = jnp.zeros_like(q)
    q12_ref[0:t, :] = jnp.where(lo, q, zero)
    q12_ref[t:2 * t, :] = jnp.where(lo, zero, q)

    m_ref[...] = jnp.full(m_ref.shape, -jnp.inf, F32)
    l_ref[...] = jnp.zeros(l_ref.shape, F32)
    acc_ref[...] = jnp.zeros(acc_ref.shape, F32)

    far_bias = rb_ref[N_BUCKETS - 1, h] * LOG2E

    def block(rc, keys_of, base, cols_of, near=None):
        for c in range(2 * t // rc):
            rows = slice(c * rc, (c + 1) * rc)
            r0 = (c * rc) % t
            ncols = cols_of(r0)
            s = jnp.dot(q12_ref[rows, :], keys_of(ncols), preferred_element_type=F32)
            tiles = []
            for k in range(ncols // LANES):
                sk = s[:, k * LANES:(k + 1) * LANES]
                if near is not None and near * t + r0 - (k * LANES + LANES - 1) < MAX_DISTANCE:
                    sk = sk + bias_ref[0, near, r0:r0 + rc, k * LANES:(k + 1) * LANES]
                tiles.append(sk)
            m_prev = m_ref[rows, :]
            part = tiles[0]
            for sk in tiles[1:]:
                part = jnp.maximum(part, sk)
            m_new = jnp.maximum(m_prev, jnp.max(part, axis=-1, keepdims=True) + far_bias)
            alpha = jnp.exp2(m_prev - m_new)
            m_sub = m_new - far_bias
            ps = [jnp.exp2(sk - m_sub) for sk in tiles]
            psum = ps[0]
            for pk in ps[1:]:
                psum = psum + pk
            l_ref[rows, :] = alpha * l_ref[rows, :] + psum
            pv = jnp.dot(jnp.concatenate(ps, axis=1).astype(BF16), v_ref[0, pl.ds(base, ncols), :],
                         preferred_element_type=F32)
            acc_ref[rows, :] = alpha * acc_ref[rows, :] + pv
            m_ref[rows, :] = m_new

    n_far = jnp.maximum(qi - 1, 0)

    def far_pair(kp, carry):
        block(rc_far, lambda n: kb2_ref[kp, :, 0:n], pl.multiple_of(kp * 2 * t, 2 * t), lambda r0: 2 * t)
        return carry

    lax.fori_loop(0, n_far // 2, far_pair, 0)

    @pl.when(n_far % 2 == 1)
    def _():
        ki = n_far - 1
        block(rc_far, lambda n: kb_ref[ki, :, 0:n], pl.multiple_of(ki * t, t), lambda r0: t)

    @pl.when(qi >= 1)
    def _():
        ki = qi - 1
        block(rc_far, lambda n: kb_ref[ki, :, 0:n], pl.multiple_of(ki * t, t), lambda r0: t, near=1)

    block(rc_near, lambda n: kb_ref[qi, :, 0:n], pl.multiple_of(qi * t, t), lambda r0: r0 + rc_near, near=0)

    l = jnp.sum(l_ref[...], axis=-1, keepdims=True)
    o1 = acc_ref[0:t, :] / l[0:t, :]
    o2 = acc_ref[t:2 * t, :] / l[t:2 * t, :]
    o = o1 - _lambda_value(lam_ref, lambda_init) * o2
    o_ref[0] = (_rms_scale(o) * sg_ref[...] * (1.0 - lambda_init)).astype(o_ref.dtype)


def _prompt_attention(q, k_t, v, rel_bias, lam_rows, subln, lambda_init, bsz, seq):
    d = q.shape[-1]
    n_heads = d // PAIR
    t = min(seq, 512)
    assert seq % t == 0 and t >= MAX_DISTANCE
    nq = seq // t
    tiles = _bias_tiles(rel_bias, t)
    rc_far, rc_near = t, max(t // 2, LANES)
    assert t % rc_near == 0 and rc_near % LANES == 0
    kern = functools.partial(_attn_kernel, t=t, rc_far=rc_far, rc_near=rc_near, nq=nq, lambda_init=lambda_init)
    const = lambda b, h, i: (0, 0)
    return pl.pallas_call(
        kern,
        grid=(bsz, n_heads, nq),
        in_specs=[pl.BlockSpec((1, t, PAIR), lambda b, h, i: (b, i, h)),
                  pl.BlockSpec((1, PAIR, seq), lambda b, h, i: (b, h, 0)),
                  pl.BlockSpec((1, seq, PAIR), lambda b, h, i: (b, 0, h)),
                  pl.BlockSpec((1, 2, t, t), lambda b, h, i: (h, 0, 0, 0)),
                  pl.BlockSpec(memory_space=pltpu.SMEM),
                  pl.BlockSpec((4, LANES), const),
                  pl.BlockSpec((1, PAIR), const)],
        out_specs=pl.BlockSpec((1, t, PAIR), lambda b, h, i: (b, i, h)),
        out_shape=jax.ShapeDtypeStruct((bsz, seq, d), BF16),
        scratch_shapes=[pltpu.VMEM((nq, PAIR, t), BF16), pltpu.VMEM((max(nq // 2, 1), PAIR, 2 * t), BF16),
                        pltpu.VMEM((2 * t, PAIR), BF16),
                        pltpu.VMEM((2 * t, LANES), F32), pltpu.VMEM((2 * t, LANES), F32),
                        pltpu.VMEM((2 * t, PAIR), F32)],
        compiler_params=_params("parallel", "parallel", "arbitrary"),
        name="prompt_attention",
    )(q.reshape(bsz, seq, d), k_t, v.reshape(bsz, seq, d), tiles, rel_bias, lam_rows, subln)


def _sample_attn_kernel(pt_ref, q_ref, kn_ref, vn_ref, *rest, pages_per_step, past, n_heads, lambda_init):
    k_refs = rest[0:pages_per_step]
    v_refs = rest[pages_per_step:2 * pages_per_step]
    rb_ref, lam_ref, sg_ref, o_ref, qb_ref, bias_ref, m_ref, l_ref, acc_ref = rest[2 * pages_per_step:]
    step_id = pl.program_id(1)
    last = pl.num_programs(1) - 1
    nrow = 2 * n_heads
    d = n_heads * PAIR
    keys = pages_per_step * PAGE_SIZE

    @pl.when(step_id == 0)
    def _():
        bias_ref[...] = jnp.broadcast_to(rb_ref[:, N_BUCKETS - 1:N_BUCKETS] * LOG2E, (nrow, keys))

    @pl.when(step_id == last)
    def _():
        key = lax.broadcasted_iota(jnp.int32, (nrow, keys), 1)
        bucket = _bucket(past - (step_id * keys + key))
        bias = jnp.zeros((nrow, keys), F32)
        for b in range(N_BUCKETS):
            bias = jnp.where(bucket == b, rb_ref[:, b:b + 1], bias)
        bias_ref[...] = bias * LOG2E

    @pl.when(step_id == 0)
    def _():
        q = q_ref[0].astype(F32)
        row = lax.broadcasted_iota(jnp.int32, (nrow, d), 0)
        col = lax.broadcasted_iota(jnp.int32, (nrow, d), 1)
        qblk = jnp.where(jnp.right_shift(col, 6) == row, jnp.broadcast_to(q, (nrow, d)), 0.0)
        qb_ref[...] = qblk.astype(BF16)
        m_ref[...] = jnp.sum(qblk * kn_ref[0], axis=-1, keepdims=True) + rb_ref[:, 0:1] * LOG2E
        l_ref[...] = jnp.ones(l_ref.shape, F32)
        vn = vn_ref[0].astype(F32)
        for h in range(n_heads):
            acc_ref[2 * h:2 * h + 2, :] = jnp.broadcast_to(vn[:, h * PAIR:(h + 1) * PAIR], (2, PAIR))

    qb = qb_ref[...]
    s = jnp.concatenate([jnp.dot(qb, k_refs[g][0].astype(BF16), preferred_element_type=F32)
                         for g in range(pages_per_step)], axis=1) + bias_ref[...]
    m_prev = m_ref[...]
    m_new = jnp.maximum(m_prev, jnp.max(s, axis=-1, keepdims=True))
    alpha = jnp.exp2(m_prev - m_new)
    p = jnp.exp2(s - m_new)
    l_ref[...] = alpha * l_ref[...] + jnp.sum(p, axis=-1, keepdims=True)
    pb = p.astype(BF16)
    row_head = jnp.right_shift(lax.broadcasted_iota(jnp.int32, (nrow, PAIR), 0), 1)
    pv = jnp.zeros((nrow, PAIR), F32)
    for h in range(n_heads):
        v_h = jnp.concatenate([v_refs[g][0, pl.ds(h, PAGE_SIZE, stride=n_heads), :].astype(BF16)
                               for g in range(pages_per_step)], axis=0)
        pv = jnp.where(row_head == h, jnp.dot(pb, v_h, preferred_element_type=F32), pv)
    acc_ref[...] = alpha * acc_ref[...] + pv
    m_ref[...] = m_new

    @pl.when(step_id == last)
    def _():
        rowc = lax.broadcasted_iota(jnp.int32, (nrow, 1), 0)
        lam = _lambda_value(lam_ref, lambda_init)
        coef = jnp.where(jnp.bitwise_and(rowc, 1) == 0, 1.0, -lam) / l_ref[...]
        scaled = acc_ref[...] * coef
        o = scaled + pltpu.roll(scaled, nrow - 1, 0)
        on = _rms_scale(o) * sg_ref[...] * (1.0 - lambda_init)
        for h in range(n_heads):
            o_ref[0, :, h * PAIR:(h + 1) * PAIR] = on[2 * h:2 * h + 1, :]


def _sample_attention(q, k_new, v_new, cache_k, cache_v, page_table, rel_rows, lam_rows, subln, lambda_init):
    bsz, d = q.shape
    n_heads = d // PAIR
    n_pages = page_table.shape[1]
    past = n_pages * PAGE_SIZE
    pages_per_step = max(g for g in (8, 4, 2, 1) if n_pages % g == 0)
    n_pool = cache_k.shape[0]
    kern = functools.partial(_sample_attn_kernel, pages_per_step=pages_per_step, past=past,
                             n_heads=n_heads, lambda_init=lambda_init)
    per_b = lambda b, s, pt: (b, 0, 0)
    const = lambda b, s, pt: (0, 0)

    def k_spec(g):
        return pl.BlockSpec((1, d, PAGE_SIZE), lambda b, s, pt: (pt[b, s * pages_per_step + g], 0, 0))

    def v_spec(g):
        return pl.BlockSpec((1, PAGE_SIZE * n_heads, PAIR),
                            lambda b, s, pt: (pt[b, s * pages_per_step + g], 0, 0))

    grid_spec = pltpu.PrefetchScalarGridSpec(
        num_scalar_prefetch=1,
        grid=(bsz, n_pages // pages_per_step),
        in_specs=([pl.BlockSpec((1, 1, d), per_b)] * 3
                  + [k_spec(g) for g in range(pages_per_step)]
                  + [v_spec(g) for g in range(pages_per_step)]
                  + [pl.BlockSpec((2 * n_heads, N_BUCKETS), const),
                     pl.BlockSpec((4, LANES), const),
                     pl.BlockSpec((1, PAIR), const)]),
        out_specs=pl.BlockSpec((1, 1, d), per_b),
        scratch_shapes=[pltpu.VMEM((2 * n_heads, d), BF16),
                        pltpu.VMEM((2 * n_heads, pages_per_step * PAGE_SIZE), F32),
                        pltpu.VMEM((2 * n_heads, 1), F32), pltpu.VMEM((2 * n_heads, 1), F32),
                        pltpu.VMEM((2 * n_heads, PAIR), F32)],
    )
    ck = jnp.transpose(cache_k, (0, 2, 3, 4, 1)).reshape(n_pool, d, PAGE_SIZE)
    cv = cache_v.reshape(n_pool, PAGE_SIZE * n_heads, PAIR)
    return pl.pallas_call(
        kern,
        grid_spec=grid_spec,
        out_shape=jax.ShapeDtypeStruct((bsz, 1, d), F32),
        compiler_params=_params("parallel", "arbitrary"),
        name="sample_attention",
    )(page_table, q.reshape(bsz, 1, d), k_new.reshape(bsz, 1, d), v_new.reshape(bsz, 1, d),
      *([ck] * pages_per_step), *([cv] * pages_per_step), rel_rows, lam_rows, subln).reshape(bsz, d)


def _pad_lanes(v):
    return jnp.pad(v.astype(F32), (0, LANES - v.shape[0])).reshape(1, LANES)


def kernel(x_prompt, x_sample, state_conv, state_ssm, cache_k, cache_v, page_table, norm_mix, norm_ffn, w_in, conv_w, conv_b, dt_bias, a_log, d_skip, ssm_norm, w_out_ssm, norm_kv, w_kv, k_norm, w_q, q_norm, lambda_q1, lambda_k1, lambda_q2, lambda_k2, subln, w_o, rel_bias, w_gate_up, w_down):
    bsz, seq, d = x_prompt.shape
    dec = x_sample.shape[0]
    assert x_sample.shape[1] == 1
    n_ssm_layers, ssm_heads = dt_bias.shape
    depth = norm_mix.shape[0]
    assert n_ssm_layers == 1 and depth == 2
    d_inner = w_out_ssm.shape[1]
    n_state = state_ssm.shape[-1]
    assert d_inner == ssm_heads * HEAD_DIM and n_state == LANES and ssm_heads <= LANES
    n_heads = d // PAIR

    xp = x_prompt.reshape(bsz * seq, d)
    xs = x_sample.reshape(dec, d)
    row = lambda v: v.astype(F32).reshape(1, -1)

    zx_cols = 2 * d_inner + 2 * SSM_GROUPS * n_state
    w_zx = w_in[0][:, :zx_cols].astype(BF16)
    w_dt = jnp.pad(w_in[0][:, zx_cols:], ((0, 0), (0, LANES - ssm_heads))).astype(BF16)
    g_mix0 = row(norm_mix[0])
    ssd_args = (conv_w[0], row(conv_b[0]), _pad_lanes(dt_bias[0]), _pad_lanes(a_log[0]),
                row(jnp.repeat(d_skip[0], HEAD_DIM)), row(ssm_norm[0]))
    w_out = w_out_ssm[0].astype(BF16)
    w_gu0, w_dn0 = w_gate_up[0].astype(BF16), w_down[0].astype(BF16)

    zx_p, dt_p = _in_proj(xp, g_mix0, w_zx, w_dt)
    yg_p, conv_p, ssm_p = _ssd_prompt(zx_p, dt_p, *ssd_args, bsz, seq, d_inner, n_state)
    xp = _linear_res(yg_p, w_out, xp)
    xp = _ffn(xp, row(norm_ffn[0]), w_gu0, w_dn0)

    zx_s, dt_s = _in_proj(xs, g_mix0, w_zx, w_dt)
    yg_s, conv_s, ssm_s = _ssd_step(zx_s, dt_s, state_conv[0], state_ssm[0], *ssd_args, d_inner, n_state)
    xs = _linear_res(yg_s.reshape(dec, d_inner), w_out, xs)
    xs = _ffn(xs, row(norm_ffn[0]), w_gu0, w_dn0)

    lambda_init = 0.8 - 0.6 * math.exp(-0.3 * 1)
    qkv_args = (row(norm_mix[1]), row(norm_kv), w_q[0], w_kv[:, :d], w_kv[:, d:], q_norm[0], k_norm)
    lam_rows = jnp.pad(jnp.stack([lambda_q1[0], lambda_k1[0], lambda_q2[0], lambda_k2[0]]).astype(F32),
                       ((0, 0), (0, LANES - HEAD_DIM)))
    sg = row(subln[0])
    w_oo = w_o[0].astype(BF16)
    w_gu1, w_dn1 = w_gate_up[1].astype(BF16), w_down[1].astype(BF16)

    q_p, kt_p, v_p, vd_p = _qkv(xp, *qkv_args, seq=seq)
    o_p = _prompt_attention(q_p, kt_p, vd_p, rel_bias.astype(F32), lam_rows, sg, lambda_init, bsz, seq)
    xp = _linear_res(o_p.reshape(bsz * seq, d), w_oo, xp)
    xp = _ffn(xp, row(norm_ffn[1]), w_gu1, w_dn1)

    q_s, k_s, v_s, vd_s = _qkv(xs, *qkv_args)
    rel_rows = jnp.repeat(rel_bias.astype(F32).T, 2, axis=0)
    o_s = _sample_attention(q_s.astype(F32), k_s, vd_s.astype(F32), cache_k, cache_v, page_table,
                            rel_rows, lam_rows, sg, lambda_init)
    xs = _linear_res(o_s, w_oo, xs)
    xs = _ffn(xs, row(norm_ffn[1]), w_gu1, w_dn1)

    k_p = jnp.transpose(kt_p.reshape(bsz, n_heads, 2, HEAD_DIM, seq), (0, 4, 1, 2, 3))
    return (xp.reshape(bsz, seq, d), xs.reshape(dec, 1, d),
            conv_p[None], ssm_p.reshape(1, bsz, ssm_heads, HEAD_DIM, n_state),
            k_p, v_p.reshape(bsz, seq, n_heads, PAIR),
            conv_s[None], ssm_s.reshape(1, dec, ssm_heads, HEAD_DIM, n_state),
            k_s.reshape(dec, 1, n_heads, 2, HEAD_DIM), v_s.reshape(dec, 1, n_heads, PAIR))
```

```python
import functools
import math

import jax
import jax.numpy as jnp
from jax import lax
from jax.experimental import pallas as pl
from jax.experimental.pallas import tpu as pltpu

F32 = jnp.float32
BF16 = jnp.bfloat16

EPS = 1e-6
LANES = 128
HEAD_DIM = 64
PAIR = 2 * HEAD_DIM
SSD_CHUNK = 128
SSM_GROUPS = 4
CONV_TAPS = 4
HALO = 8
N_BUCKETS = 32
MAX_EXACT = N_BUCKETS // 2
MAX_DISTANCE = 128
PAGE_SIZE = 128
LOG2E = math.log2(math.e)
MXU_WIDTH = 256
VMEM_LIMIT = 52 * 1024 * 1024


def _params(*semantics):
    return pltpu.CompilerParams(dimension_semantics=semantics, vmem_limit_bytes=VMEM_LIMIT)


def _row_tile(m, preferred):
    if m <= preferred:
        return m
    t = preferred
    while m % t or t % 16:
        t -= 1
    return t


def _nt_dot(a, b):
    return lax.dot_general(a, b, (((1,), (1,)), ((), ())), preferred_element_type=F32)


def _tn_dot(a, b):
    return lax.dot_general(a, b, (((0,), (0,)), ((), ())), preferred_element_type=F32)


def _rms_scale(x):
    return x * lax.rsqrt(jnp.mean(x * x, axis=-1, keepdims=True) + EPS)


def _log1p(u):
    w = 1.0 + u
    return jnp.where(w == 1.0, u, jnp.log(w) * (u / (w - 1.0)))


def _softplus(x):
    return jnp.maximum(x, 0.0) + _log1p(jnp.exp(-jnp.abs(x)))


def _lane_pair(arr, j, lo):
    return jnp.where(lo, arr[:, 2 * j:2 * j + 1], arr[:, 2 * j + 1:2 * j + 2])


def _bucket(n):
    n = jnp.maximum(n, 0)
    nf = jnp.maximum(n, 1).astype(F32)
    large = MAX_EXACT + (jnp.log(nf / MAX_EXACT) / math.log(MAX_DISTANCE / MAX_EXACT)
                         * (N_BUCKETS - MAX_EXACT)).astype(jnp.int32)
    large = jnp.minimum(large, N_BUCKETS - 1)
    return jnp.where(n < MAX_EXACT, n, large)


def _in_proj_kernel(x_ref, g_ref, w_ref, wdt_ref, zx_ref, dt_ref, xn_ref, *, tn):
    @pl.when(pl.program_id(1) == 0)
    def _():
        xn_ref[...] = (_rms_scale(x_ref[...]) * g_ref[...]).astype(BF16)
        dt_ref[...] = jnp.dot(xn_ref[...], wdt_ref[...], preferred_element_type=F32)

    xn = xn_ref[...]
    for c0 in range(0, tn, 2 * MXU_WIDTH):
        cols = slice(c0, min(c0 + 2 * MXU_WIDTH, tn))
        zx_ref[:, cols] = jnp.dot(xn, w_ref[:, cols], preferred_element_type=F32)


def _in_proj(x, g, w_zx, w_dt):
    m, d = x.shape
    n = w_zx.shape[1]
    tm = _row_tile(m, 1024)
    tn = n // 2 if n % (2 * LANES) == 0 else n
    return pl.pallas_call(
        functools.partial(_in_proj_kernel, tn=tn),
        grid=(m // tm, n // tn),
        in_specs=[pl.BlockSpec((tm, d), lambda i, j: (i, 0)),
                  pl.BlockSpec((1, d), lambda i, j: (0, 0)),
                  pl.BlockSpec((d, tn), lambda i, j: (0, j)),
                  pl.BlockSpec((d, LANES), lambda i, j: (0, 0))],
        out_specs=[pl.BlockSpec((tm, tn), lambda i, j: (i, j)),
                   pl.BlockSpec((tm, LANES), lambda i, j: (i, 0))],
        out_shape=[jax.ShapeDtypeStruct((m, n), F32), jax.ShapeDtypeStruct((m, LANES), F32)],
        scratch_shapes=[pltpu.VMEM((tm, d), BF16)],
        compiler_params=_params("parallel", "arbitrary"),
        name="in_proj",
    )(x, g, w_zx, w_dt)


def _ssd_kernel(z_ref, xs_ref, bc_ref, dt_ref, cw_ref, cb_ref, dtb_ref, alog_ref, dsk_ref, ng_ref,
                yg_ref, conv_ref, h_ref, xtail_ref, bctail_ref, ht_ref, *, cl, d_inner, n_state):
    c = pl.program_id(1)
    n_pairs = d_inner // PAIR
    pairs_per_group = n_pairs // SSM_GROUPS
    gb = SSM_GROUPS * n_state

    @pl.when(c == 0)
    def _():
        xtail_ref[...] = jnp.zeros((HALO, d_inner), F32)
        bctail_ref[...] = jnp.zeros((HALO, 2 * gb), F32)
        ht_ref[...] = jnp.zeros_like(ht_ref)

    row8 = lax.broadcasted_iota(jnp.int32, (HALO, 1), 0)

    def conv(in_ref, tail_ref, c0, width):
        x = in_ref[...]
        tail = tail_ref[...]
        acc = cb_ref[:, c0:c0 + width] + cw_ref[CONV_TAPS - 1:CONV_TAPS, c0:c0 + width] * x
        for back in range(1, CONV_TAPS):
            shifted = pltpu.roll(x, back, 0)
            head = jnp.where(row8 < back, pltpu.roll(tail, back, 0), shifted[0:HALO, :])
            shifted = jnp.concatenate([head, shifted[HALO:, :]], axis=0)
            k = CONV_TAPS - 1 - back
            acc = acc + cw_ref[k:k + 1, c0:c0 + width] * shifted
        tail_ref[...] = x[cl - HALO:cl, :]
        return jax.nn.silu(acc)

    xc = conv(xs_ref, xtail_ref, 0, d_inner)
    bcc = conv(bc_ref, bctail_ref, d_inner, 2 * gb)

    @pl.when(c == pl.num_programs(1) - 1)
    def _():
        conv_ref[0, :, 0:d_inner] = xs_ref[cl - (CONV_TAPS - 1):cl, :]
        conv_ref[0, :, d_inner:d_inner + 2 * gb] = bc_ref[cl - (CONV_TAPS - 1):cl, :]

    dt = _softplus(dt_ref[...] + dtb_ref[...])
    dta = dt * (-jnp.exp(alog_ref[...]))
    causal = (lax.broadcasted_iota(jnp.int32, (cl, cl), 0)
              >= lax.broadcasted_iota(jnp.int32, (cl, cl), 1))
    tri = jnp.where(causal, 1.0, 0.0).astype(BF16)
    a_cum = jnp.zeros((cl, LANES), F32)
    rest = dta
    for _ in range(3):
        term = rest.astype(BF16)
        a_cum = a_cum + jnp.dot(tri, term, preferred_element_type=F32)
        rest = rest - term.astype(F32)
    a_cum = a_cum * LOG2E
    a_cum_t = a_cum.T
    dt_t = dt.T
    dte_t = jnp.exp2(a_cum_t[:, cl - 1:cl] - a_cum_t) * dt_t
    cd_row = jnp.exp2(a_cum[cl - 1:cl, :])

    lo = lax.broadcasted_iota(jnp.int32, (1, PAIR), 1) < HEAD_DIM

    for g in range(SSM_GROUPS):
        b_f = bcc[:, g * n_state:(g + 1) * n_state]
        bt_g = b_f.T
        c_g = bcc[:, gb + g * n_state:gb + (g + 1) * n_state].astype(BF16)
        cb_g = _nt_dot(c_g, b_f.astype(BF16))
        gated = []
        ssq = jnp.zeros((cl, 1), F32)
        for jj in range(pairs_per_group):
            j = g * pairs_per_group + jj
            sl = slice(j * PAIR, (j + 1) * PAIR)
            x_p = xc[:, sl]
            x_half = (jnp.where(lo, x_p, 0.0).astype(BF16), jnp.where(lo, 0.0, x_p).astype(BF16))
            y = None
            st = None
            ea = []
            for half in range(2):
                r = 2 * j + half
                col = jnp.broadcast_to(a_cum[:, r:r + 1], (cl, cl))
                decay = jnp.exp2(jnp.where(causal, col - a_cum_t[r:r + 1, :], -jnp.inf))
                w = (cb_g * decay * dt_t[r:r + 1, :]).astype(BF16)
                part = jnp.dot(w, x_half[half], preferred_element_type=F32)
                y = part if y is None else y + part
                part = jnp.dot((bt_g * dte_t[r:r + 1, :]).astype(BF16), x_half[half],
                               preferred_element_type=F32)
                st = part if st is None else st + part
                ea.append(jnp.exp2(col))
            ht_p = ht_ref[:, sl]
            y = y + jnp.dot(c_g, ht_p.astype(BF16), preferred_element_type=F32) * jnp.where(lo, ea[0], ea[1])
            ht_ref[:, sl] = ht_p * _lane_pair(cd_row, j, lo) + st
            y = y + dsk_ref[:, sl] * x_p
            gy = y * jax.nn.silu(z_ref[:, sl])
            ssq = ssq + jnp.sum(gy * gy, axis=-1, keepdims=True)
            gated.append(gy)
        scale = lax.rsqrt(ssq / (pairs_per_group * PAIR) + EPS)
        for jj in range(pairs_per_group):
            sl = slice((g * pairs_per_group + jj) * PAIR, (g * pairs_per_group + jj + 1) * PAIR)
            yg_ref[:, sl] = (gated[jj] * scale * ng_ref[:, sl]).astype(BF16)

    @pl.when(c == pl.num_programs(1) - 1)
    def _():
        for j in range(n_pairs):
            sl = slice(j * PAIR, (j + 1) * PAIR)
            h_ref[0, sl, :] = ht_ref[:, sl].T


def _ssd_prompt(zx, dt_raw, conv_w, conv_b, dt_bias, a_log, d_skip, norm_g, bsz, seq, d_inner, n_state):
    cl = SSD_CHUNK
    assert seq % cl == 0
    nc = seq // cl
    gb = SSM_GROUPS * n_state
    conv_dim = d_inner + 2 * gb
    xs_blk = d_inner // d_inner
    bc_blk = (2 * d_inner) // (2 * gb)
    assert (2 * d_inner) % (2 * gb) == 0
    kern = functools.partial(_ssd_kernel, cl=cl, d_inner=d_inner, n_state=n_state)
    row = lambda b, c: b * nc + c
    const = lambda b, c: (0, 0)
    return pl.pallas_call(
        kern,
        grid=(bsz, nc),
        in_specs=[pl.BlockSpec((cl, d_inner), lambda b, c: (row(b, c), 0)),
                  pl.BlockSpec((cl, d_inner), lambda b, c: (row(b, c), xs_blk)),
                  pl.BlockSpec((cl, 2 * gb), lambda b, c: (row(b, c), bc_blk)),
                  pl.BlockSpec((cl, LANES), lambda b, c: (row(b, c), 0)),
                  pl.BlockSpec((CONV_TAPS, conv_dim), const),
                  pl.BlockSpec((1, conv_dim), const),
                  pl.BlockSpec((1, LANES), const),
                  pl.BlockSpec((1, LANES), const),
                  pl.BlockSpec((1, d_inner), const),
                  pl.BlockSpec((1, d_inner), const)],
        out_specs=[pl.BlockSpec((cl, d_inner), lambda b, c: (row(b, c), 0)),
                   pl.BlockSpec((1, CONV_TAPS - 1, conv_dim), lambda b, c: (b, 0, 0)),
                   pl.BlockSpec((1, d_inner, n_state), lambda b, c: (b, 0, 0))],
        out_shape=[jax.ShapeDtypeStruct((bsz * seq, d_inner), BF16),
                   jax.ShapeDtypeStruct((bsz, CONV_TAPS - 1, conv_dim), F32),
                   jax.ShapeDtypeStruct((bsz, d_inner, n_state), F32)],
        scratch_shapes=[pltpu.VMEM((HALO, d_inner), F32), pltpu.VMEM((HALO, 2 * gb), F32),
                        pltpu.VMEM((n_state, d_inner), F32)],
        compiler_params=_params("parallel", "arbitrary"),
        name="ssd_prompt",
    )(zx, zx, zx, dt_raw, conv_w, conv_b, dt_bias, a_log, d_skip, norm_g)


def _ssd_step_kernel(zx_ref, dt_ref, cs_ref, h_ref, cw_ref, cb_ref, dtb_ref, alog_ref, dsk_ref, ng_ref,
                     yg_ref, cso_ref, ho_ref, *, d_inner, n_state):
    n_pairs = d_inner // PAIR
    pairs_per_group = n_pairs // SSM_GROUPS
    gb = SSM_GROUPS * n_state
    zx = zx_ref[0]
    z = zx[:, 0:d_inner]
    xbc = zx[:, d_inner:]
    prev = cs_ref[0]
    acc = cb_ref[...] + cw_ref[CONV_TAPS - 1:CONV_TAPS, :] * xbc
    for k in range(CONV_TAPS - 1):
        acc = acc + cw_ref[k:k + 1, :] * prev[k:k + 1, :]
    cso_ref[0, 0:CONV_TAPS - 2, :] = prev[1:CONV_TAPS - 1, :]
    cso_ref[0, CONV_TAPS - 2:CONV_TAPS - 1, :] = xbc
    act = jax.nn.silu(acc)
    xs = act[:, 0:d_inner]
    dt = _softplus(dt_ref[0] + dtb_ref[...])
    da = jnp.exp(dt * (-jnp.exp(alog_ref[...])))

    lo = lax.broadcasted_iota(jnp.int32, (1, PAIR), 1) < HEAD_DIM
    row_lo = lax.broadcasted_iota(jnp.int32, (PAIR, 1), 0) < HEAD_DIM
    eye = (lax.broadcasted_iota(jnp.int32, (PAIR, PAIR), 0)
           == lax.broadcasted_iota(jnp.int32, (PAIR, PAIR), 1))

    for g in range(SSM_GROUPS):
        b_g = act[:, d_inner + g * n_state:d_inner + (g + 1) * n_state]
        c_g = act[:, d_inner + gb + g * n_state:d_inner + gb + (g + 1) * n_state]
        gated = []
        ssq = jnp.zeros((1, 1), F32)
        for jj in range(pairs_per_group):
            j = g * pairs_per_group + jj
            sl = slice(j * PAIR, (j + 1) * PAIR)
            x_p = xs[:, sl]
            xdt = x_p * _lane_pair(dt, j, lo)
            x_col = jnp.sum(jnp.where(eye, jnp.broadcast_to(xdt, (PAIR, PAIR)), 0.0), axis=1, keepdims=True)
            d_col = jnp.where(row_lo, da[:, 2 * j:2 * j + 1], da[:, 2 * j + 1:2 * j + 2])
            h_new = d_col * h_ref[0, sl, :] + x_col * b_g
            ho_ref[0, sl, :] = h_new
            y_col = jnp.sum(h_new * c_g, axis=1, keepdims=True)
            y = jnp.sum(jnp.where(eye, jnp.broadcast_to(y_col, (PAIR, PAIR)), 0.0), axis=0, keepdims=True)
            y = y + dsk_ref[:, sl] * x_p
            gy = y * jax.nn.silu(z[:, sl])
            ssq = ssq + jnp.sum(gy * gy, axis=-1, keepdims=True)
            gated.append(gy)
        scale = lax.rsqrt(ssq / (pairs_per_group * PAIR) + EPS)
        for jj in range(pairs_per_group):
            sl = slice((g * pairs_per_group + jj) * PAIR, (g * pairs_per_group + jj + 1) * PAIR)
            yg_ref[0, :, sl] = gated[jj] * scale * ng_ref[:, sl]


def _ssd_step(zx, dt_raw, state_conv, state_ssm, conv_w, conv_b, dt_bias, a_log, d_skip, norm_g,
              d_inner, n_state):
    bsz = zx.shape[0]
    conv_dim = conv_w.shape[1]
    kern = functools.partial(_ssd_step_kernel, d_inner=d_inner, n_state=n_state)
    const = lambda b: (0, 0)
    per_b = lambda b: (b, 0, 0)
    return pl.pallas_call(
        kern,
        grid=(bsz,),
        in_specs=[pl.BlockSpec((1, 1, d_inner + conv_dim), per_b),
                  pl.BlockSpec((1, 1, LANES), per_b),
                  pl.BlockSpec((1, CONV_TAPS - 1, conv_dim), per_b),
                  pl.BlockSpec((1, d_inner, n_state), per_b),
                  pl.BlockSpec((CONV_TAPS, conv_dim), const),
                  pl.BlockSpec((1, conv_dim), const),
                  pl.BlockSpec((1, LANES), const),
                  pl.BlockSpec((1, LANES), const),
                  pl.BlockSpec((1, d_inner), const),
                  pl.BlockSpec((1, d_inner), const)],
        out_specs=[pl.BlockSpec((1, 1, d_inner), per_b),
                   pl.BlockSpec((1, CONV_TAPS - 1, conv_dim), per_b),
                   pl.BlockSpec((1, d_inner, n_state), per_b)],
        out_shape=[jax.ShapeDtypeStruct((bsz, 1, d_inner), F32),
                   jax.ShapeDtypeStruct((bsz, CONV_TAPS - 1, conv_dim), F32),
                   jax.ShapeDtypeStruct((bsz, d_inner, n_state), F32)],
        compiler_params=_params("parallel"),
        name="ssd_step",
    )(zx.reshape(bsz, 1, -1), dt_raw.reshape(bsz, 1, LANES), state_conv,
      state_ssm.reshape(bsz, d_inner, n_state), conv_w, conv_b, dt_bias, a_log, d_skip, norm_g)


def _linear_res_kernel(a_ref, w_ref, r_ref, o_ref):
    o_ref[...] = r_ref[...] + jnp.dot(a_ref[...].astype(BF16), w_ref[...], preferred_element_type=F32)


def _linear_res(a, w, res):
    m, k = a.shape
    n = w.shape[1]
    tm = _row_tile(m, 512)
    return pl.pallas_call(
        _linear_res_kernel,
        grid=(m // tm,),
        in_specs=[pl.BlockSpec((tm, k), lambda i: (i, 0)),
                  pl.BlockSpec((k, n), lambda i: (0, 0)),
                  pl.BlockSpec((tm, n), lambda i: (i, 0))],
        out_specs=pl.BlockSpec((tm, n), lambda i: (i, 0)),
        out_shape=jax.ShapeDtypeStruct((m, n), F32),
        compiler_params=_params("parallel"),
        name="linear_res",
    )(a, w, res)


def _ffn_kernel(x_ref, g_ref, wg_ref, wu_ref, wd_ref, o_ref, xn_ref, *, tf):
    @pl.when(pl.program_id(1) == 0)
    def _():
        x = x_ref[...]
        xn_ref[...] = (_rms_scale(x) * g_ref[...]).astype(BF16)
        o_ref[...] = x

    xn = xn_ref[...]
    acts = []
    for c0 in range(0, tf, MXU_WIDTH):
        cols = slice(c0, min(c0 + MXU_WIDTH, tf))
        gate = jnp.dot(xn, wg_ref[:, cols], preferred_element_type=F32)
        up = jnp.dot(xn, wu_ref[:, cols], preferred_element_type=F32)
        acts.append((jax.nn.silu(gate) * up).astype(BF16))
    o_ref[...] += jnp.dot(jnp.concatenate(acts, axis=1), wd_ref[...], preferred_element_type=F32)


def _ffn(x, g, w_gate_up, w_down):
    m, d = x.shape
    d_ff = w_down.shape[0]
    tm = _row_tile(m, 1024)
    tf = d_ff // 2 if d_ff % (2 * LANES) == 0 else d_ff
    nf = d_ff // tf
    return pl.pallas_call(
        functools.partial(_ffn_kernel, tf=tf),
        grid=(m // tm, nf),
        in_specs=[pl.BlockSpec((tm, d), lambda i, j: (i, 0)),
                  pl.BlockSpec((1, d), lambda i, j: (0, 0)),
                  pl.BlockSpec((d, tf), lambda i, j: (0, j)),
                  pl.BlockSpec((d, tf), lambda i, j: (0, nf + j)),
                  pl.BlockSpec((tf, d), lambda i, j: (j, 0))],
        out_specs=pl.BlockSpec((tm, d), lambda i, j: (i, 0)),
        out_shape=jax.ShapeDtypeStruct((m, d), F32),
        scratch_shapes=[pltpu.VMEM((tm, d), BF16)],
        compiler_params=_params("parallel", "arbitrary"),
        name="ffn",
    )(x, g, w_gate_up, w_gate_up, w_down)


def _qkv_kernel(x_ref, gq_ref, gkv_ref, w_ref, qn_ref, kn_ref, knc_ref, q_ref, k_ref, v4_ref, vd_ref,
                *, n_heads, k_transposed):
    lo = lax.broadcasted_iota(jnp.int32, (1, PAIR), 1) < HEAD_DIM

    def qk_norm(y, gain_ref, out_ref, post):
        for h in range(n_heads):
            yh = y[:, h * PAIR:(h + 1) * PAIR]
            sq = yh * yh
            s_lo = jnp.sum(jnp.where(lo, sq, 0.0), axis=-1, keepdims=True)
            s_hi = jnp.sum(jnp.where(lo, 0.0, sq), axis=-1, keepdims=True)
            r = jnp.where(lo, lax.rsqrt(s_lo / HEAD_DIM + EPS), lax.rsqrt(s_hi / HEAD_DIM + EPS))
            out_ref[:, h * PAIR:(h + 1) * PAIR] = (yh * r * gain_ref[...] * post).astype(out_ref.dtype)

    xh = _rms_scale(x_ref[...])
    xq = (xh * gq_ref[...]).astype(BF16)
    xkv = (xh * gkv_ref[...]).astype(BF16)
    qk_norm(jnp.dot(xq, w_ref[0], preferred_element_type=F32), qn_ref, q_ref, HEAD_DIM ** -0.5 * LOG2E)

    if k_transposed:
        kt = _nt_dot(w_ref[1], xkv)
        for g in range(2 * n_heads):
            rows = slice(g * HEAD_DIM, (g + 1) * HEAD_DIM)
            blk = kt[rows, :]
            r = lax.rsqrt(jnp.mean(blk * blk, axis=0, keepdims=True) + EPS)
            k_ref[0, rows, :] = blk * r * knc_ref[...]
    else:
        qk_norm(jnp.dot(xkv, w_ref[1], preferred_element_type=F32), kn_ref, k_ref, 1.0)

    v = jnp.dot(xkv, w_ref[2], preferred_element_type=F32)
    vd_ref[...] = v.astype(BF16)
    for h in range(n_heads):
        v4_ref[:, h, :] = v[:, h * PAIR:(h + 1) * PAIR]


def _qkv(x, g_q, g_kv, w_q, w_k, w_v, q_norm, k_norm, seq=None):
    m, d = x.shape
    n_heads = d // PAIR
    k_transposed = seq is not None
    tm = _row_tile(m if seq is None else seq, 512)
    w = jnp.stack([w_q, w_k.T if k_transposed else w_k, w_v]).astype(BF16)
    pair_gain = jnp.tile(k_norm.astype(F32), 2).reshape(1, PAIR)
    kern = functools.partial(_qkv_kernel, n_heads=n_heads, k_transposed=k_transposed)
    const = lambda i: (0, 0)
    rowblk = lambda i: (i, 0)
    if k_transposed:
        per_seq = seq // tm
        k_spec = pl.BlockSpec((1, d, tm), lambda i: (i // per_seq, 0, i % per_seq))
        k_shape = jax.ShapeDtypeStruct((m // seq, d, seq), F32)
    else:
        k_spec = pl.BlockSpec((tm, d), rowblk)
        k_shape = jax.ShapeDtypeStruct((m, d), F32)
    return pl.pallas_call(
        kern,
        grid=(m // tm,),
        in_specs=[pl.BlockSpec((tm, d), rowblk),
                  pl.BlockSpec((1, d), const),
                  pl.BlockSpec((1, d), const),
                  pl.BlockSpec((3, d, d), lambda i: (0, 0, 0)),
                  pl.BlockSpec((1, PAIR), const),
                  pl.BlockSpec((1, PAIR), const),
                  pl.BlockSpec((HEAD_DIM, 1), const)],
        out_specs=[pl.BlockSpec((tm, d), rowblk), k_spec,
                   pl.BlockSpec((tm, n_heads, PAIR), lambda i: (i, 0, 0)),
                   pl.BlockSpec((tm, d), rowblk)],
        out_shape=[jax.ShapeDtypeStruct((m, d), BF16), k_shape,
                   jax.ShapeDtypeStruct((m, n_heads, PAIR), F32),
                   jax.ShapeDtypeStruct((m, d), BF16)],
        compiler_params=_params("parallel"),
        name="qkv_proj",
    )(x, g_q, g_kv, w, jnp.tile(q_norm.astype(F32), 2).reshape(1, PAIR), pair_gain,
      k_norm.astype(F32).reshape(HEAD_DIM, 1))


def _bias_tiles_kernel(rb_ref, o_ref, *, t):
    h = pl.program_id(0)
    i = lax.broadcasted_iota(jnp.int32, (t, t), 0)
    j = lax.broadcasted_iota(jnp.int32, (t, t), 1)
    for off in range(2):
        n = i - j + off * t
        bucket = _bucket(n)
        tile = jnp.zeros((t, t), F32)
        for b in range(N_BUCKETS):
            tile = jnp.where(bucket == b, rb_ref[b, h], tile)
        o_ref[0, off] = jnp.where(n >= 0, (tile - rb_ref[N_BUCKETS - 1, h]) * LOG2E, -jnp.inf)


def _bias_tiles(rel_bias, t):
    n_heads = rel_bias.shape[1]
    return pl.pallas_call(
        functools.partial(_bias_tiles_kernel, t=t),
        grid=(n_heads,),
        in_specs=[pl.BlockSpec(memory_space=pltpu.SMEM)],
        out_specs=pl.BlockSpec((1, 2, t, t), lambda h: (h, 0, 0, 0)),
        out_shape=jax.ShapeDtypeStruct((n_heads, 2, t, t), F32),
        compiler_params=_params("parallel"),
        name="bias_tiles",
    )(rel_bias)


def _lambda_value(lam_ref, lambda_init):
    s1 = jnp.sum(lam_ref[0:1, :] * lam_ref[1:2, :], axis=-1, keepdims=True)
    s2 = jnp.sum(lam_ref[2:3, :] * lam_ref[3:4, :], axis=-1, keepdims=True)
    return jnp.exp(s1) - jnp.exp(s2) + lambda_init


def _attn_kernel(q_ref, k_ref, v_ref, bias_ref, rb_ref, lam_ref, sg_ref, o_ref,
                 kb_ref, kb2_ref, q12_ref, m_ref, l_ref, acc_ref, *, t, rc_far, rc_near, nq, lambda_init):
    h = pl.program_id(1)
    qi = pl.program_id(2)

    @pl.when(qi == 0)
    def _():
        for ki in range(nq):
            kb_ref[ki] = k_ref[0, :, ki * t:(ki + 1) * t].astype(BF16)
        for kp in range(nq // 2):
            kb2_ref[kp] = k_ref[0, :, kp * 2 * t:(kp + 1) * 2 * t].astype(BF16)

    q = q_ref[0]
    lo = lax.broadcasted_iota(jnp.int32, (t, PAIR), 1) < HEAD_DIM
    zero = jnp.zeros_like(q)
    q12_ref[0:t, :] = jnp.where(lo, q, zero)
    q12_ref[t:2 * t, :] = jnp.where(lo, zero, q)

    m_ref[...] = jnp.full(m_ref.shape, -jnp.inf, F32)
    l_ref[...] = jnp.zeros(l_ref.shape, F32)
    acc_ref[...] = jnp.zeros(acc_ref.shape, F32)

    far_bias = rb_ref[N_BUCKETS - 1, h] * LOG2E

    def block(rc, keys_of, base, cols_of, near=None):
        for c in range(2 * t // rc):
            rows = slice(c * rc, (c + 1) * rc)
            r0 = (c * rc) % t
            ncols = cols_of(r0)
            s = jnp.dot(q12_ref[rows, :], keys_of(ncols), preferred_element_type=F32)
            tiles = []
            for k in range(ncols // LANES):
                sk = s[:, k * LANES:(k + 1) * LANES]
                if near is not None and near * t + r0 - (k * LANES + LANES - 1) < MAX_DISTANCE:
                    sk = sk + bias_ref[0, near, r0:r0 + rc, k * LANES:(k + 1) * LANES]
                tiles.append(sk)
            m_prev = m_ref[rows, :]
            part = tiles[0]
            for sk in tiles[1:]:
                part = jnp.maximum(part, sk)
            m_new = jnp.maximum(m_prev, jnp.max(part, axis=-1, keepdims=True) + far_bias)
            alpha = jnp.exp2(m_prev - m_new)
            m_sub = m_new - far_bias
            ps = [jnp.exp2(sk - m_sub) for sk in tiles]
            psum = ps[0]
            for pk in ps[1:]:
                psum = psum + pk
            l_ref[rows, :] = alpha * l_ref[rows, :] + psum
            pv = jnp.dot(jnp.concatenate(ps, axis=1).astype(BF16), v_ref[0, pl.ds(base, ncols), :],
                         preferred_element_type=F32)
            acc_ref[rows, :] = alpha * acc_ref[rows, :] + pv
            m_ref[rows, :] = m_new

    n_far = jnp.maximum(qi - 1, 0)

    def far_pair(kp, carry):
        block(rc_far, lambda n: kb2_ref[kp, :, 0:n], pl.multiple_of(kp * 2 * t, 2 * t), lambda r0: 2 * t)
        return carry

    lax.fori_loop(0, n_far // 2, far_pair, 0)

    @pl.when(n_far % 2 == 1)
    def _():
        ki = n_far - 1
        block(rc_far, lambda n: kb_ref[ki, :, 0:n], pl.multiple_of(ki * t, t), lambda r0: t)

    @pl.when(qi >= 1)
    def _():
        ki = qi - 1
        block(rc_far, lambda n: kb_ref[ki, :, 0:n], pl.multiple_of(ki * t, t), lambda r0: t, near=1)

    block(rc_near, lambda n: kb_ref[qi, :, 0:n], pl.multiple_of(qi * t, t), lambda r0: r0 + rc_near, near=0)

    l = jnp.sum(l_ref[...], axis=-1, keepdims=True)
    o1 = acc_ref[0:t, :] / l[0:t, :]
    o2 = acc_ref[t:2 * t, :] / l[t:2 * t, :]
    o = o1 - _lambda_value(lam_ref, lambda_init) * o2
    o_ref[0] = (_rms_scale(o) * sg_ref[...] * (1.0 - lambda_init)).astype(o_ref.dtype)


def _prompt_attention(q, k_t, v, rel_bias, lam_rows, subln, lambda_init, bsz, seq):
    d = q.shape[-1]
    n_heads = d // PAIR
    t = min(seq, 512)
    assert seq % t == 0 and t >= MAX_DISTANCE
    nq = seq // t
    tiles = _bias_tiles(rel_bias, t)
    rc_far, rc_near = t, t
    assert t % rc_near == 0 and rc_near % LANES == 0
    kern = functools.partial(_attn_kernel, t=t, rc_far=rc_far, rc_near=rc_near, nq=nq, lambda_init=lambda_init)
    const = lambda b, h, i: (0, 0)
    return pl.pallas_call(
        kern,
        grid=(bsz, n_heads, nq),
        in_specs=[pl.BlockSpec((1, t, PAIR), lambda b, h, i: (b, i, h)),
                  pl.BlockSpec((1, PAIR, seq), lambda b, h, i: (b, h, 0)),
                  pl.BlockSpec((1, seq, PAIR), lambda b, h, i: (b, 0, h)),
                  pl.BlockSpec((1, 2, t, t), lambda b, h, i: (h, 0, 0, 0)),
                  pl.BlockSpec(memory_space=pltpu.SMEM),
                  pl.BlockSpec((4, LANES), const),
                  pl.BlockSpec((1, PAIR), const)],
        out_specs=pl.BlockSpec((1, t, PAIR), lambda b, h, i: (b, i, h)),
        out_shape=jax.ShapeDtypeStruct((bsz, seq, d), BF16),
        scratch_shapes=[pltpu.VMEM((nq, PAIR, t), BF16), pltpu.VMEM((max(nq // 2, 1), PAIR, 2 * t), BF16),
                        pltpu.VMEM((2 * t, PAIR), BF16),
                        pltpu.VMEM((2 * t, LANES), F32), pltpu.VMEM((2 * t, LANES), F32),
                        pltpu.VMEM((2 * t, PAIR), F32)],
        compiler_params=_params("parallel", "parallel", "arbitrary"),
        name="prompt_attention",
    )(q.reshape(bsz, seq, d), k_t, v.reshape(bsz, seq, d), tiles, rel_bias, lam_rows, subln)


def _sample_attn_kernel(pt_ref, q_ref, kn_ref, vn_ref, *rest, pages_per_step, past, n_heads, lambda_init):
    k_refs = rest[0:pages_per_step]
    v_refs = rest[pages_per_step:2 * pages_per_step]
    rb_ref, lam_ref, sg_ref, o_ref, qb_ref, bias_ref, m_ref, l_ref, acc_ref = rest[2 * pages_per_step:]
    step_id = pl.program_id(1)
    last = pl.num_programs(1) - 1
    nrow = 2 * n_heads
    d = n_heads * PAIR
    keys = pages_per_step * PAGE_SIZE

    @pl.when(step_id == 0)
    def _():
        bias_ref[...] = jnp.broadcast_to(rb_ref[:, N_BUCKETS - 1:N_BUCKETS] * LOG2E, (nrow, keys))

    @pl.when(step_id == last)
    def _():
        key = lax.broadcasted_iota(jnp.int32, (nrow, keys), 1)
        bucket = _bucket(past - (step_id * keys + key))
        bias = jnp.zeros((nrow, keys), F32)
        for b in range(N_BUCKETS):
            bias = jnp.where(bucket == b, rb_ref[:, b:b + 1], bias)
        bias_ref[...] = bias * LOG2E

    @pl.when(step_id == 0)
    def _():
        q = q_ref[0].astype(F32)
        row = lax.broadcasted_iota(jnp.int32, (nrow, d), 0)
        col = lax.broadcasted_iota(jnp.int32, (nrow, d), 1)
        qblk = jnp.where(jnp.right_shift(col, 6) == row, jnp.broadcast_to(q, (nrow, d)), 0.0)
        qb_ref[...] = qblk.astype(BF16)
        m_ref[...] = jnp.sum(qblk * kn_ref[0], axis=-1, keepdims=True) + rb_ref[:, 0:1] * LOG2E
        l_ref[...] = jnp.ones(l_ref.shape, F32)
        vn = vn_ref[0].astype(F32)
        for h in range(n_heads):
            acc_ref[2 * h:2 * h + 2, :] = jnp.broadcast_to(vn[:, h * PAIR:(h + 1) * PAIR], (2, PAIR))

    qb = qb_ref[...]
    s = jnp.concatenate([jnp.dot(qb, k_refs[g][0].astype(BF16), preferred_element_type=F32)
                         for g in range(pages_per_step)], axis=1) + bias_ref[...]
    m_prev = m_ref[...]
    m_new = jnp.maximum(m_prev, jnp.max(s, axis=-1, keepdims=True))
    alpha = jnp.exp2(m_prev - m_new)
    p = jnp.exp2(s - m_new)
    l_ref[...] = alpha * l_ref[...] + jnp.sum(p, axis=-1, keepdims=True)
    pb = p.astype(BF16)
    row_head = jnp.right_shift(lax.broadcasted_iota(jnp.int32, (nrow, PAIR), 0), 1)
    pv = jnp.zeros((nrow, PAIR), F32)
    for h in range(n_heads):
        v_h = jnp.concatenate([v_refs[g][0, pl.ds(h, PAGE_SIZE, stride=n_heads), :].astype(BF16)
                               for g in range(pages_per_step)], axis=0)
        pv = jnp.where(row_head == h, jnp.dot(pb, v_h, preferred_element_type=F32), pv)
    acc_ref[...] = alpha * acc_ref[...] + pv
    m_ref[...] = m_new

    @pl.when(step_id == last)
    def _():
        rowc = lax.broadcasted_iota(jnp.int32, (nrow, 1), 0)
        lam = _lambda_value(lam_ref, lambda_init)
        coef = jnp.where(jnp.bitwise_and(rowc, 1) == 0, 1.0, -lam) / l_ref[...]
        scaled = acc_ref[...] * coef
        o = scaled + pltpu.roll(scaled, nrow - 1, 0)
        on = _rms_scale(o) * sg_ref[...] * (1.0 - lambda_init)
        for h in range(n_heads):
            o_ref[0, :, h * PAIR:(h + 1) * PAIR] = on[2 * h:2 * h + 1, :]


def _sample_attention(q, k_new, v_new, cache_k, cache_v, page_table, rel_rows, lam_rows, subln, lambda_init):
    bsz, d = q.shape
    n_heads = d // PAIR
    n_pages = page_table.shape[1]
    past = n_pages * PAGE_SIZE
    pages_per_step = max(g for g in (8, 4, 2, 1) if n_pages % g == 0)
    n_pool = cache_k.shape[0]
    kern = functools.partial(_sample_attn_kernel, pages_per_step=pages_per_step, past=past,
                             n_heads=n_heads, lambda_init=lambda_init)
    per_b = lambda b, s, pt: (b, 0, 0)
    const = lambda b, s, pt: (0, 0)

    def k_spec(g):
        return pl.BlockSpec((1, d, PAGE_SIZE), lambda b, s, pt: (pt[b, s * pages_per_step + g], 0, 0))

    def v_spec(g):
        return pl.BlockSpec((1, PAGE_SIZE * n_heads, PAIR),
                            lambda b, s, pt: (pt[b, s * pages_per_step + g], 0, 0))

    grid_spec = pltpu.PrefetchScalarGridSpec(
        num_scalar_prefetch=1,
        grid=(bsz, n_pages // pages_per_step),
        in_specs=([pl.BlockSpec((1, 1, d), per_b)] * 3
                  + [k_spec(g) for g in range(pages_per_step)]
                  + [v_spec(g) for g in range(pages_per_step)]
                  + [pl.BlockSpec((2 * n_heads, N_BUCKETS), const),
                     pl.BlockSpec((4, LANES), const),
                     pl.BlockSpec((1, PAIR), const)]),
        out_specs=pl.BlockSpec((1, 1, d), per_b),
        scratch_shapes=[pltpu.VMEM((2 * n_heads, d), BF16),
                        pltpu.VMEM((2 * n_heads, pages_per_step * PAGE_SIZE), F32),
                        pltpu.VMEM((2 * n_heads, 1), F32), pltpu.VMEM((2 * n_heads, 1), F32),
                        pltpu.VMEM((2 * n_heads, PAIR), F32)],
    )
    ck = jnp.transpose(cache_k, (0, 2, 3, 4, 1)).reshape(n_pool, d, PAGE_SIZE)
    cv = cache_v.reshape(n_pool, PAGE_SIZE * n_heads, PAIR)
    return pl.pallas_call(
        kern,
        grid_spec=grid_spec,
        out_shape=jax.ShapeDtypeStruct((bsz, 1, d), F32),
        compiler_params=_params("parallel", "arbitrary"),
        name="sample_attention",
    )(page_table, q.reshape(bsz, 1, d), k_new.reshape(bsz, 1, d), v_new.reshape(bsz, 1, d),
      *([ck] * pages_per_step), *([cv] * pages_per_step), rel_rows, lam_rows, subln).reshape(bsz, d)


def _pad_lanes(v):
    return jnp.pad(v.astype(F32), (0, LANES - v.shape[0])).reshape(1, LANES)


def kernel(x_prompt, x_sample, state_conv, state_ssm, cache_k, cache_v, page_table, norm_mix, norm_ffn, w_in, conv_w, conv_b, dt_bias, a_log, d_skip, ssm_norm, w_out_ssm, norm_kv, w_kv, k_norm, w_q, q_norm, lambda_q1, lambda_k1, lambda_q2, lambda_k2, subln, w_o, rel_bias, w_gate_up, w_down):
    bsz, seq, d = x_prompt.shape
    dec = x_sample.shape[0]
    assert x_sample.shape[1] == 1
    n_ssm_layers, ssm_heads = dt_bias.shape
    depth = norm_mix.shape[0]
    assert n_ssm_layers == 1 and depth == 2
    d_inner = w_out_ssm.shape[1]
    n_state = state_ssm.shape[-1]
    assert d_inner == ssm_heads * HEAD_DIM and n_state == LANES and ssm_heads <= LANES
    n_heads = d // PAIR

    xp = x_prompt.reshape(bsz * seq, d)
    xs = x_sample.reshape(dec, d)
    row = lambda v: v.astype(F32).reshape(1, -1)

    zx_cols = 2 * d_inner + 2 * SSM_GROUPS * n_state
    w_zx = w_in[0][:, :zx_cols].astype(BF16)
    w_dt = jnp.pad(w_in[0][:, zx_cols:], ((0, 0), (0, LANES - ssm_heads))).astype(BF16)
    g_mix0 = row(norm_mix[0])
    ssd_args = (conv_w[0], row(conv_b[0]), _pad_lanes(dt_bias[0]), _pad_lanes(a_log[0]),
                row(jnp.repeat(d_skip[0], HEAD_DIM)), row(ssm_norm[0]))
    w_out = w_out_ssm[0].astype(BF16)
    w_gu0, w_dn0 = w_gate_up[0].astype(BF16), w_down[0].astype(BF16)

    zx_p, dt_p = _in_proj(xp, g_mix0, w_zx, w_dt)
    yg_p, conv_p, ssm_p = _ssd_prompt(zx_p, dt_p, *ssd_args, bsz, seq, d_inner, n_state)
    xp = _linear_res(yg_p, w_out, xp)
    xp = _ffn(xp, row(norm_ffn[0]), w_gu0, w_dn0)

    zx_s, dt_s = _in_proj(xs, g_mix0, w_zx, w_dt)
    yg_s, conv_s, ssm_s = _ssd_step(zx_s, dt_s, state_conv[0], state_ssm[0], *ssd_args, d_inner, n_state)
    xs = _linear_res(yg_s.reshape(dec, d_inner), w_out, xs)
    xs = _ffn(xs, row(norm_ffn[0]), w_gu0, w_dn0)

    lambda_init = 0.8 - 0.6 * math.exp(-0.3 * 1)
    qkv_args = (row(norm_mix[1]), row(norm_kv), w_q[0], w_kv[:, :d], w_kv[:, d:], q_norm[0], k_norm)
    lam_rows = jnp.pad(jnp.stack([lambda_q1[0], lambda_k1[0], lambda_q2[0], lambda_k2[0]]).astype(F32),
                       ((0, 0), (0, LANES - HEAD_DIM)))
    sg = row(subln[0])
    w_oo = w_o[0].astype(BF16)
    w_gu1, w_dn1 = w_gate_up[1].astype(BF16), w_down[1].astype(BF16)

    q_p, kt_p, v_p, vd_p = _qkv(xp, *qkv_args, seq=seq)
    o_p = _prompt_attention(q_p, kt_p, vd_p, rel_bias.astype(F32), lam_rows, sg, lambda_init, bsz, seq)
    xp = _linear_res(o_p.reshape(bsz * seq, d), w_oo, xp)
    xp = _ffn(xp, row(norm_ffn[1]), w_gu1, w_dn1)

    q_s, k_s, v_s, vd_s = _qkv(xs, *qkv_args)
    rel_rows = jnp.repeat(rel_bias.astype(F32).T, 2, axis=0)
    o_s = _sample_attention(q_s.astype(F32), k_s, vd_s.astype(F32), cache_k, cache_v, page_table,
                            rel_rows, lam_rows, sg, lambda_init)
    xs = _linear_res(o_s, w_oo, xs)
    xs = _ffn(xs, row(norm_ffn[1]), w_gu1, w_dn1)

    k_p = jnp.transpose(kt_p.reshape(bsz, n_heads, 2, HEAD_DIM, seq), (0, 4, 1, 2, 3))
    return (xp.reshape(bsz, seq, d), xs.reshape(dec, 1, d),
            conv_p[None], ssm_p.reshape(1, bsz, ssm_heads, HEAD_DIM, n_state),
            k_p, v_p.reshape(bsz, seq, n_heads, PAIR),
            conv_s[None], ssm_s.reshape(1, dec, ssm_heads, HEAD_DIM, n_state),
            k_s.reshape(dec, 1, n_heads, 2, HEAD_DIM), v_s.reshape(dec, 1, n_heads, PAIR))
```

```python
import functools
import math

import jax
import jax.numpy as jnp
from jax import lax
from jax.experimental import pallas as pl
from jax.experimental.pallas import tpu as pltpu

F32 = jnp.float32
BF16 = jnp.bfloat16

EPS = 1e-6
LANES = 128
HEAD_DIM = 64
PAIR = 2 * HEAD_DIM
SSD_CHUNK = 128
SSM_GROUPS = 4
CONV_TAPS = 4
HALO = 8
N_BUCKETS = 32
MAX_EXACT = N_BUCKETS // 2
MAX_DISTANCE = 128
PAGE_SIZE = 128
LOG2E = math.log2(math.e)
MXU_WIDTH = 256
VMEM_LIMIT = 52 * 1024 * 1024


def _params(*semantics):
    return pltpu.CompilerParams(dimension_semantics=semantics, vmem_limit_bytes=VMEM_LIMIT)


def _row_tile(m, preferred):
    if m <= preferred:
        return m
    t = preferred
    while m % t or t % 16:
        t -= 1
    return t


def _nt_dot(a, b):
    return lax.dot_general(a, b, (((1,), (1,)), ((), ())), preferred_element_type=F32)


def _tn_dot(a, b):
    return lax.dot_general(a, b, (((0,), (0,)), ((), ())), preferred_element_type=F32)


def _rms_scale(x):
    return x * lax.rsqrt(jnp.mean(x * x, axis=-1, keepdims=True) + EPS)


def _log1p(u):
    w = 1.0 + u
    return jnp.where(w == 1.0, u, jnp.log(w) * (u / (w - 1.0)))


def _softplus(x):
    return jnp.maximum(x, 0.0) + _log1p(jnp.exp(-jnp.abs(x)))


def _lane_pair(arr, j, lo):
    return jnp.where(lo, arr[:, 2 * j:2 * j + 1], arr[:, 2 * j + 1:2 * j + 2])


def _bucket(n):
    n = jnp.maximum(n, 0)
    nf = jnp.maximum(n, 1).astype(F32)
    large = MAX_EXACT + (jnp.log(nf / MAX_EXACT) / math.log(MAX_DISTANCE / MAX_EXACT)
                         * (N_BUCKETS - MAX_EXACT)).astype(jnp.int32)
    large = jnp.minimum(large, N_BUCKETS - 1)
    return jnp.where(n < MAX_EXACT, n, large)


def _in_proj_kernel(*refs, tz, tc, tiles_per_seq):
    conv = tiles_per_seq is not None
    if conv:
        (x_ref, g_ref, wz_ref, wc_ref, wdt_ref, cw_ref, cb_ref,
         z_ref, xc_ref, dt_ref, ct_ref, xn_ref, tail_ref) = refs
    else:
        x_ref, g_ref, wz_ref, wc_ref, wdt_ref, z_ref, xc_ref, dt_ref, xn_ref = refs
    i = pl.program_id(0)
    j = pl.program_id(1)
    tm = x_ref.shape[0]
    step = 2 * MXU_WIDTH

    @pl.when(j == 0)
    def _():
        xn_ref[...] = (_rms_scale(x_ref[...]) * g_ref[...]).astype(BF16)
        dt_ref[...] = jnp.dot(xn_ref[...], wdt_ref[...], preferred_element_type=F32)

    xn = xn_ref[...]
    for c0 in range(0, tz, step):
        cols = slice(c0, min(c0 + step, tz))
        z_ref[:, cols] = jnp.dot(xn, wz_ref[:, cols], preferred_element_type=F32)

    if conv:
        seq_start = (i % tiles_per_seq) == 0
        row8 = lax.broadcasted_iota(jnp.int32, (HALO, 1), 0)
    for c0 in range(0, tc, step):
        cols = slice(c0, min(c0 + step, tc))
        raw = jnp.dot(xn, wc_ref[:, cols], preferred_element_type=F32)
        if not conv:
            xc_ref[:, cols] = raw
            continue
        tail = jnp.where(seq_start, 0.0, tail_ref[j, :, cols])
        acc = cb_ref[:, cols] + cw_ref[CONV_TAPS - 1:CONV_TAPS, cols] * raw
        for back in range(1, CONV_TAPS):
            shifted = pltpu.roll(raw, back, 0)
            head = jnp.where(row8 < back, pltpu.roll(tail, back, 0), shifted[0:HALO, :])
            shifted = jnp.concatenate([head, shifted[HALO:, :]], axis=0)
            k = CONV_TAPS - 1 - back
            acc = acc + cw_ref[k:k + 1, cols] * shifted
        tail_ref[j, :, cols] = raw[tm - HALO:tm, :]
        ct_ref[0, :, cols] = raw[tm - (CONV_TAPS - 1):tm, :]
        xc_ref[:, cols] = jax.nn.silu(acc)


def _in_proj(x, g, w_z, w_c, w_dt, conv_w=None, conv_b=None, seq=None):
    m, d = x.shape
    nz, nc = w_z.shape[1], w_c.shape[1]
    conv = seq is not None
    tm = _row_tile(m if seq is None else seq, 512)
    assert nz % (2 * LANES) == 0 and nc % (2 * LANES) == 0
    tz, tc = nz // 2, nc // 2
    rowblk = lambda i, j: (i, j)
    colblk = lambda i, j: (0, j)
    in_specs = [pl.BlockSpec((tm, d), lambda i, j: (i, 0)),
                pl.BlockSpec((1, d), lambda i, j: (0, 0)),
                pl.BlockSpec((d, tz), colblk),
                pl.BlockSpec((d, tc), colblk),
                pl.BlockSpec((d, LANES), lambda i, j: (0, 0))]
    out_specs = [pl.BlockSpec((tm, tz), rowblk), pl.BlockSpec((tm, tc), rowblk),
                 pl.BlockSpec((tm, LANES), lambda i, j: (i, 0))]
    out_shape = [jax.ShapeDtypeStruct((m, nz), F32), jax.ShapeDtypeStruct((m, nc), F32),
                 jax.ShapeDtypeStruct((m, LANES), F32)]
    scratch = [pltpu.VMEM((tm, d), BF16)]
    args = [x, g, w_z, w_c, w_dt]
    tiles_per_seq = None
    if conv:
        tiles_per_seq = seq // tm
        in_specs += [pl.BlockSpec((CONV_TAPS, tc), colblk), pl.BlockSpec((1, tc), colblk)]
        out_specs.append(pl.BlockSpec((1, CONV_TAPS - 1, tc), lambda i, j: (i, 0, j)))
        out_shape.append(jax.ShapeDtypeStruct((m // tm, CONV_TAPS - 1, nc), F32))
        scratch.append(pltpu.VMEM((2, HALO, tc), F32))
        args += [conv_w, conv_b]
    outs = pl.pallas_call(
        functools.partial(_in_proj_kernel, tz=tz, tc=tc, tiles_per_seq=tiles_per_seq),
        grid=(m // tm, 2),
        in_specs=in_specs,
        out_specs=out_specs,
        out_shape=out_shape,
        scratch_shapes=scratch,
        compiler_params=_params("arbitrary", "arbitrary"),
        name="in_proj",
    )(*args)
    outs = list(outs)
    if conv:
        outs[3] = outs[3][tiles_per_seq - 1::tiles_per_seq]
    return outs


def _ssd_kernel(z_ref, xs_ref, bc_ref, dt_ref, dtb_ref, alog_ref, dsk_ref, ng_ref,
                yg_ref, h_ref, ht_ref, *, cl, d_inner, n_state):
    c = pl.program_id(1)
    n_pairs = d_inner // PAIR
    pairs_per_group = n_pairs // SSM_GROUPS
    gb = SSM_GROUPS * n_state

    @pl.when(c == 0)
    def _():
        ht_ref[...] = jnp.zeros_like(ht_ref)

    xc = xs_ref[...]
    bcc = bc_ref[...]

    dt = _softplus(dt_ref[...] + dtb_ref[...])
    dta = dt * (-jnp.exp(alog_ref[...]))
    causal = (lax.broadcasted_iota(jnp.int32, (cl, cl), 0)
              >= lax.broadcasted_iota(jnp.int32, (cl, cl), 1))
    tri = jnp.where(causal, 1.0, 0.0).astype(BF16)
    a_cum = jnp.zeros((cl, LANES), F32)
    rest = dta
    for _ in range(3):
        term = rest.astype(BF16)
        a_cum = a_cum + jnp.dot(tri, term, preferred_element_type=F32)
        rest = rest - term.astype(F32)
    a_cum = a_cum * LOG2E
    a_cum_t = a_cum.T
    dt_t = dt.T
    dte_t = jnp.exp2(a_cum_t[:, cl - 1:cl] - a_cum_t) * dt_t
    cd_row = jnp.exp2(a_cum[cl - 1:cl, :])

    lo = lax.broadcasted_iota(jnp.int32, (1, PAIR), 1) < HEAD_DIM

    for g in range(SSM_GROUPS):
        b_f = bcc[:, g * n_state:(g + 1) * n_state]
        bt_g = b_f.T
        c_g = bcc[:, gb + g * n_state:gb + (g + 1) * n_state].astype(BF16)
        cb_g = _nt_dot(c_g, b_f.astype(BF16))
        gated = []
        ssq = jnp.zeros((cl, 1), F32)
        for jj in range(pairs_per_group):
            j = g * pairs_per_group + jj
            sl = slice(j * PAIR, (j + 1) * PAIR)
            x_p = xc[:, sl]
            x_half = (jnp.where(lo, x_p, 0.0).astype(BF16), jnp.where(lo, 0.0, x_p).astype(BF16))
            y = None
            st = None
            ea = []
            for half in range(2):
                r = 2 * j + half
                col = jnp.broadcast_to(a_cum[:, r:r + 1], (cl, cl))
                decay = jnp.exp2(jnp.where(causal, col - a_cum_t[r:r + 1, :], -jnp.inf))
                w = (cb_g * decay * dt_t[r:r + 1, :]).astype(BF16)
                part = jnp.dot(w, x_half[half], preferred_element_type=F32)
                y = part if y is None else y + part
                part = jnp.dot((bt_g * dte_t[r:r + 1, :]).astype(BF16), x_half[half],
                               preferred_element_type=F32)
                st = part if st is None else st + part
                ea.append(jnp.exp2(col))
            ht_p = ht_ref[:, sl]
            y = y + jnp.dot(c_g, ht_p.astype(BF16), preferred_element_type=F32) * jnp.where(lo, ea[0], ea[1])
            ht_ref[:, sl] = ht_p * _lane_pair(cd_row, j, lo) + st
            y = y + dsk_ref[:, sl] * x_p
            gy = y * jax.nn.silu(z_ref[:, sl])
            ssq = ssq + jnp.sum(gy * gy, axis=-1, keepdims=True)
            gated.append(gy)
        scale = lax.rsqrt(ssq / (pairs_per_group * PAIR) + EPS)
        for jj in range(pairs_per_group):
            sl = slice((g * pairs_per_group + jj) * PAIR, (g * pairs_per_group + jj + 1) * PAIR)
            yg_ref[:, sl] = (gated[jj] * scale * ng_ref[:, sl]).astype(BF16)

    @pl.when(c == pl.num_programs(1) - 1)
    def _():
        for j in range(n_pairs):
            sl = slice(j * PAIR, (j + 1) * PAIR)
            h_ref[0, sl, :] = ht_ref[:, sl].T


def _ssd_prompt(z, xbc, dt_raw, dt_bias, a_log, d_skip, norm_g, bsz, seq, d_inner, n_state):
    cl = SSD_CHUNK
    assert seq % cl == 0
    nc = seq // cl
    gb = SSM_GROUPS * n_state
    assert d_inner % (2 * gb) == 0
    bc_blk = d_inner // (2 * gb)
    kern = functools.partial(_ssd_kernel, cl=cl, d_inner=d_inner, n_state=n_state)
    row = lambda b, c: b * nc + c
    const = lambda b, c: (0, 0)
    return pl.pallas_call(
        kern,
        grid=(bsz, nc),
        in_specs=[pl.BlockSpec((cl, d_inner), lambda b, c: (row(b, c), 0)),
                  pl.BlockSpec((cl, d_inner), lambda b, c: (row(b, c), 0)),
                  pl.BlockSpec((cl, 2 * gb), lambda b, c: (row(b, c), bc_blk)),
                  pl.BlockSpec((cl, LANES), lambda b, c: (row(b, c), 0)),
                  pl.BlockSpec((1, LANES), const),
                  pl.BlockSpec((1, LANES), const),
                  pl.BlockSpec((1, d_inner), const),
                  pl.BlockSpec((1, d_inner), const)],
        out_specs=[pl.BlockSpec((cl, d_inner), lambda b, c: (row(b, c), 0)),
                   pl.BlockSpec((1, d_inner, n_state), lambda b, c: (b, 0, 0))],
        out_shape=[jax.ShapeDtypeStruct((bsz * seq, d_inner), BF16),
                   jax.ShapeDtypeStruct((bsz, d_inner, n_state), F32)],
        scratch_shapes=[pltpu.VMEM((n_state, d_inner), F32)],
        compiler_params=_params("parallel", "arbitrary"),
        name="ssd_prompt",
    )(z, xbc, xbc, dt_raw, dt_bias, a_log, d_skip, norm_g)


def _ssd_step_kernel(zx_ref, dt_ref, cs_ref, h_ref, cw_ref, cb_ref, dtb_ref, alog_ref, dsk_ref, ng_ref,
                     yg_ref, cso_ref, ho_ref, *, d_inner, n_state):
    n_pairs = d_inner // PAIR
    pairs_per_group = n_pairs // SSM_GROUPS
    gb = SSM_GROUPS * n_state
    zx = zx_ref[0]
    z = zx[:, 0:d_inner]
    xbc = zx[:, d_inner:]
    prev = cs_ref[0]
    acc = cb_ref[...] + cw_ref[CONV_TAPS - 1:CONV_TAPS, :] * xbc
    for k in range(CONV_TAPS - 1):
        acc = acc + cw_ref[k:k + 1, :] * prev[k:k + 1, :]
    cso_ref[0, 0:CONV_TAPS - 2, :] = prev[1:CONV_TAPS - 1, :]
    cso_ref[0, CONV_TAPS - 2:CONV_TAPS - 1, :] = xbc
    act = jax.nn.silu(acc)
    xs = act[:, 0:d_inner]
    dt = _softplus(dt_ref[0] + dtb_ref[...])
    da = jnp.exp(dt * (-jnp.exp(alog_ref[...])))

    lo = lax.broadcasted_iota(jnp.int32, (1, PAIR), 1) < HEAD_DIM
    row_lo = lax.broadcasted_iota(jnp.int32, (PAIR, 1), 0) < HEAD_DIM
    eye = (lax.broadcasted_iota(jnp.int32, (PAIR, PAIR), 0)
           == lax.broadcasted_iota(jnp.int32, (PAIR, PAIR), 1))

    for g in range(SSM_GROUPS):
        b_g = act[:, d_inner + g * n_state:d_inner + (g + 1) * n_state]
        c_g = act[:, d_inner + gb + g * n_state:d_inner + gb + (g + 1) * n_state]
        gated = []
        ssq = jnp.zeros((1, 1), F32)
        for jj in range(pairs_per_group):
            j = g * pairs_per_group + jj
            sl = slice(j * PAIR, (j + 1) * PAIR)
            x_p = xs[:, sl]
            xdt = x_p * _lane_pair(dt, j, lo)
            x_col = jnp.sum(jnp.where(eye, jnp.broadcast_to(xdt, (PAIR, PAIR)), 0.0), axis=1, keepdims=True)
            d_col = jnp.where(row_lo, da[:, 2 * j:2 * j + 1], da[:, 2 * j + 1:2 * j + 2])
            h_new = d_col * h_ref[0, sl, :] + x_col * b_g
            ho_ref[0, sl, :] = h_new
            y_col = jnp.sum(h_new * c_g, axis=1, keepdims=True)
            y = jnp.sum(jnp.where(eye, jnp.broadcast_to(y_col, (PAIR, PAIR)), 0.0), axis=0, keepdims=True)
            y = y + dsk_ref[:, sl] * x_p
            gy = y * jax.nn.silu(z[:, sl])
            ssq = ssq + jnp.sum(gy * gy, axis=-1, keepdims=True)
            gated.append(gy)
        scale = lax.rsqrt(ssq / (pairs_per_group * PAIR) + EPS)
        for jj in range(pairs_per_group):
            sl = slice((g * pairs_per_group + jj) * PAIR, (g * pairs_per_group + jj + 1) * PAIR)
            yg_ref[0, :, sl] = gated[jj] * scale * ng_ref[:, sl]


def _ssd_step(zx, dt_raw, state_conv, state_ssm, conv_w, conv_b, dt_bias, a_log, d_skip, norm_g,
              d_inner, n_state):
    bsz = zx.shape[0]
    conv_dim = conv_w.shape[1]
    kern = functools.partial(_ssd_step_kernel, d_inner=d_inner, n_state=n_state)
    const = lambda b: (0, 0)
    per_b = lambda b: (b, 0, 0)
    return pl.pallas_call(
        kern,
        grid=(bsz,),
        in_specs=[pl.BlockSpec((1, 1, d_inner + conv_dim), per_b),
                  pl.BlockSpec((1, 1, LANES), per_b),
                  pl.BlockSpec((1, CONV_TAPS - 1, conv_dim), per_b),
                  pl.BlockSpec((1, d_inner, n_state), per_b),
                  pl.BlockSpec((CONV_TAPS, conv_dim), const),
                  pl.BlockSpec((1, conv_dim), const),
                  pl.BlockSpec((1, LANES), const),
                  pl.BlockSpec((1, LANES), const),
                  pl.BlockSpec((1, d_inner), const),
                  pl.BlockSpec((1, d_inner), const)],
        out_specs=[pl.BlockSpec((1, 1, d_inner), per_b),
                   pl.BlockSpec((1, CONV_TAPS - 1, conv_dim), per_b),
                   pl.BlockSpec((1, d_inner, n_state), per_b)],
        out_shape=[jax.ShapeDtypeStruct((bsz, 1, d_inner), F32),
                   jax.ShapeDtypeStruct((bsz, CONV_TAPS - 1, conv_dim), F32),
                   jax.ShapeDtypeStruct((bsz, d_inner, n_state), F32)],
        compiler_params=_params("parallel"),
        name="ssd_step",
    )(zx.reshape(bsz, 1, -1), dt_raw.reshape(bsz, 1, LANES), state_conv,
      state_ssm.reshape(bsz, d_inner, n_state), conv_w, conv_b, dt_bias, a_log, d_skip, norm_g)


def _linear_res_kernel(a_ref, w_ref, r_ref, o_ref):
    o_ref[...] = r_ref[...] + jnp.dot(a_ref[...].astype(BF16), w_ref[...], preferred_element_type=F32)


def _linear_res(a, w, res):
    m, k = a.shape
    n = w.shape[1]
    tm = _row_tile(m, 512)
    return pl.pallas_call(
        _linear_res_kernel,
        grid=(m // tm,),
        in_specs=[pl.BlockSpec((tm, k), lambda i: (i, 0)),
                  pl.BlockSpec((k, n), lambda i: (0, 0)),
                  pl.BlockSpec((tm, n), lambda i: (i, 0))],
        out_specs=pl.BlockSpec((tm, n), lambda i: (i, 0)),
        out_shape=jax.ShapeDtypeStruct((m, n), F32),
        compiler_params=_params("parallel"),
        name="linear_res",
    )(a, w, res)


def _ffn_kernel(x_ref, g_ref, wg_ref, wu_ref, wd_ref, o_ref, xn_ref, *, tf):
    @pl.when(pl.program_id(1) == 0)
    def _():
        x = x_ref[...]
        xn_ref[...] = (_rms_scale(x) * g_ref[...]).astype(BF16)
        o_ref[...] = x

    xn = xn_ref[...]
    acts = []
    for c0 in range(0, tf, MXU_WIDTH):
        cols = slice(c0, min(c0 + MXU_WIDTH, tf))
        gate = jnp.dot(xn, wg_ref[:, cols], preferred_element_type=F32)
        up = jnp.dot(xn, wu_ref[:, cols], preferred_element_type=F32)
        acts.append((jax.nn.silu(gate) * up).astype(BF16))
    o_ref[...] += jnp.dot(jnp.concatenate(acts, axis=1), wd_ref[...], preferred_element_type=F32)


def _ffn(x, g, w_gate_up, w_down):
    m, d = x.shape
    d_ff = w_down.shape[0]
    tm = _row_tile(m, 1024)
    tf = d_ff // 2 if d_ff % (2 * LANES) == 0 else d_ff
    nf = d_ff // tf
    return pl.pallas_call(
        functools.partial(_ffn_kernel, tf=tf),
        grid=(m // tm, nf),
        in_specs=[pl.BlockSpec((tm, d), lambda i, j: (i, 0)),
                  pl.BlockSpec((1, d), lambda i, j: (0, 0)),
                  pl.BlockSpec((d, tf), lambda i, j: (0, j)),
                  pl.BlockSpec((d, tf), lambda i, j: (0, nf + j)),
                  pl.BlockSpec((tf, d), lambda i, j: (j, 0))],
        out_specs=pl.BlockSpec((tm, d), lambda i, j: (i, 0)),
        out_shape=jax.ShapeDtypeStruct((m, d), F32),
        scratch_shapes=[pltpu.VMEM((tm, d), BF16)],
        compiler_params=_params("parallel", "arbitrary"),
        name="ffn",
    )(x, g, w_gate_up, w_gate_up, w_down)


def _qkv_kernel(x_ref, gq_ref, gkv_ref, w_ref, qn_ref, kn_ref, knc_ref, q_ref, k_ref, v4_ref, vd_ref,
                *, n_heads, k_transposed):
    lo = lax.broadcasted_iota(jnp.int32, (1, PAIR), 1) < HEAD_DIM

    def qk_norm(y, gain_ref, out_ref, post):
        for h in range(n_heads):
            yh = y[:, h * PAIR:(h + 1) * PAIR]
            sq = yh * yh
            s_lo = jnp.sum(jnp.where(lo, sq, 0.0), axis=-1, keepdims=True)
            s_hi = jnp.sum(jnp.where(lo, 0.0, sq), axis=-1, keepdims=True)
            r = jnp.where(lo, lax.rsqrt(s_lo / HEAD_DIM + EPS), lax.rsqrt(s_hi / HEAD_DIM + EPS))
            out_ref[:, h * PAIR:(h + 1) * PAIR] = (yh * r * gain_ref[...] * post).astype(out_ref.dtype)

    xh = _rms_scale(x_ref[...])
    xq = (xh * gq_ref[...]).astype(BF16)
    xkv = (xh * gkv_ref[...]).astype(BF16)
    qk_norm(jnp.dot(xq, w_ref[0], preferred_element_type=F32), qn_ref, q_ref, HEAD_DIM ** -0.5 * LOG2E)

    if k_transposed:
        kt = _nt_dot(w_ref[1], xkv)
        for g in range(2 * n_heads):
            rows = slice(g * HEAD_DIM, (g + 1) * HEAD_DIM)
            blk = kt[rows, :]
            r = lax.rsqrt(jnp.mean(blk * blk, axis=0, keepdims=True) + EPS)
            k_ref[0, rows, :] = blk * r * knc_ref[...]
    else:
        qk_norm(jnp.dot(xkv, w_ref[1], preferred_element_type=F32), kn_ref, k_ref, 1.0)

    v = jnp.dot(xkv, w_ref[2], preferred_element_type=F32)
    vd_ref[...] = v.astype(BF16)
    for h in range(n_heads):
        v4_ref[:, h, :] = v[:, h * PAIR:(h + 1) * PAIR]


def _qkv(x, g_q, g_kv, w_q, w_k, w_v, q_norm, k_norm, seq=None):
    m, d = x.shape
    n_heads = d // PAIR
    k_transposed = seq is not None
    tm = _row_tile(m if seq is None else seq, 512)
    w = jnp.stack([w_q, w_k.T if k_transposed else w_k, w_v]).astype(BF16)
    pair_gain = jnp.tile(k_norm.astype(F32), 2).reshape(1, PAIR)
    kern = functools.partial(_qkv_kernel, n_heads=n_heads, k_transposed=k_transposed)
    const = lambda i: (0, 0)
    rowblk = lambda i: (i, 0)
    if k_transposed:
        per_seq = seq // tm
        k_spec = pl.BlockSpec((1, d, tm), lambda i: (i // per_seq, 0, i % per_seq))
        k_shape = jax.ShapeDtypeStruct((m // seq, d, seq), F32)
    else:
        k_spec = pl.BlockSpec((tm, d), rowblk)
        k_shape = jax.ShapeDtypeStruct((m, d), F32)
    return pl.pallas_call(
        kern,
        grid=(m // tm,),
        in_specs=[pl.BlockSpec((tm, d), rowblk),
                  pl.BlockSpec((1, d), const),
                  pl.BlockSpec((1, d), const),
                  pl.BlockSpec((3, d, d), lambda i: (0, 0, 0)),
                  pl.BlockSpec((1, PAIR), const),
                  pl.BlockSpec((1, PAIR), const),
                  pl.BlockSpec((HEAD_DIM, 1), const)],
        out_specs=[pl.BlockSpec((tm, d), rowblk), k_spec,
                   pl.BlockSpec((tm, n_heads, PAIR), lambda i: (i, 0, 0)),
                   pl.BlockSpec((tm, d), rowblk)],
        out_shape=[jax.ShapeDtypeStruct((m, d), BF16), k_shape,
                   jax.ShapeDtypeStruct((m, n_heads, PAIR), F32),
                   jax.ShapeDtypeStruct((m, d), BF16)],
        compiler_params=_params("parallel"),
        name="qkv_proj",
    )(x, g_q, g_kv, w, jnp.tile(q_norm.astype(F32), 2).reshape(1, PAIR), pair_gain,
      k_norm.astype(F32).reshape(HEAD_DIM, 1))


def _bias_tiles_kernel(rb_ref, o_ref, *, t):
    h = pl.program_id(0)
    i = lax.broadcasted_iota(jnp.int32, (t, t), 0)
    j = lax.broadcasted_iota(jnp.int32, (t, t), 1)
    for off in range(2):
        n = i - j + off * t
        bucket = _bucket(n)
        tile = jnp.zeros((t, t), F32)
        for b in range(N_BUCKETS):
            tile = jnp.where(bucket == b, rb_ref[b, h], tile)
        o_ref[0, off] = jnp.where(n >= 0, (tile - rb_ref[N_BUCKETS - 1, h]) * LOG2E, -jnp.inf)


def _bias_tiles(rel_bias, t):
    n_heads = rel_bias.shape[1]
    return pl.pallas_call(
        functools.partial(_bias_tiles_kernel, t=t),
        grid=(n_heads,),
        in_specs=[pl.BlockSpec(memory_space=pltpu.SMEM)],
        out_specs=pl.BlockSpec((1, 2, t, t), lambda h: (h, 0, 0, 0)),
        out_shape=jax.ShapeDtypeStruct((n_heads, 2, t, t), F32),
        compiler_params=_params("parallel"),
        name="bias_tiles",
    )(rel_bias)


def _lambda_value(lam_ref, lambda_init):
    s1 = jnp.sum(lam_ref[0:1, :] * lam_ref[1:2, :], axis=-1, keepdims=True)
    s2 = jnp.sum(lam_ref[2:3, :] * lam_ref[3:4, :], axis=-1, keepdims=True)
    return jnp.exp(s1) - jnp.exp(s2) + lambda_init


def _attn_kernel(q_ref, k_ref, v_ref, bias_ref, rb_ref, lam_ref, sg_ref, o_ref,
                 kb_ref, kb2_ref, q12_ref, m_ref, l_ref, acc_ref, *, t, rc_far, rc_near, nq, lambda_init):
    h = pl.program_id(1)
    qi = pl.program_id(2)

    @pl.when(qi == 0)
    def _():
        for ki in range(nq):
            kb_ref[ki] = k_ref[0, :, ki * t:(ki + 1) * t].astype(BF16)
        for kp in range(nq // 2):
            kb2_ref[kp] = k_ref[0, :, kp * 2 * t:(kp + 1) * 2 * t].astype(BF16)

    q = q_ref[0]
    lo = lax.broadcasted_iota(jnp.int32, (t, PAIR), 1) < HEAD_DIM
    zero = jnp.zeros_like(q)
    q12_ref[0:t, :] = jnp.where(lo, q, zero)
    q12_ref[t:2 * t, :] = jnp.where(lo, zero, q)

    m_ref[...] = jnp.full(m_ref.shape, -jnp.inf, F32)
    l_ref[...] = jnp.zeros(l_ref.shape, F32)
    acc_ref[...] = jnp.zeros(acc_ref.shape, F32)

    far_bias = rb_ref[N_BUCKETS - 1, h] * LOG2E

    def block(rc, keys_of, base, cols_of, near_of=None):
        for c in range(2 * t // rc):
            rows = slice(c * rc, (c + 1) * rc)
            r0 = (c * rc) % t
            ncols = cols_of(r0)
            s = jnp.dot(q12_ref[rows, :], keys_of(ncols), preferred_element_type=F32)
            tiles = []
            for k in range(ncols // LANES):
                sk = s[:, k * LANES:(k + 1) * LANES]
                near = None if near_of is None else near_of(k)
                if near is not None and near[0] * t + r0 - (near[1] * LANES + LANES - 1) < MAX_DISTANCE:
                    sk = sk + bias_ref[0, near[0], r0:r0 + rc, near[1] * LANES:(near[1] + 1) * LANES]
                tiles.append(sk)
            m_prev = m_ref[rows, :]
            part = tiles[0]
            for sk in tiles[1:]:
                part = jnp.maximum(part, sk)
            m_new = jnp.maximum(m_prev, jnp.max(part, axis=-1, keepdims=True) + far_bias)
            alpha = jnp.exp2(m_prev - m_new)
            m_sub = m_new - far_bias
            ps = [jnp.exp2(sk - m_sub) for sk in tiles]
            psum = ps[0]
            for pk in ps[1:]:
                psum = psum + pk
            l_ref[rows, :] = alpha * l_ref[rows, :] + psum
            pv = jnp.dot(jnp.concatenate(ps, axis=1).astype(BF16), v_ref[0, pl.ds(base, ncols), :],
                         preferred_element_type=F32)
            acc_ref[rows, :] = alpha * acc_ref[rows, :] + pv
            m_ref[rows, :] = m_new

    n_far = jnp.maximum(qi - 1, 0)

    def far_pair(kp, carry):
        block(rc_far, lambda n: kb2_ref[kp, :, 0:n], pl.multiple_of(kp * 2 * t, 2 * t), lambda r0: 2 * t)
        return carry

    lax.fori_loop(0, n_far // 2, far_pair, 0)

    @pl.when(n_far % 2 == 1)
    def _():
        ki = n_far - 1
        block(rc_far, lambda n: kb_ref[ki, :, 0:n], pl.multiple_of(ki * t, t), lambda r0: t)

    nlt = t // LANES

    @pl.when(qi >= 1)
    def _():
        ki = qi - 1
        block(rc_near, lambda n: jnp.concatenate([kb_ref[ki], kb_ref[qi, :, 0:n - t]], axis=1),
              pl.multiple_of(ki * t, t), lambda r0: t + r0 + rc_near,
              near_of=lambda k: (1, k) if k < nlt else (0, k - nlt))

    @pl.when(qi == 0)
    def _():
        block(rc_near, lambda n: kb_ref[0, :, 0:n], 0, lambda r0: r0 + rc_near, near_of=lambda k: (0, k))

    l = jnp.sum(l_ref[...], axis=-1, keepdims=True)
    o1 = acc_ref[0:t, :] / l[0:t, :]
    o2 = acc_ref[t:2 * t, :] / l[t:2 * t, :]
    o = o1 - _lambda_value(lam_ref, lambda_init) * o2
    o_ref[0] = (_rms_scale(o) * sg_ref[...] * (1.0 - lambda_init)).astype(o_ref.dtype)


def _prompt_attention(q, k_t, v, rel_bias, lam_rows, subln, lambda_init, bsz, seq):
    d = q.shape[-1]
    n_heads = d // PAIR
    t = min(seq, 512)
    assert seq % t == 0 and t >= MAX_DISTANCE
    nq = seq // t
    tiles = _bias_tiles(rel_bias, t)
    rc_far, rc_near = t, t
    assert t % rc_near == 0 and rc_near % LANES == 0
    kern = functools.partial(_attn_kernel, t=t, rc_far=rc_far, rc_near=rc_near, nq=nq, lambda_init=lambda_init)
    const = lambda b, h, i: (0, 0)
    return pl.pallas_call(
        kern,
        grid=(bsz, n_heads, nq),
        in_specs=[pl.BlockSpec((1, t, PAIR), lambda b, h, i: (b, i, h)),
                  pl.BlockSpec((1, PAIR, seq), lambda b, h, i: (b, h, 0)),
                  pl.BlockSpec((1, seq, PAIR), lambda b, h, i: (b, 0, h)),
                  pl.BlockSpec((1, 2, t, t), lambda b, h, i: (h, 0, 0, 0)),
                  pl.BlockSpec(memory_space=pltpu.SMEM),
                  pl.BlockSpec((4, LANES), const),
                  pl.BlockSpec((1, PAIR), const)],
        out_specs=pl.BlockSpec((1, t, PAIR), lambda b, h, i: (b, i, h)),
        out_shape=jax.ShapeDtypeStruct((bsz, seq, d), BF16),
        scratch_shapes=[pltpu.VMEM((nq, PAIR, t), BF16), pltpu.VMEM((max(nq // 2, 1), PAIR, 2 * t), BF16),
                        pltpu.VMEM((2 * t, PAIR), BF16),
                        pltpu.VMEM((2 * t, LANES), F32), pltpu.VMEM((2 * t, LANES), F32),
                        pltpu.VMEM((2 * t, PAIR), F32)],
        compiler_params=_params("parallel", "parallel", "arbitrary"),
        name="prompt_attention",
    )(q.reshape(bsz, seq, d), k_t, v.reshape(bsz, seq, d), tiles, rel_bias, lam_rows, subln)


def _sample_attn_kernel(pt_ref, q_ref, kn_ref, vn_ref, *rest, pages_per_step, past, n_heads, lambda_init):
    k_refs = rest[0:pages_per_step]
    v_refs = rest[pages_per_step:2 * pages_per_step]
    rb_ref, lam_ref, sg_ref, o_ref, qb_ref, bias_ref, m_ref, l_ref, acc_ref = rest[2 * pages_per_step:]
    step_id = pl.program_id(1)
    last = pl.num_programs(1) - 1
    nrow = 2 * n_heads
    d = n_heads * PAIR
    keys = pages_per_step * PAGE_SIZE

    @pl.when(step_id == 0)
    def _():
        bias_ref[...] = jnp.broadcast_to(rb_ref[:, N_BUCKETS - 1:N_BUCKETS] * LOG2E, (nrow, keys))

    @pl.when(step_id == last)
    def _():
        key = lax.broadcasted_iota(jnp.int32, (nrow, keys), 1)
        bucket = _bucket(past - (step_id * keys + key))
        bias = jnp.zeros((nrow, keys), F32)
        for b in range(N_BUCKETS):
            bias = jnp.where(bucket == b, rb_ref[:, b:b + 1], bias)
        bias_ref[...] = bias * LOG2E

    @pl.when(step_id == 0)
    def _():
        q = q_ref[0].astype(F32)
        row = lax.broadcasted_iota(jnp.int32, (nrow, d), 0)
        col = lax.broadcasted_iota(jnp.int32, (nrow, d), 1)
        qblk = jnp.where(jnp.right_shift(col, 6) == row, jnp.broadcast_to(q, (nrow, d)), 0.0)
        qb_ref[...] = qblk.astype(BF16)
        m_ref[...] = jnp.sum(qblk * kn_ref[0], axis=-1, keepdims=True) + rb_ref[:, 0:1] * LOG2E
        l_ref[...] = jnp.ones(l_ref.shape, F32)
        vn = vn_ref[0].astype(F32)
        for h in range(n_heads):
            acc_ref[2 * h:2 * h + 2, :] = jnp.broadcast_to(vn[:, h * PAIR:(h + 1) * PAIR], (2, PAIR))

    qb = qb_ref[...]
    s = jnp.concatenate([jnp.dot(qb, k_refs[g][0].astype(BF16), preferred_element_type=F32)
                         for g in range(pages_per_step)], axis=1) + bias_ref[...]
    m_prev = m_ref[...]
    m_new = jnp.maximum(m_prev, jnp.max(s, axis=-1, keepdims=True))
    alpha = jnp.exp2(m_prev - m_new)
    p = jnp.exp2(s - m_new)
    l_ref[...] = alpha * l_ref[...] + jnp.sum(p, axis=-1, keepdims=True)
    pb = p.astype(BF16)
    row_head = jnp.right_shift(lax.broadcasted_iota(jnp.int32, (nrow, PAIR), 0), 1)
    pv = jnp.zeros((nrow, PAIR), F32)
    for h in range(n_heads):
        v_h = jnp.concatenate([v_refs[g][0, pl.ds(h, PAGE_SIZE, stride=n_heads), :].astype(BF16)
                               for g in range(pages_per_step)], axis=0)
        pv = jnp.where(row_head == h, jnp.dot(pb, v_h, preferred_element_type=F32), pv)
    acc_ref[...] = alpha * acc_ref[...] + pv
    m_ref[...] = m_new

    @pl.when(step_id == last)
    def _():
        rowc = lax.broadcasted_iota(jnp.int32, (nrow, 1), 0)
        lam = _lambda_value(lam_ref, lambda_init)
        coef = jnp.where(jnp.bitwise_and(rowc, 1) == 0, 1.0, -lam) / l_ref[...]
        scaled = acc_ref[...] * coef
        o = scaled + pltpu.roll(scaled, nrow - 1, 0)
        on = _rms_scale(o) * sg_ref[...] * (1.0 - lambda_init)
        for h in range(n_heads):
            o_ref[0, :, h * PAIR:(h + 1) * PAIR] = on[2 * h:2 * h + 1, :]


def _sample_attention(q, k_new, v_new, cache_k, cache_v, page_table, rel_rows, lam_rows, subln, lambda_init):
    bsz, d = q.shape
    n_heads = d // PAIR
    n_pages = page_table.shape[1]
    past = n_pages * PAGE_SIZE
    pages_per_step = max(g for g in (8, 4, 2, 1) if n_pages % g == 0)
    n_pool = cache_k.shape[0]
    kern = functools.partial(_sample_attn_kernel, pages_per_step=pages_per_step, past=past,
                             n_heads=n_heads, lambda_init=lambda_init)
    per_b = lambda b, s, pt: (b, 0, 0)
    const = lambda b, s, pt: (0, 0)

    def k_spec(g):
        return pl.BlockSpec((1, d, PAGE_SIZE), lambda b, s, pt: (pt[b, s * pages_per_step + g], 0, 0))

    def v_spec(g):
        return pl.BlockSpec((1, PAGE_SIZE * n_heads, PAIR),
                            lambda b, s, pt: (pt[b, s * pages_per_step + g], 0, 0))

    grid_spec = pltpu.PrefetchScalarGridSpec(
        num_scalar_prefetch=1,
        grid=(bsz, n_pages // pages_per_step),
        in_specs=([pl.BlockSpec((1, 1, d), per_b)] * 3
                  + [k_spec(g) for g in range(pages_per_step)]
                  + [v_spec(g) for g in range(pages_per_step)]
                  + [pl.BlockSpec((2 * n_heads, N_BUCKETS), const),
                     pl.BlockSpec((4, LANES), const),
                     pl.BlockSpec((1, PAIR), const)]),
        out_specs=pl.BlockSpec((1, 1, d), per_b),
        scratch_shapes=[pltpu.VMEM((2 * n_heads, d), BF16),
                        pltpu.VMEM((2 * n_heads, pages_per_step * PAGE_SIZE), F32),
                        pltpu.VMEM((2 * n_heads, 1), F32), pltpu.VMEM((2 * n_heads, 1), F32),
                        pltpu.VMEM((2 * n_heads, PAIR), F32)],
    )
    ck = jnp.transpose(cache_k, (0, 2, 3, 4, 1)).reshape(n_pool, d, PAGE_SIZE)
    cv = cache_v.reshape(n_pool, PAGE_SIZE * n_heads, PAIR)
    return pl.pallas_call(
        kern,
        grid_spec=grid_spec,
        out_shape=jax.ShapeDtypeStruct((bsz, 1, d), F32),
        compiler_params=_params("parallel", "arbitrary"),
        name="sample_attention",
    )(page_table, q.reshape(bsz, 1, d), k_new.reshape(bsz, 1, d), v_new.reshape(bsz, 1, d),
      *([ck] * pages_per_step), *([cv] * pages_per_step), rel_rows, lam_rows, subln).reshape(bsz, d)


def _pad_lanes(v):
    return jnp.pad(v.astype(F32), (0, LANES - v.shape[0])).reshape(1, LANES)


def kernel(x_prompt, x_sample, state_conv, state_ssm, cache_k, cache_v, page_table, norm_mix, norm_ffn, w_in, conv_w, conv_b, dt_bias, a_log, d_skip, ssm_norm, w_out_ssm, norm_kv, w_kv, k_norm, w_q, q_norm, lambda_q1, lambda_k1, lambda_q2, lambda_k2, subln, w_o, rel_bias, w_gate_up, w_down):
    bsz, seq, d = x_prompt.shape
    dec = x_sample.shape[0]
    assert x_sample.shape[1] == 1
    n_ssm_layers, ssm_heads = dt_bias.shape
    depth = norm_mix.shape[0]
    assert n_ssm_layers == 1 and depth == 2
    d_inner = w_out_ssm.shape[1]
    n_state = state_ssm.shape[-1]
    assert d_inner == ssm_heads * HEAD_DIM and n_state == LANES and ssm_heads <= LANES
    n_heads = d // PAIR

    xp = x_prompt.reshape(bsz * seq, d)
    xs = x_sample.reshape(dec, d)
    row = lambda v: v.astype(F32).reshape(1, -1)

    zx_cols = 2 * d_inner + 2 * SSM_GROUPS * n_state
    w_z = w_in[0][:, :d_inner].astype(BF16)
    w_c = w_in[0][:, d_inner:zx_cols].astype(BF16)
    w_dt = jnp.pad(w_in[0][:, zx_cols:], ((0, 0), (0, LANES - ssm_heads))).astype(BF16)
    g_mix0 = row(norm_mix[0])
    cw0, cb0 = conv_w[0], row(conv_b[0])
    ssd_args = (_pad_lanes(dt_bias[0]), _pad_lanes(a_log[0]),
                row(jnp.repeat(d_skip[0], HEAD_DIM)), row(ssm_norm[0]))
    w_out = w_out_ssm[0].astype(BF16)
    w_gu0, w_dn0 = w_gate_up[0].astype(BF16), w_down[0].astype(BF16)

    z_p, xbc_p, dt_p, conv_p = _in_proj(xp, g_mix0, w_z, w_c, w_dt, cw0, cb0, seq=seq)
    yg_p, ssm_p = _ssd_prompt(z_p, xbc_p, dt_p, *ssd_args, bsz, seq, d_inner, n_state)
    xp = _linear_res(yg_p, w_out, xp)
    xp = _ffn(xp, row(norm_ffn[0]), w_gu0, w_dn0)

    z_s, xbc_s, dt_s = _in_proj(xs, g_mix0, w_z, w_c, w_dt)
    zx_s = jnp.concatenate([z_s, xbc_s], axis=1)
    yg_s, conv_s, ssm_s = _ssd_step(zx_s, dt_s, state_conv[0], state_ssm[0], cw0, cb0, *ssd_args,
                                    d_inner, n_state)
    xs = _linear_res(yg_s.reshape(dec, d_inner), w_out, xs)
    xs = _ffn(xs, row(norm_ffn[0]), w_gu0, w_dn0)

    lambda_init = 0.8 - 0.6 * math.exp(-0.3 * 1)
    qkv_args = (row(norm_mix[1]), row(norm_kv), w_q[0], w_kv[:, :d], w_kv[:, d:], q_norm[0], k_norm)
    lam_rows = jnp.pad(jnp.stack([lambda_q1[0], lambda_k1[0], lambda_q2[0], lambda_k2[0]]).astype(F32),
                       ((0, 0), (0, LANES - HEAD_DIM)))
    sg = row(subln[0])
    w_oo = w_o[0].astype(BF16)
    w_gu1, w_dn1 = w_gate_up[1].astype(BF16), w_down[1].astype(BF16)

    q_p, kt_p, v_p, vd_p = _qkv(xp, *qkv_args, seq=seq)
    o_p = _prompt_attention(q_p, kt_p, vd_p, rel_bias.astype(F32), lam_rows, sg, lambda_init, bsz, seq)
    xp = _linear_res(o_p.reshape(bsz * seq, d), w_oo, xp)
    xp = _ffn(xp, row(norm_ffn[1]), w_gu1, w_dn1)

    q_s, k_s, v_s, vd_s = _qkv(xs, *qkv_args)
    rel_rows = jnp.repeat(rel_bias.astype(F32).T, 2, axis=0)
    o_s = _sample_attention(q_s.astype(F32), k_s, vd_s.astype(F32), cache_k, cache_v, page_table,
                            rel_rows, lam_rows, sg, lambda_init)
    xs = _linear_res(o_s, w_oo, xs)
    xs = _ffn(xs, row(norm_ffn[1]), w_gu1, w_dn1)

    k_p = jnp.transpose(kt_p.reshape(bsz, n_heads, 2, HEAD_DIM, seq), (0, 4, 1, 2, 3))
    return (xp.reshape(bsz, seq, d), xs.reshape(dec, 1, d),
            conv_p[None], ssm_p.reshape(1, bsz, ssm_heads, HEAD_DIM, n_state),
            k_p, v_p.reshape(bsz, seq, n_heads, PAIR),
            conv_s[None], ssm_s.reshape(1, dec, ssm_heads, HEAD_DIM, n_state),
            k_s.reshape(dec, 1, n_heads, 2, HEAD_DIM), v_s.reshape(dec, 1, n_heads, PAIR))
```

```python
import functools
import math

import jax
import jax.numpy as jnp
from jax import lax
from jax.experimental import pallas as pl
from jax.experimental.pallas import tpu as pltpu

F32 = jnp.float32
BF16 = jnp.bfloat16

EPS = 1e-6
LANES = 128
HEAD_DIM = 64
PAIR = 2 * HEAD_DIM
SSD_CHUNK = 128
SSM_GROUPS = 4
CONV_TAPS = 4
HALO = 8
N_BUCKETS = 32
MAX_EXACT = N_BUCKETS // 2
MAX_DISTANCE = 128
PAGE_SIZE = 128
LOG2E = math.log2(math.e)
MXU_WIDTH = 256
VMEM_LIMIT = 52 * 1024 * 1024


def _params(*semantics):
    return pltpu.CompilerParams(dimension_semantics=semantics, vmem_limit_bytes=VMEM_LIMIT)


def _row_tile(m, preferred):
    if m <= preferred:
        return m
    t = preferred
    while m % t or t % 16:
        t -= 1
    return t


def _nt_dot(a, b):
    return lax.dot_general(a, b, (((1,), (1,)), ((), ())), preferred_element_type=F32)


def _tn_dot(a, b):
    return lax.dot_general(a, b, (((0,), (0,)), ((), ())), preferred_element_type=F32)


def _rms_scale(x):
    return x * lax.rsqrt(jnp.mean(x * x, axis=-1, keepdims=True) + EPS)


def _silu(x):
    h = 0.5 * x
    return h + h * jnp.tanh(h)


def _log1p(u):
    w = 1.0 + u
    return jnp.where(w == 1.0, u, jnp.log(w) * (u / (w - 1.0)))


def _softplus(x):
    return jnp.maximum(x, 0.0) + _log1p(jnp.exp(-jnp.abs(x)))


def _lane_pair(arr, j, lo):
    return jnp.where(lo, arr[:, 2 * j:2 * j + 1], arr[:, 2 * j + 1:2 * j + 2])


def _bucket(n):
    n = jnp.maximum(n, 0)
    nf = jnp.maximum(n, 1).astype(F32)
    large = MAX_EXACT + (jnp.log(nf / MAX_EXACT) / math.log(MAX_DISTANCE / MAX_EXACT)
                         * (N_BUCKETS - MAX_EXACT)).astype(jnp.int32)
    large = jnp.minimum(large, N_BUCKETS - 1)
    return jnp.where(n < MAX_EXACT, n, large)


def _in_proj_kernel(*refs, tz, tc, tiles_per_seq):
    conv = tiles_per_seq is not None
    if conv:
        (x_ref, g_ref, wz_ref, wc_ref, wdt_ref, cw_ref, cb_ref,
         z_ref, xc_ref, dt_ref, ct_ref, xn_ref, tail_ref) = refs
    else:
        x_ref, g_ref, wz_ref, wc_ref, wdt_ref, z_ref, xc_ref, dt_ref, xn_ref = refs
    i = pl.program_id(0)
    j = pl.program_id(1)
    tm = x_ref.shape[0]
    step = 2 * MXU_WIDTH

    @pl.when(j == 0)
    def _():
        xn_ref[...] = (_rms_scale(x_ref[...]) * g_ref[...]).astype(BF16)
        dt_ref[...] = jnp.dot(xn_ref[...], wdt_ref[...], preferred_element_type=F32)

    xn = xn_ref[...]
    c_slices = [slice(c0, min(c0 + step, tc)) for c0 in range(0, tc, step)]
    z_slices = [slice(c0, min(c0 + step, tz)) for c0 in range(0, tz, step)]
    if not conv:
        for cols in c_slices:
            xc_ref[:, cols] = jnp.dot(xn, wc_ref[:, cols], preferred_element_type=F32)
        for cols in z_slices:
            z_ref[:, cols] = jnp.dot(xn, wz_ref[:, cols], preferred_element_type=F32)
        return

    raws = [jnp.dot(xn, wc_ref[:, cols], preferred_element_type=F32) for cols in c_slices]
    for cols in z_slices:
        z_ref[:, cols] = jnp.dot(xn, wz_ref[:, cols], preferred_element_type=F32)

    seq_start = (i % tiles_per_seq) == 0
    row8 = lax.broadcasted_iota(jnp.int32, (HALO, 1), 0)
    for cols, raw in zip(c_slices, raws):
        tail = jnp.where(seq_start, 0.0, tail_ref[j, :, cols])
        acc = cb_ref[:, cols] + cw_ref[CONV_TAPS - 1:CONV_TAPS, cols] * raw
        for back in range(1, CONV_TAPS):
            shifted = pltpu.roll(raw, back, 0)
            head = jnp.where(row8 < back, pltpu.roll(tail, back, 0), shifted[0:HALO, :])
            shifted = jnp.concatenate([head, shifted[HALO:, :]], axis=0)
            k = CONV_TAPS - 1 - back
            acc = acc + cw_ref[k:k + 1, cols] * shifted
        tail_ref[j, :, cols] = raw[tm - HALO:tm, :]
        ct_ref[0, :, cols] = raw[tm - (CONV_TAPS - 1):tm, :]
        xc_ref[:, cols] = _silu(acc)


def _in_proj(x, g, w_z, w_c, w_dt, conv_w=None, conv_b=None, seq=None):
    m, d = x.shape
    nz, nc = w_z.shape[1], w_c.shape[1]
    conv = seq is not None
    tm = _row_tile(m if seq is None else seq, 1024)
    assert nz % (2 * LANES) == 0 and nc % (2 * LANES) == 0
    tz, tc = nz // 2, nc // 2
    rowblk = lambda i, j: (i, j)
    colblk = lambda i, j: (0, j)
    in_specs = [pl.BlockSpec((tm, d), lambda i, j: (i, 0)),
                pl.BlockSpec((1, d), lambda i, j: (0, 0)),
                pl.BlockSpec((d, tz), colblk),
                pl.BlockSpec((d, tc), colblk),
                pl.BlockSpec((d, LANES), lambda i, j: (0, 0))]
    out_specs = [pl.BlockSpec((tm, tz), rowblk), pl.BlockSpec((tm, tc), rowblk),
                 pl.BlockSpec((tm, LANES), lambda i, j: (i, 0))]
    out_shape = [jax.ShapeDtypeStruct((m, nz), F32), jax.ShapeDtypeStruct((m, nc), F32),
                 jax.ShapeDtypeStruct((m, LANES), F32)]
    scratch = [pltpu.VMEM((tm, d), BF16)]
    args = [x, g, w_z, w_c, w_dt]
    tiles_per_seq = None
    if conv:
        tiles_per_seq = seq // tm
        in_specs += [pl.BlockSpec((CONV_TAPS, tc), colblk), pl.BlockSpec((1, tc), colblk)]
        out_specs.append(pl.BlockSpec((1, CONV_TAPS - 1, tc), lambda i, j: (i, 0, j)))
        out_shape.append(jax.ShapeDtypeStruct((m // tm, CONV_TAPS - 1, nc), F32))
        scratch.append(pltpu.VMEM((2, HALO, tc), F32))
        args += [conv_w, conv_b]
    outs = pl.pallas_call(
        functools.partial(_in_proj_kernel, tz=tz, tc=tc, tiles_per_seq=tiles_per_seq),
        grid=(m // tm, 2),
        in_specs=in_specs,
        out_specs=out_specs,
        out_shape=out_shape,
        scratch_shapes=scratch,
        compiler_params=_params("arbitrary", "arbitrary"),
        name="in_proj",
    )(*args)
    outs = list(outs)
    if conv:
        outs[3] = outs[3][tiles_per_seq - 1::tiles_per_seq]
    return outs


def _ssd_kernel(z_ref, xs_ref, bc_ref, dt_ref, dtb_ref, alog_ref, dsk_ref, ng_ref,
                yg_ref, h_ref, ht_ref, *, cl, d_inner, n_state):
    c = pl.program_id(1)
    n_pairs = d_inner // PAIR
    pairs_per_group = n_pairs // SSM_GROUPS
    gb = SSM_GROUPS * n_state

    @pl.when(c == 0)
    def _():
        ht_ref[...] = jnp.zeros_like(ht_ref)

    xc = xs_ref[...]
    bcc = bc_ref[...]

    dt = _softplus(dt_ref[...] + dtb_ref[...])
    dta = dt * (-jnp.exp(alog_ref[...]))
    causal = (lax.broadcasted_iota(jnp.int32, (cl, cl), 0)
              >= lax.broadcasted_iota(jnp.int32, (cl, cl), 1))
    tri = jnp.where(causal, 1.0, 0.0).astype(BF16)
    a_cum = jnp.zeros((cl, LANES), F32)
    rest = dta
    for _ in range(3):
        term = rest.astype(BF16)
        a_cum = a_cum + jnp.dot(tri, term, preferred_element_type=F32)
        rest = rest - term.astype(F32)
    a_cum = a_cum * LOG2E
    a_cum_t = a_cum.T
    dt_t = dt.T
    dte_t = jnp.exp2(a_cum_t[:, cl - 1:cl] - a_cum_t) * dt_t
    cd_row = jnp.exp2(a_cum[cl - 1:cl, :])

    lo = lax.broadcasted_iota(jnp.int32, (1, PAIR), 1) < HEAD_DIM

    for g in range(SSM_GROUPS):
        b_f = bcc[:, g * n_state:(g + 1) * n_state]
        bt_g = b_f.T
        c_g = bcc[:, gb + g * n_state:gb + (g + 1) * n_state].astype(BF16)
        cb_g = _nt_dot(c_g, b_f.astype(BF16))
        gated = []
        ssq = jnp.zeros((cl, 1), F32)
        for jj in range(pairs_per_group):
            j = g * pairs_per_group + jj
            sl = slice(j * PAIR, (j + 1) * PAIR)
            x_p = xc[:, sl]
            x_half = (jnp.where(lo, x_p, 0.0).astype(BF16), jnp.where(lo, 0.0, x_p).astype(BF16))
            y = None
            st = None
            ea = []
            for half in range(2):
                r = 2 * j + half
                col = jnp.broadcast_to(a_cum[:, r:r + 1], (cl, cl))
                decay = jnp.exp2(jnp.where(causal, col - a_cum_t[r:r + 1, :], -jnp.inf))
                w = (cb_g * decay * dt_t[r:r + 1, :]).astype(BF16)
                part = jnp.dot(w, x_half[half], preferred_element_type=F32)
                y = part if y is None else y + part
                part = jnp.dot((bt_g * dte_t[r:r + 1, :]).astype(BF16), x_half[half],
                               preferred_element_type=F32)
                st = part if st is None else st + part
                ea.append(jnp.exp2(col))
            ht_p = ht_ref[:, sl]
            y = y + jnp.dot(c_g, ht_p.astype(BF16), preferred_element_type=F32) * jnp.where(lo, ea[0], ea[1])
            ht_ref[:, sl] = ht_p * _lane_pair(cd_row, j, lo) + st
            y = y + dsk_ref[:, sl] * x_p
            gy = y * _silu(z_ref[:, sl])
            ssq = ssq + jnp.sum(gy * gy, axis=-1, keepdims=True)
            gated.append(gy)
        scale = lax.rsqrt(ssq / (pairs_per_group * PAIR) + EPS)
        for jj in range(pairs_per_group):
            sl = slice((g * pairs_per_group + jj) * PAIR, (g * pairs_per_group + jj + 1) * PAIR)
            yg_ref[:, sl] = (gated[jj] * scale * ng_ref[:, sl]).astype(BF16)

    @pl.when(c == pl.num_programs(1) - 1)
    def _():
        for j in range(n_pairs):
            sl = slice(j * PAIR, (j + 1) * PAIR)
            h_ref[0, sl, :] = ht_ref[:, sl].T


def _ssd_prompt(z, xbc, dt_raw, dt_bias, a_log, d_skip, norm_g, bsz, seq, d_inner, n_state):
    cl = SSD_CHUNK
    assert seq % cl == 0
    nc = seq // cl
    gb = SSM_GROUPS * n_state
    assert d_inner % (2 * gb) == 0
    bc_blk = d_inner // (2 * gb)
    kern = functools.partial(_ssd_kernel, cl=cl, d_inner=d_inner, n_state=n_state)
    row = lambda b, c: b * nc + c
    const = lambda b, c: (0, 0)
    return pl.pallas_call(
        kern,
        grid=(bsz, nc),
        in_specs=[pl.BlockSpec((cl, d_inner), lambda b, c: (row(b, c), 0)),
                  pl.BlockSpec((cl, d_inner), lambda b, c: (row(b, c), 0)),
                  pl.BlockSpec((cl, 2 * gb), lambda b, c: (row(b, c), bc_blk)),
                  pl.BlockSpec((cl, LANES), lambda b, c: (row(b, c), 0)),
                  pl.BlockSpec((1, LANES), const),
                  pl.BlockSpec((1, LANES), const),
                  pl.BlockSpec((1, d_inner), const),
                  pl.BlockSpec((1, d_inner), const)],
        out_specs=[pl.BlockSpec((cl, d_inner), lambda b, c: (row(b, c), 0)),
                   pl.BlockSpec((1, d_inner, n_state), lambda b, c: (b, 0, 0))],
        out_shape=[jax.ShapeDtypeStruct((bsz * seq, d_inner), BF16),
                   jax.ShapeDtypeStruct((bsz, d_inner, n_state), F32)],
        scratch_shapes=[pltpu.VMEM((n_state, d_inner), F32)],
        compiler_params=_params("parallel", "arbitrary"),
        name="ssd_prompt",
    )(z, xbc, xbc, dt_raw, dt_bias, a_log, d_skip, norm_g)


def _ssd_step_kernel(zx_ref, dt_ref, cs_ref, h_ref, cw_ref, cb_ref, dtb_ref, alog_ref, dsk_ref, ng_ref,
                     yg_ref, cso_ref, ho_ref, *, d_inner, n_state):
    n_pairs = d_inner // PAIR
    pairs_per_group = n_pairs // SSM_GROUPS
    gb = SSM_GROUPS * n_state
    zx = zx_ref[0]
    z = zx[:, 0:d_inner]
    xbc = zx[:, d_inner:]
    prev = cs_ref[0]
    acc = cb_ref[...] + cw_ref[CONV_TAPS - 1:CONV_TAPS, :] * xbc
    for k in range(CONV_TAPS - 1):
        acc = acc + cw_ref[k:k + 1, :] * prev[k:k + 1, :]
    cso_ref[0, 0:CONV_TAPS - 2, :] = prev[1:CONV_TAPS - 1, :]
    cso_ref[0, CONV_TAPS - 2:CONV_TAPS - 1, :] = xbc
    act = jax.nn.silu(acc)
    xs = act[:, 0:d_inner]
    dt = _softplus(dt_ref[0] + dtb_ref[...])
    da = jnp.exp(dt * (-jnp.exp(alog_ref[...])))

    lo = lax.broadcasted_iota(jnp.int32, (1, PAIR), 1) < HEAD_DIM
    row_lo = lax.broadcasted_iota(jnp.int32, (PAIR, 1), 0) < HEAD_DIM
    eye = (lax.broadcasted_iota(jnp.int32, (PAIR, PAIR), 0)
           == lax.broadcasted_iota(jnp.int32, (PAIR, PAIR), 1))

    for g in range(SSM_GROUPS):
        b_g = act[:, d_inner + g * n_state:d_inner + (g + 1) * n_state]
        c_g = act[:, d_inner + gb + g * n_state:d_inner + gb + (g + 1) * n_state]
        gated = []
        ssq = jnp.zeros((1, 1), F32)
        for jj in range(pairs_per_group):
            j = g * pairs_per_group + jj
            sl = slice(j * PAIR, (j + 1) * PAIR)
            x_p = xs[:, sl]
            xdt = x_p * _lane_pair(dt, j, lo)
            x_col = jnp.sum(jnp.where(eye, jnp.broadcast_to(xdt, (PAIR, PAIR)), 0.0), axis=1, keepdims=True)
            d_col = jnp.where(row_lo, da[:, 2 * j:2 * j + 1], da[:, 2 * j + 1:2 * j + 2])
            h_new = d_col * h_ref[0, sl, :] + x_col * b_g
            ho_ref[0, sl, :] = h_new
            y_col = jnp.sum(h_new * c_g, axis=1, keepdims=True)
            y = jnp.sum(jnp.where(eye, jnp.broadcast_to(y_col, (PAIR, PAIR)), 0.0), axis=0, keepdims=True)
            y = y + dsk_ref[:, sl] * x_p
            gy = y * jax.nn.silu(z[:, sl])
            ssq = ssq + jnp.sum(gy * gy, axis=-1, keepdims=True)
            gated.append(gy)
        scale = lax.rsqrt(ssq / (pairs_per_group * PAIR) + EPS)
        for jj in range(pairs_per_group):
            sl = slice((g * pairs_per_group + jj) * PAIR, (g * pairs_per_group + jj + 1) * PAIR)
            yg_ref[0, :, sl] = gated[jj] * scale * ng_ref[:, sl]


def _ssd_step(zx, dt_raw, state_conv, state_ssm, conv_w, conv_b, dt_bias, a_log, d_skip, norm_g,
              d_inner, n_state):
    bsz = zx.shape[0]
    conv_dim = conv_w.shape[1]
    kern = functools.partial(_ssd_step_kernel, d_inner=d_inner, n_state=n_state)
    const = lambda b: (0, 0)
    per_b = lambda b: (b, 0, 0)
    return pl.pallas_call(
        kern,
        grid=(bsz,),
        in_specs=[pl.BlockSpec((1, 1, d_inner + conv_dim), per_b),
                  pl.BlockSpec((1, 1, LANES), per_b),
                  pl.BlockSpec((1, CONV_TAPS - 1, conv_dim), per_b),
                  pl.BlockSpec((1, d_inner, n_state), per_b),
                  pl.BlockSpec((CONV_TAPS, conv_dim), const),
                  pl.BlockSpec((1, conv_dim), const),
                  pl.BlockSpec((1, LANES), const),
                  pl.BlockSpec((1, LANES), const),
                  pl.BlockSpec((1, d_inner), const),
                  pl.BlockSpec((1, d_inner), const)],
        out_specs=[pl.BlockSpec((1, 1, d_inner), per_b),
                   pl.BlockSpec((1, CONV_TAPS - 1, conv_dim), per_b),
                   pl.BlockSpec((1, d_inner, n_state), per_b)],
        out_shape=[jax.ShapeDtypeStruct((bsz, 1, d_inner), F32),
                   jax.ShapeDtypeStruct((bsz, CONV_TAPS - 1, conv_dim), F32),
                   jax.ShapeDtypeStruct((bsz, d_inner, n_state), F32)],
        compiler_params=_params("parallel"),
        name="ssd_step",
    )(zx.reshape(bsz, 1, -1), dt_raw.reshape(bsz, 1, LANES), state_conv,
      state_ssm.reshape(bsz, d_inner, n_state), conv_w, conv_b, dt_bias, a_log, d_skip, norm_g)


def _linear_res_kernel(a_ref, w_ref, r_ref, o_ref):
    o_ref[...] = r_ref[...] + jnp.dot(a_ref[...].astype(BF16), w_ref[...], preferred_element_type=F32)


def _linear_res(a, w, res):
    m, k = a.shape
    n = w.shape[1]
    tm = _row_tile(m, 512)
    return pl.pallas_call(
        _linear_res_kernel,
        grid=(m // tm,),
        in_specs=[pl.BlockSpec((tm, k), lambda i: (i, 0)),
                  pl.BlockSpec((k, n), lambda i: (0, 0)),
                  pl.BlockSpec((tm, n), lambda i: (i, 0))],
        out_specs=pl.BlockSpec((tm, n), lambda i: (i, 0)),
        out_shape=jax.ShapeDtypeStruct((m, n), F32),
        compiler_params=_params("parallel"),
        name="linear_res",
    )(a, w, res)


def _ffn_kernel(x_ref, g_ref, wg_ref, wu_ref, wd_ref, o_ref, xn_ref, *, tf):
    @pl.when(pl.program_id(1) == 0)
    def _():
        x = x_ref[...]
        xn_ref[...] = (_rms_scale(x) * g_ref[...]).astype(BF16)
        o_ref[...] = x

    xn = xn_ref[...]
    acts = []
    for c0 in range(0, tf, MXU_WIDTH):
        cols = slice(c0, min(c0 + MXU_WIDTH, tf))
        gate = jnp.dot(xn, wg_ref[:, cols], preferred_element_type=F32)
        up = jnp.dot(xn, wu_ref[:, cols], preferred_element_type=F32)
        acts.append((jax.nn.silu(gate) * up).astype(BF16))
    o_ref[...] += jnp.dot(jnp.concatenate(acts, axis=1), wd_ref[...], preferred_element_type=F32)


def _ffn(x, g, w_gate_up, w_down):
    m, d = x.shape
    d_ff = w_down.shape[0]
    tm = _row_tile(m, 1024)
    tf = d_ff // 2 if d_ff % (2 * LANES) == 0 else d_ff
    nf = d_ff // tf
    return pl.pallas_call(
        functools.partial(_ffn_kernel, tf=tf),
        grid=(m // tm, nf),
        in_specs=[pl.BlockSpec((tm, d), lambda i, j: (i, 0)),
                  pl.BlockSpec((1, d), lambda i, j: (0, 0)),
                  pl.BlockSpec((d, tf), lambda i, j: (0, j)),
                  pl.BlockSpec((d, tf), lambda i, j: (0, nf + j)),
                  pl.BlockSpec((tf, d), lambda i, j: (j, 0))],
        out_specs=pl.BlockSpec((tm, d), lambda i, j: (i, 0)),
        out_shape=jax.ShapeDtypeStruct((m, d), F32),
        scratch_shapes=[pltpu.VMEM((tm, d), BF16)],
        compiler_params=_params("parallel", "arbitrary"),
        name="ffn",
    )(x, g, w_gate_up, w_gate_up, w_down)


def _qkv_kernel(x_ref, gq_ref, gkv_ref, w_ref, qn_ref, kn_ref, knc_ref, q_ref, k_ref, v4_ref, vd_ref,
                *, n_heads, k_transposed):
    lo = lax.broadcasted_iota(jnp.int32, (1, PAIR), 1) < HEAD_DIM

    def qk_norm(y, gain_ref, out_ref, post):
        for h in range(n_heads):
            yh = y[:, h * PAIR:(h + 1) * PAIR]
            sq = yh * yh
            s_lo = jnp.sum(jnp.where(lo, sq, 0.0), axis=-1, keepdims=True)
            s_hi = jnp.sum(jnp.where(lo, 0.0, sq), axis=-1, keepdims=True)
            r = jnp.where(lo, lax.rsqrt(s_lo / HEAD_DIM + EPS), lax.rsqrt(s_hi / HEAD_DIM + EPS))
            out_ref[:, h * PAIR:(h + 1) * PAIR] = (yh * r * gain_ref[...] * post).astype(out_ref.dtype)

    xh = _rms_scale(x_ref[...])
    xq = (xh * gq_ref[...]).astype(BF16)
    xkv = (xh * gkv_ref[...]).astype(BF16)
    qk_norm(jnp.dot(xq, w_ref[0], preferred_element_type=F32), qn_ref, q_ref, HEAD_DIM ** -0.5 * LOG2E)

    if k_transposed:
        kt = _nt_dot(w_ref[1], xkv)
        for g in range(2 * n_heads):
            rows = slice(g * HEAD_DIM, (g + 1) * HEAD_DIM)
            blk = kt[rows, :]
            r = lax.rsqrt(jnp.mean(blk * blk, axis=0, keepdims=True) + EPS)
            k_ref[0, rows, :] = blk * r * knc_ref[...]
    else:
        qk_norm(jnp.dot(xkv, w_ref[1], preferred_element_type=F32), kn_ref, k_ref, 1.0)

    v = jnp.dot(xkv, w_ref[2], preferred_element_type=F32)
    vd_ref[...] = v.astype(BF16)
    for h in range(n_heads):
        v4_ref[:, h, :] = v[:, h * PAIR:(h + 1) * PAIR]


def _qkv(x, g_q, g_kv, w_q, w_k, w_v, q_norm, k_norm, seq=None):
    m, d = x.shape
    n_heads = d // PAIR
    k_transposed = seq is not None
    tm = _row_tile(m if seq is None else seq, 512)
    w = jnp.stack([w_q, w_k.T if k_transposed else w_k, w_v]).astype(BF16)
    pair_gain = jnp.tile(k_norm.astype(F32), 2).reshape(1, PAIR)
    kern = functools.partial(_qkv_kernel, n_heads=n_heads, k_transposed=k_transposed)
    const = lambda i: (0, 0)
    rowblk = lambda i: (i, 0)
    if k_transposed:
        per_seq = seq // tm
        k_spec = pl.BlockSpec((1, d, tm), lambda i: (i // per_seq, 0, i % per_seq))
        k_shape = jax.ShapeDtypeStruct((m // seq, d, seq), F32)
    else:
        k_spec = pl.BlockSpec((tm, d), rowblk)
        k_shape = jax.ShapeDtypeStruct((m, d), F32)
    return pl.pallas_call(
        kern,
        grid=(m // tm,),
        in_specs=[pl.BlockSpec((tm, d), rowblk),
                  pl.BlockSpec((1, d), const),
                  pl.BlockSpec((1, d), const),
                  pl.BlockSpec((3, d, d), lambda i: (0, 0, 0)),
                  pl.BlockSpec((1, PAIR), const),
                  pl.BlockSpec((1, PAIR), const),
                  pl.BlockSpec((HEAD_DIM, 1), const)],
        out_specs=[pl.BlockSpec((tm, d), rowblk), k_spec,
                   pl.BlockSpec((tm, n_heads, PAIR), lambda i: (i, 0, 0)),
                   pl.BlockSpec((tm, d), rowblk)],
        out_shape=[jax.ShapeDtypeStruct((m, d), BF16), k_shape,
                   jax.ShapeDtypeStruct((m, n_heads, PAIR), F32),
                   jax.ShapeDtypeStruct((m, d), BF16)],
        compiler_params=_params("parallel"),
        name="qkv_proj",
    )(x, g_q, g_kv, w, jnp.tile(q_norm.astype(F32), 2).reshape(1, PAIR), pair_gain,
      k_norm.astype(F32).reshape(HEAD_DIM, 1))


def _bias_tiles_kernel(rb_ref, o_ref, *, t):
    h = pl.program_id(0)
    i = lax.broadcasted_iota(jnp.int32, (t, t), 0)
    j = lax.broadcasted_iota(jnp.int32, (t, t), 1)
    for off in range(2):
        n = i - j + off * t
        bucket = _bucket(n)
        tile = jnp.zeros((t, t), F32)
        for b in range(N_BUCKETS):
            tile = jnp.where(bucket == b, rb_ref[b, h], tile)
        o_ref[0, off] = jnp.where(n >= 0, (tile - rb_ref[N_BUCKETS - 1, h]) * LOG2E, -jnp.inf)


def _bias_tiles(rel_bias, t):
    n_heads = rel_bias.shape[1]
    return pl.pallas_call(
        functools.partial(_bias_tiles_kernel, t=t),
        grid=(n_heads,),
        in_specs=[pl.BlockSpec(memory_space=pltpu.SMEM)],
        out_specs=pl.BlockSpec((1, 2, t, t), lambda h: (h, 0, 0, 0)),
        out_shape=jax.ShapeDtypeStruct((n_heads, 2, t, t), F32),
        compiler_params=_params("parallel"),
        name="bias_tiles",
    )(rel_bias)


def _lambda_value(lam_ref, lambda_init):
    s1 = jnp.sum(lam_ref[0:1, :] * lam_ref[1:2, :], axis=-1, keepdims=True)
    s2 = jnp.sum(lam_ref[2:3, :] * lam_ref[3:4, :], axis=-1, keepdims=True)
    return jnp.exp(s1) - jnp.exp(s2) + lambda_init


def _attn_kernel(q_ref, k_ref, v_ref, bias_ref, rb_ref, lam_ref, sg_ref, o_ref,
                 kb_ref, kb2_ref, q12_ref, m_ref, l_ref, acc_ref, *, t, rc_far, rc_near, nq, lambda_init):
    h = pl.program_id(1)
    qi = pl.program_id(2)

    @pl.when(qi == 0)
    def _():
        for ki in range(nq):
            kb_ref[ki] = k_ref[0, :, ki * t:(ki + 1) * t].astype(BF16)
        for kp in range(nq // 2):
            kb2_ref[kp] = k_ref[0, :, kp * 2 * t:(kp + 1) * 2 * t].astype(BF16)

    q = q_ref[0]
    lo = lax.broadcasted_iota(jnp.int32, (t, PAIR), 1) < HEAD_DIM
    zero = jnp.zeros_like(q)
    q12_ref[0:t, :] = jnp.where(lo, q, zero)
    q12_ref[t:2 * t, :] = jnp.where(lo, zero, q)

    m_ref[...] = jnp.full(m_ref.shape, -jnp.inf, F32)
    l_ref[...] = jnp.zeros(l_ref.shape, F32)
    acc_ref[...] = jnp.zeros(acc_ref.shape, F32)

    far_bias = rb_ref[N_BUCKETS - 1, h] * LOG2E

    def block(rc, keys_of, base, cols_of, near_of=None):
        for c in range(2 * t // rc):
            rows = slice(c * rc, (c + 1) * rc)
            r0 = (c * rc) % t
            ncols = cols_of(r0)
            s = jnp.dot(q12_ref[rows, :], keys_of(ncols), preferred_element_type=F32)
            tiles = []
            for k in range(ncols // LANES):
                sk = s[:, k * LANES:(k + 1) * LANES]
                near = None if near_of is None else near_of(k)
                if near is not None and near[0] * t + r0 - (near[1] * LANES + LANES - 1) < MAX_DISTANCE:
                    sk = sk + bias_ref[0, near[0], r0:r0 + rc, near[1] * LANES:(near[1] + 1) * LANES]
                tiles.append(sk)
            m_prev = m_ref[rows, :]
            part = tiles[0]
            for sk in tiles[1:]:
                part = jnp.maximum(part, sk)
            m_new = jnp.maximum(m_prev, jnp.max(part, axis=-1, keepdims=True) + far_bias)
            alpha = jnp.exp2(m_prev - m_new)
            m_sub = m_new - far_bias
            ps = [jnp.exp2(sk - m_sub) for sk in tiles]
            psum = ps[0]
            for pk in ps[1:]:
                psum = psum + pk
            l_ref[rows, :] = alpha * l_ref[rows, :] + psum
            pv = jnp.dot(jnp.concatenate(ps, axis=1).astype(BF16), v_ref[0, pl.ds(base, ncols), :],
                         preferred_element_type=F32)
            acc_ref[rows, :] = alpha * acc_ref[rows, :] + pv
            m_ref[rows, :] = m_new

    n_far = jnp.maximum(qi - 1, 0)

    def far_pair(kp, carry):
        block(rc_far, lambda n: kb2_ref[kp, :, 0:n], pl.multiple_of(kp * 2 * t, 2 * t), lambda r0: 2 * t)
        return carry

    lax.fori_loop(0, n_far // 2, far_pair, 0)

    @pl.when(n_far % 2 == 1)
    def _():
        ki = n_far - 1
        block(rc_far, lambda n: kb_ref[ki, :, 0:n], pl.multiple_of(ki * t, t), lambda r0: t)

    nlt = t // LANES

    @pl.when(qi >= 1)
    def _():
        ki = qi - 1
        block(rc_near, lambda n: jnp.concatenate([kb_ref[ki], kb_ref[qi, :, 0:n - t]], axis=1),
              pl.multiple_of(ki * t, t), lambda r0: t + r0 + rc_near,
              near_of=lambda k: (1, k) if k < nlt else (0, k - nlt))

    @pl.when(qi == 0)
    def _():
        block(rc_near, lambda n: kb_ref[0, :, 0:n], 0, lambda r0: r0 + rc_near, near_of=lambda k: (0, k))

    l = jnp.sum(l_ref[...], axis=-1, keepdims=True)
    o1 = acc_ref[0:t, :] / l[0:t, :]
    o2 = acc_ref[t:2 * t, :] / l[t:2 * t, :]
    o = o1 - _lambda_value(lam_ref, lambda_init) * o2
    o_ref[0] = (_rms_scale(o) * sg_ref[...] * (1.0 - lambda_init)).astype(o_ref.dtype)


def _prompt_attention(q, k_t, v, rel_bias, lam_rows, subln, lambda_init, bsz, seq):
    d = q.shape[-1]
    n_heads = d // PAIR
    t = min(seq, 512)
    assert seq % t == 0 and t >= MAX_DISTANCE
    nq = seq // t
    tiles = _bias_tiles(rel_bias, t)
    rc_far, rc_near = t, t
    assert t % rc_near == 0 and rc_near % LANES == 0
    kern = functools.partial(_attn_kernel, t=t, rc_far=rc_far, rc_near=rc_near, nq=nq, lambda_init=lambda_init)
    const = lambda b, h, i: (0, 0)
    return pl.pallas_call(
        kern,
        grid=(bsz, n_heads, nq),
        in_specs=[pl.BlockSpec((1, t, PAIR), lambda b, h, i: (b, i, h)),
                  pl.BlockSpec((1, PAIR, seq), lambda b, h, i: (b, h, 0)),
                  pl.BlockSpec((1, seq, PAIR), lambda b, h, i: (b, 0, h)),
                  pl.BlockSpec((1, 2, t, t), lambda b, h, i: (h, 0, 0, 0)),
                  pl.BlockSpec(memory_space=pltpu.SMEM),
                  pl.BlockSpec((4, LANES), const),
                  pl.BlockSpec((1, PAIR), const)],
        out_specs=pl.BlockSpec((1, t, PAIR), lambda b, h, i: (b, i, h)),
        out_shape=jax.ShapeDtypeStruct((bsz, seq, d), BF16),
        scratch_shapes=[pltpu.VMEM((nq, PAIR, t), BF16), pltpu.VMEM((max(nq // 2, 1), PAIR, 2 * t), BF16),
                        pltpu.VMEM((2 * t, PAIR), BF16),
                        pltpu.VMEM((2 * t, LANES), F32), pltpu.VMEM((2 * t, LANES), F32),
                        pltpu.VMEM((2 * t, PAIR), F32)],
        compiler_params=_params("parallel", "parallel", "arbitrary"),
        name="prompt_attention",
    )(q.reshape(bsz, seq, d), k_t, v.reshape(bsz, seq, d), tiles, rel_bias, lam_rows, subln)


def _sample_attn_kernel(pt_ref, q_ref, kn_ref, vn_ref, *rest, pages_per_step, past, n_heads, lambda_init):
    k_refs = rest[0:pages_per_step]
    v_refs = rest[pages_per_step:2 * pages_per_step]
    rb_ref, lam_ref, sg_ref, o_ref, qb_ref, bias_ref, m_ref, l_ref, acc_ref = rest[2 * pages_per_step:]
    step_id = pl.program_id(1)
    last = pl.num_programs(1) - 1
    nrow = 2 * n_heads
    d = n_heads * PAIR
    keys = pages_per_step * PAGE_SIZE

    @pl.when(step_id == 0)
    def _():
        bias_ref[...] = jnp.broadcast_to(rb_ref[:, N_BUCKETS - 1:N_BUCKETS] * LOG2E, (nrow, keys))

    @pl.when(step_id == last)
    def _():
        key = lax.broadcasted_iota(jnp.int32, (nrow, keys), 1)
        bucket = _bucket(past - (step_id * keys + key))
        bias = jnp.zeros((nrow, keys), F32)
        for b in range(N_BUCKETS):
            bias = jnp.where(bucket == b, rb_ref[:, b:b + 1], bias)
        bias_ref[...] = bias * LOG2E

    @pl.when(step_id == 0)
    def _():
        q = q_ref[0].astype(F32)
        row = lax.broadcasted_iota(jnp.int32, (nrow, d), 0)
        col = lax.broadcasted_iota(jnp.int32, (nrow, d), 1)
        qblk = jnp.where(jnp.right_shift(col, 6) == row, jnp.broadcast_to(q, (nrow, d)), 0.0)
        qb_ref[...] = qblk.astype(BF16)
        m_ref[...] = jnp.sum(qblk * kn_ref[0], axis=-1, keepdims=True) + rb_ref[:, 0:1] * LOG2E
        l_ref[...] = jnp.ones(l_ref.shape, F32)
        vn = vn_ref[0].astype(F32)
        for h in range(n_heads):
            acc_ref[2 * h:2 * h + 2, :] = jnp.broadcast_to(vn[:, h * PAIR:(h + 1) * PAIR], (2, PAIR))

    qb = qb_ref[...]
    s = jnp.concatenate([jnp.dot(qb, k_refs[g][0].astype(BF16), preferred_element_type=F32)
                         for g in range(pages_per_step)], axis=1) + bias_ref[...]
    m_prev = m_ref[...]
    m_new = jnp.maximum(m_prev, jnp.max(s, axis=-1, keepdims=True))
    alpha = jnp.exp2(m_prev - m_new)
    p = jnp.exp2(s - m_new)
    l_ref[...] = alpha * l_ref[...] + jnp.sum(p, axis=-1, keepdims=True)
    pb = p.astype(BF16)
    row_head = jnp.right_shift(lax.broadcasted_iota(jnp.int32, (nrow, PAIR), 0), 1)
    pv = jnp.zeros((nrow, PAIR), F32)
    for h in range(n_heads):
        v_h = jnp.concatenate([v_refs[g][0, pl.ds(h, PAGE_SIZE, stride=n_heads), :].astype(BF16)
                               for g in range(pages_per_step)], axis=0)
        pv = jnp.where(row_head == h, jnp.dot(pb, v_h, preferred_element_type=F32), pv)
    acc_ref[...] = alpha * acc_ref[...] + pv
    m_ref[...] = m_new

    @pl.when(step_id == last)
    def _():
        rowc = lax.broadcasted_iota(jnp.int32, (nrow, 1), 0)
        lam = _lambda_value(lam_ref, lambda_init)
        coef = jnp.where(jnp.bitwise_and(rowc, 1) == 0, 1.0, -lam) / l_ref[...]
        scaled = acc_ref[...] * coef
        o = scaled + pltpu.roll(scaled, nrow - 1, 0)
        on = _rms_scale(o) * sg_ref[...] * (1.0 - lambda_init)
        for h in range(n_heads):
            o_ref[0, :, h * PAIR:(h + 1) * PAIR] = on[2 * h:2 * h + 1, :]


def _sample_attention(q, k_new, v_new, cache_k, cache_v, page_table, rel_rows, lam_rows, subln, lambda_init):
    bsz, d = q.shape
    n_heads = d // PAIR
    n_pages = page_table.shape[1]
    past = n_pages * PAGE_SIZE
    pages_per_step = max(g for g in (16, 8, 4, 2, 1) if n_pages % g == 0)
    n_pool = cache_k.shape[0]
    kern = functools.partial(_sample_attn_kernel, pages_per_step=pages_per_step, past=past,
                             n_heads=n_heads, lambda_init=lambda_init)
    per_b = lambda b, s, pt: (b, 0, 0)
    const = lambda b, s, pt: (0, 0)

    def k_spec(g):
        return pl.BlockSpec((1, d, PAGE_SIZE), lambda b, s, pt: (pt[b, s * pages_per_step + g], 0, 0))

    def v_spec(g):
        return pl.BlockSpec((1, PAGE_SIZE * n_heads, PAIR),
                            lambda b, s, pt: (pt[b, s * pages_per_step + g], 0, 0))

    grid_spec = pltpu.PrefetchScalarGridSpec(
        num_scalar_prefetch=1,
        grid=(bsz, n_pages // pages_per_step),
        in_specs=([pl.BlockSpec((1, 1, d), per_b)] * 3
                  + [k_spec(g) for g in range(pages_per_step)]
                  + [v_spec(g) for g in range(pages_per_step)]
                  + [pl.BlockSpec((2 * n_heads, N_BUCKETS), const),
                     pl.BlockSpec((4, LANES), const),
                     pl.BlockSpec((1, PAIR), const)]),
        out_specs=pl.BlockSpec((1, 1, d), per_b),
        scratch_shapes=[pltpu.VMEM((2 * n_heads, d), BF16),
                        pltpu.VMEM((2 * n_heads, pages_per_step * PAGE_SIZE), F32),
                        pltpu.VMEM((2 * n_heads, 1), F32), pltpu.VMEM((2 * n_heads, 1), F32),
                        pltpu.VMEM((2 * n_heads, PAIR), F32)],
    )
    ck = jnp.transpose(cache_k, (0, 2, 3, 4, 1)).reshape(n_pool, d, PAGE_SIZE)
    cv = cache_v.reshape(n_pool, PAGE_SIZE * n_heads, PAIR)
    return pl.pallas_call(
        kern,
        grid_spec=grid_spec,
        out_shape=jax.ShapeDtypeStruct((bsz, 1, d), F32),
        compiler_params=_params("parallel", "arbitrary"),
        name="sample_attention",
    )(page_table, q.reshape(bsz, 1, d), k_new.reshape(bsz, 1, d), v_new.reshape(bsz, 1, d),
      *([ck] * pages_per_step), *([cv] * pages_per_step), rel_rows, lam_rows, subln).reshape(bsz, d)


def _pad_lanes(v):
    return jnp.pad(v.astype(F32), (0, LANES - v.shape[0])).reshape(1, LANES)


def kernel(x_prompt, x_sample, state_conv, state_ssm, cache_k, cache_v, page_table, norm_mix, norm_ffn, w_in, conv_w, conv_b, dt_bias, a_log, d_skip, ssm_norm, w_out_ssm, norm_kv, w_kv, k_norm, w_q, q_norm, lambda_q1, lambda_k1, lambda_q2, lambda_k2, subln, w_o, rel_bias, w_gate_up, w_down):
    bsz, seq, d = x_prompt.shape
    dec = x_sample.shape[0]
    assert x_sample.shape[1] == 1
    n_ssm_layers, ssm_heads = dt_bias.shape
    depth = norm_mix.shape[0]
    assert n_ssm_layers == 1 and depth == 2
    d_inner = w_out_ssm.shape[1]
    n_state = state_ssm.shape[-1]
    assert d_inner == ssm_heads * HEAD_DIM and n_state == LANES and ssm_heads <= LANES
    n_heads = d // PAIR

    xp = x_prompt.reshape(bsz * seq, d)
    xs = x_sample.reshape(dec, d)
    row = lambda v: v.astype(F32).reshape(1, -1)

    zx_cols = 2 * d_inner + 2 * SSM_GROUPS * n_state
    w_z = w_in[0][:, :d_inner].astype(BF16)
    w_c = w_in[0][:, d_inner:zx_cols].astype(BF16)
    w_dt = jnp.pad(w_in[0][:, zx_cols:], ((0, 0), (0, LANES - ssm_heads))).astype(BF16)
    g_mix0 = row(norm_mix[0])
    cw0, cb0 = conv_w[0], row(conv_b[0])
    ssd_args = (_pad_lanes(dt_bias[0]), _pad_lanes(a_log[0]),
                row(jnp.repeat(d_skip[0], HEAD_DIM)), row(ssm_norm[0]))
    w_out = w_out_ssm[0].astype(BF16)
    w_gu0, w_dn0 = w_gate_up[0].astype(BF16), w_down[0].astype(BF16)

    z_p, xbc_p, dt_p, conv_p = _in_proj(xp, g_mix0, w_z, w_c, w_dt, cw0, cb0, seq=seq)
    yg_p, ssm_p = _ssd_prompt(z_p, xbc_p, dt_p, *ssd_args, bsz, seq, d_inner, n_state)
    xp = _linear_res(yg_p, w_out, xp)
    xp = _ffn(xp, row(norm_ffn[0]), w_gu0, w_dn0)

    z_s, xbc_s, dt_s = _in_proj(xs, g_mix0, w_z, w_c, w_dt)
    zx_s = jnp.concatenate([z_s, xbc_s], axis=1)
    yg_s, conv_s, ssm_s = _ssd_step(zx_s, dt_s, state_conv[0], state_ssm[0], cw0, cb0, *ssd_args,
                                    d_inner, n_state)
    xs = _linear_res(yg_s.reshape(dec, d_inner), w_out, xs)
    xs = _ffn(xs, row(norm_ffn[0]), w_gu0, w_dn0)

    lambda_init = 0.8 - 0.6 * math.exp(-0.3 * 1)
    qkv_args = (row(norm_mix[1]), row(norm_kv), w_q[0], w_kv[:, :d], w_kv[:, d:], q_norm[0], k_norm)
    lam_rows = jnp.pad(jnp.stack([lambda_q1[0], lambda_k1[0], lambda_q2[0], lambda_k2[0]]).astype(F32),
                       ((0, 0), (0, LANES - HEAD_DIM)))
    sg = row(subln[0])
    w_oo = w_o[0].astype(BF16)
    w_gu1, w_dn1 = w_gate_up[1].astype(BF16), w_down[1].astype(BF16)

    q_p, kt_p, v_p, vd_p = _qkv(xp, *qkv_args, seq=seq)
    o_p = _prompt_attention(q_p, kt_p, vd_p, rel_bias.astype(F32), lam_rows, sg, lambda_init, bsz, seq)
    xp = _linear_res(o_p.reshape(bsz * seq, d), w_oo, xp)
    xp = _ffn(xp, row(norm_ffn[1]), w_gu1, w_dn1)

    q_s, k_s, v_s, vd_s = _qkv(xs, *qkv_args)
    rel_rows = jnp.repeat(rel_bias.astype(F32).T, 2, axis=0)
    o_s = _sample_attention(q_s.astype(F32), k_s, vd_s.astype(F32), cache_k, cache_v, page_table,
                            rel_rows, lam_rows, sg, lambda_init)
    xs = _linear_res(o_s, w_oo, xs)
    xs = _ffn(xs, row(norm_ffn[1]), w_gu1, w_dn1)

    k_p = jnp.transpose(kt_p.reshape(bsz, n_heads, 2, HEAD_DIM, seq), (0, 4, 1, 2, 3))
    return (xp.reshape(bsz, seq, d), xs.reshape(dec, 1, d),
            conv_p[None], ssm_p.reshape(1, bsz, ssm_heads, HEAD_DIM, n_state),
            k_p, v_p.reshape(bsz, seq, n_heads, PAIR),
            conv_s[None], ssm_s.reshape(1, dec, ssm_heads, HEAD_DIM, n_state),
            k_s.reshape(dec, 1, n_heads, 2, HEAD_DIM), v_s.reshape(dec, 1, n_heads, PAIR))
```

```python
import functools
import math

import jax
import jax.numpy as jnp
from jax import lax
from jax.experimental import pallas as pl
from jax.experimental.pallas import tpu as pltpu

F32 = jnp.float32
BF16 = jnp.bfloat16

EPS = 1e-6
LANES = 128
HEAD_DIM = 64
PAIR = 2 * HEAD_DIM
SSD_CHUNK = 128
SSM_GROUPS = 4
CONV_TAPS = 4
HALO = 8
N_BUCKETS = 32
MAX_EXACT = N_BUCKETS // 2
MAX_DISTANCE = 128
PAGE_SIZE = 128
LOG2E = math.log2(math.e)
MXU_WIDTH = 256
VMEM_LIMIT = 52 * 1024 * 1024


def _params(*semantics):
    return pltpu.CompilerParams(dimension_semantics=semantics, vmem_limit_bytes=VMEM_LIMIT)


def _row_tile(m, preferred):
    if m <= preferred:
        return m
    t = preferred
    while m % t or t % 16:
        t -= 1
    return t


def _nt_dot(a, b):
    return lax.dot_general(a, b, (((1,), (1,)), ((), ())), preferred_element_type=F32)


def _tn_dot(a, b):
    return lax.dot_general(a, b, (((0,), (0,)), ((), ())), preferred_element_type=F32)


def _rms_scale(x):
    return x * lax.rsqrt(jnp.mean(x * x, axis=-1, keepdims=True) + EPS)


def _split2(x):
    hi = x.astype(BF16).astype(F32)
    lo = (x - hi).astype(BF16).astype(F32)
    return hi, lo


def _silu(x):
    h = 0.5 * x
    return h + h * jnp.tanh(h)


def _log1p(u):
    w = 1.0 + u
    return jnp.where(w == 1.0, u, jnp.log(w) * (u / (w - 1.0)))


def _softplus(x):
    return jnp.maximum(x, 0.0) + _log1p(jnp.exp(-jnp.abs(x)))


def _lane_pair(arr, j, lo):
    return jnp.where(lo, arr[:, 2 * j:2 * j + 1], arr[:, 2 * j + 1:2 * j + 2])


def _bucket(n):
    n = jnp.maximum(n, 0)
    nf = jnp.maximum(n, 1).astype(F32)
    large = MAX_EXACT + (jnp.log(nf / MAX_EXACT) / math.log(MAX_DISTANCE / MAX_EXACT)
                         * (N_BUCKETS - MAX_EXACT)).astype(jnp.int32)
    large = jnp.minimum(large, N_BUCKETS - 1)
    return jnp.where(n < MAX_EXACT, n, large)


def _in_proj_kernel(*refs, tz, tc, tiles_per_seq):
    conv = tiles_per_seq is not None
    if conv:
        (x_ref, g_ref, wz_ref, wc_ref, wdt_ref, cw_ref, cb_ref,
         z_ref, xc_ref, dt_ref, ct_ref, xn_ref, tail_ref) = refs
    else:
        x_ref, g_ref, wz_ref, wc_ref, wdt_ref, z_ref, xc_ref, dt_ref, xn_ref = refs
    i = pl.program_id(0)
    j = pl.program_id(1)
    tm = x_ref.shape[0]
    step = 2 * MXU_WIDTH

    @pl.when(j == 0)
    def _():
        xn_ref[...] = (_rms_scale(x_ref[...]) * g_ref[...]).astype(BF16)
        dt_ref[...] = jnp.dot(xn_ref[...], wdt_ref[...], preferred_element_type=F32)

    xn = xn_ref[...]
    c_slices = [slice(c0, min(c0 + step, tc)) for c0 in range(0, tc, step)]
    z_slices = [slice(c0, min(c0 + step, tz)) for c0 in range(0, tz, step)]
    if not conv:
        for cols in c_slices:
            xc_ref[:, cols] = jnp.dot(xn, wc_ref[:, cols], preferred_element_type=F32)
        for cols in z_slices:
            z_ref[:, cols] = jnp.dot(xn, wz_ref[:, cols], preferred_element_type=F32)
        return

    raws = [jnp.dot(xn, wc_ref[:, cols], preferred_element_type=F32) for cols in c_slices]
    for cols in z_slices:
        z_ref[:, cols] = jnp.dot(xn, wz_ref[:, cols], preferred_element_type=F32)

    seq_start = (i % tiles_per_seq) == 0
    row8 = lax.broadcasted_iota(jnp.int32, (HALO, 1), 0)
    for cols, raw in zip(c_slices, raws):
        tail = jnp.where(seq_start, 0.0, tail_ref[j, :, cols])
        acc = cb_ref[:, cols] + cw_ref[CONV_TAPS - 1:CONV_TAPS, cols] * raw
        for back in range(1, CONV_TAPS):
            shifted = pltpu.roll(raw, back, 0)
            head = jnp.where(row8 < back, pltpu.roll(tail, back, 0), shifted[0:HALO, :])
            shifted = jnp.concatenate([head, shifted[HALO:, :]], axis=0)
            k = CONV_TAPS - 1 - back
            acc = acc + cw_ref[k:k + 1, cols] * shifted
        tail_ref[j, :, cols] = raw[tm - HALO:tm, :]
        ct_ref[0, :, cols] = raw[tm - (CONV_TAPS - 1):tm, :]
        xc_ref[:, cols] = _silu(acc)


def _in_proj(x, g, w_z, w_c, w_dt, conv_w=None, conv_b=None, seq=None):
    m, d = x.shape
    nz, nc = w_z.shape[1], w_c.shape[1]
    conv = seq is not None
    tm = _row_tile(m if seq is None else seq, 1024)
    assert nz % (2 * LANES) == 0 and nc % (2 * LANES) == 0
    tz, tc = nz // 2, nc // 2
    rowblk = lambda i, j: (i, j)
    colblk = lambda i, j: (0, j)
    in_specs = [pl.BlockSpec((tm, d), lambda i, j: (i, 0)),
                pl.BlockSpec((1, d), lambda i, j: (0, 0)),
                pl.BlockSpec((d, tz), colblk),
                pl.BlockSpec((d, tc), colblk),
                pl.BlockSpec((d, LANES), lambda i, j: (0, 0))]
    out_specs = [pl.BlockSpec((tm, tz), rowblk), pl.BlockSpec((tm, tc), rowblk),
                 pl.BlockSpec((tm, LANES), lambda i, j: (i, 0))]
    out_shape = [jax.ShapeDtypeStruct((m, nz), F32), jax.ShapeDtypeStruct((m, nc), F32),
                 jax.ShapeDtypeStruct((m, LANES), F32)]
    scratch = [pltpu.VMEM((tm, d), BF16)]
    args = [x, g, w_z, w_c, w_dt]
    tiles_per_seq = None
    if conv:
        tiles_per_seq = seq // tm
        in_specs += [pl.BlockSpec((CONV_TAPS, tc), colblk), pl.BlockSpec((1, tc), colblk)]
        out_specs.append(pl.BlockSpec((1, CONV_TAPS - 1, tc), lambda i, j: (i, 0, j)))
        out_shape.append(jax.ShapeDtypeStruct((m // tm, CONV_TAPS - 1, nc), F32))
        scratch.append(pltpu.VMEM((2, HALO, tc), F32))
        args += [conv_w, conv_b]
    outs = pl.pallas_call(
        functools.partial(_in_proj_kernel, tz=tz, tc=tc, tiles_per_seq=tiles_per_seq),
        grid=(m // tm, 2),
        in_specs=in_specs,
        out_specs=out_specs,
        out_shape=out_shape,
        scratch_shapes=scratch,
        compiler_params=_params("arbitrary", "arbitrary"),
        name="in_proj",
    )(*args)
    outs = list(outs)
    if conv:
        outs[3] = outs[3][tiles_per_seq - 1::tiles_per_seq]
    return outs


def _ssd_kernel(z_ref, xs_ref, bc_ref, dt_ref, dtb_ref, alog_ref, dsk_ref, ng_ref,
                yg_ref, h_ref, ht_ref, *, cl, d_inner, n_state):
    c = pl.program_id(1)
    n_pairs = d_inner // PAIR
    pairs_per_group = n_pairs // SSM_GROUPS
    gb = SSM_GROUPS * n_state

    @pl.when(c == 0)
    def _():
        ht_ref[...] = jnp.zeros_like(ht_ref)

    def chunk(rows):
        xc = xs_ref[rows, :]
        bcc = bc_ref[rows, :]

        dt = _softplus(dt_ref[rows, :] + dtb_ref[...])
        dta = dt * (-jnp.exp(alog_ref[...]))
        causal = (lax.broadcasted_iota(jnp.int32, (cl, cl), 0)
                  >= lax.broadcasted_iota(jnp.int32, (cl, cl), 1))
        tri = jnp.where(causal, 1.0, 0.0).astype(BF16)
        a_cum = jnp.zeros((cl, LANES), F32)
        rest = dta
        for _ in range(3):
            term = rest.astype(BF16)
            a_cum = a_cum + jnp.dot(tri, term, preferred_element_type=F32)
            rest = rest - term.astype(F32)
        a_cum = a_cum * LOG2E
        a_cum_t = a_cum.T
        dt_t = dt.T
        dte_t = jnp.exp2(a_cum_t[:, cl - 1:cl] - a_cum_t) * dt_t
        cd_row = jnp.exp2(a_cum[cl - 1:cl, :])

        lo = lax.broadcasted_iota(jnp.int32, (1, PAIR), 1) < HEAD_DIM

        for g in range(SSM_GROUPS):
            b_f = bcc[:, g * n_state:(g + 1) * n_state]
            bt_g = b_f.T
            c_g = bcc[:, gb + g * n_state:gb + (g + 1) * n_state].astype(BF16)
            cb_g = _nt_dot(c_g, b_f.astype(BF16))
            gated = []
            ssq = jnp.zeros((cl, 1), F32)
            for jj in range(pairs_per_group):
                j = g * pairs_per_group + jj
                sl = slice(j * PAIR, (j + 1) * PAIR)
                x_p = xc[:, sl]
                x_half = (jnp.where(lo, x_p, 0.0).astype(BF16), jnp.where(lo, 0.0, x_p).astype(BF16))
                y = None
                st = None
                ea = []
                for half in range(2):
                    r = 2 * j + half
                    col = jnp.broadcast_to(a_cum[:, r:r + 1], (cl, cl))
                    decay = jnp.exp2(jnp.where(causal, col - a_cum_t[r:r + 1, :], -jnp.inf))
                    w = (cb_g * decay * dt_t[r:r + 1, :]).astype(BF16)
                    part = jnp.dot(w, x_half[half], preferred_element_type=F32)
                    y = part if y is None else y + part
                    part = jnp.dot((bt_g * dte_t[r:r + 1, :]).astype(BF16), x_half[half],
                                   preferred_element_type=F32)
                    st = part if st is None else st + part
                    ea.append(jnp.exp2(col))
                ht_p = ht_ref[:, sl]
                y = y + (jnp.dot(c_g, ht_p.astype(BF16), preferred_element_type=F32)
                         * jnp.where(lo, ea[0], ea[1]))
                ht_ref[:, sl] = ht_p * _lane_pair(cd_row, j, lo) + st
                y = y + dsk_ref[:, sl] * x_p
                gy = y * _silu(z_ref[rows, sl])
                ssq = ssq + jnp.sum(gy * gy, axis=-1, keepdims=True)
                gated.append(gy)
            scale = lax.rsqrt(ssq / (pairs_per_group * PAIR) + EPS)
            for jj in range(pairs_per_group):
                sl = slice((g * pairs_per_group + jj) * PAIR, (g * pairs_per_group + jj + 1) * PAIR)
                yg_ref[rows, sl] = (gated[jj] * scale * ng_ref[:, sl]).astype(BF16)

    chunk(slice(0, cl))

    @pl.when(c == pl.num_programs(1) - 1)
    def _():
        for j in range(n_pairs):
            sl = slice(j * PAIR, (j + 1) * PAIR)
            h_ref[0, sl, :] = ht_ref[:, sl].T


def _ssd_prompt(z, xbc, dt_raw, dt_bias, a_log, d_skip, norm_g, bsz, seq, d_inner, n_state):
    cl = SSD_CHUNK
    assert seq % cl == 0
    rows = cl
    nc = seq // rows
    gb = SSM_GROUPS * n_state
    assert d_inner % (2 * gb) == 0
    bc_blk = d_inner // (2 * gb)
    kern = functools.partial(_ssd_kernel, cl=cl, d_inner=d_inner, n_state=n_state)
    row = lambda b, c: b * nc + c
    const = lambda b, c: (0, 0)
    return pl.pallas_call(
        kern,
        grid=(bsz, nc),
        in_specs=[pl.BlockSpec((rows, d_inner), lambda b, c: (row(b, c), 0)),
                  pl.BlockSpec((rows, d_inner), lambda b, c: (row(b, c), 0)),
                  pl.BlockSpec((rows, 2 * gb), lambda b, c: (row(b, c), bc_blk)),
                  pl.BlockSpec((rows, LANES), lambda b, c: (row(b, c), 0)),
                  pl.BlockSpec((1, LANES), const),
                  pl.BlockSpec((1, LANES), const),
                  pl.BlockSpec((1, d_inner), const),
                  pl.BlockSpec((1, d_inner), const)],
        out_specs=[pl.BlockSpec((rows, d_inner), lambda b, c: (row(b, c), 0)),
                   pl.BlockSpec((1, d_inner, n_state), lambda b, c: (b, 0, 0))],
        out_shape=[jax.ShapeDtypeStruct((bsz * seq, d_inner), BF16),
                   jax.ShapeDtypeStruct((bsz, d_inner, n_state), F32)],
        scratch_shapes=[pltpu.VMEM((n_state, d_inner), F32)],
        compiler_params=_params("parallel", "arbitrary"),
        name="ssd_prompt",
    )(z, xbc, xbc, dt_raw, dt_bias, a_log, d_skip, norm_g)


def _ssd_step_kernel(zx_ref, dt_ref, cs_ref, h_ref, cw_ref, cb_ref, dtb_ref, alog_ref, dsk_ref, ng_ref,
                     yg_ref, cso_ref, ho_ref, *, d_inner, n_state):
    n_pairs = d_inner // PAIR
    pairs_per_group = n_pairs // SSM_GROUPS
    gb = SSM_GROUPS * n_state
    zx = zx_ref[0]
    z = zx[:, 0:d_inner]
    xbc = zx[:, d_inner:]
    prev = cs_ref[0]
    acc = cb_ref[...] + cw_ref[CONV_TAPS - 1:CONV_TAPS, :] * xbc
    for k in range(CONV_TAPS - 1):
        acc = acc + cw_ref[k:k + 1, :] * prev[k:k + 1, :]
    cso_ref[0, 0:CONV_TAPS - 2, :] = prev[1:CONV_TAPS - 1, :]
    cso_ref[0, CONV_TAPS - 2:CONV_TAPS - 1, :] = xbc
    act = jax.nn.silu(acc)
    xs = act[:, 0:d_inner]
    dt = _softplus(dt_ref[0] + dtb_ref[...])
    da = jnp.exp(dt * (-jnp.exp(alog_ref[...])))

    lo = lax.broadcasted_iota(jnp.int32, (1, PAIR), 1) < HEAD_DIM
    row_lo = lax.broadcasted_iota(jnp.int32, (PAIR, 1), 0) < HEAD_DIM
    row8 = lax.broadcasted_iota(jnp.int32, (HALO, 1), 0)

    for g in range(SSM_GROUPS):
        b_g = act[:, d_inner + g * n_state:d_inner + (g + 1) * n_state]
        c_g = act[:, d_inner + gb + g * n_state:d_inner + gb + (g + 1) * n_state]
        gated = []
        ssq = jnp.zeros((1, 1), F32)
        for jj in range(pairs_per_group):
            j = g * pairs_per_group + jj
            sl = slice(j * PAIR, (j + 1) * PAIR)
            x_p = xs[:, sl]
            xdt = x_p * _lane_pair(dt, j, lo)
            x_hi, x_lo = _split2(xdt)
            b_hi, b_lo = _split2(b_g)
            lhs = jnp.where(row8 == 0, x_hi, jnp.where(row8 == 1, x_lo, jnp.where(row8 == 2, x_hi,
                            jnp.where(row8 == 3, x_lo, 0.0)))).astype(BF16)
            rhs = jnp.where(row8 == 0, b_hi, jnp.where(row8 == 1, b_hi, jnp.where(row8 == 2, b_lo,
                            jnp.where(row8 == 3, b_lo, 0.0)))).astype(BF16)
            d_col = jnp.where(row_lo, da[:, 2 * j:2 * j + 1], da[:, 2 * j + 1:2 * j + 2])
            h_new = d_col * h_ref[0, sl, :] + _tn_dot(lhs, rhs)
            ho_ref[0, sl, :] = h_new
            c_hi, c_lo = _split2(c_g)
            c2 = jnp.where(row8 == 0, c_hi, jnp.where(row8 == 1, c_lo, 0.0)).astype(BF16)
            h_hi = h_new.astype(BF16)
            h_lo = (h_new - h_hi.astype(F32)).astype(BF16)
            yy = _nt_dot(c2, h_hi) + _nt_dot(c2, h_lo)
            y = yy[0:1, :] + yy[1:2, :]
            y = y + dsk_ref[:, sl] * x_p
            gy = y * jax.nn.silu(z[:, sl])
            ssq = ssq + jnp.sum(gy * gy, axis=-1, keepdims=True)
            gated.append(gy)
        scale = lax.rsqrt(ssq / (pairs_per_group * PAIR) + EPS)
        for jj in range(pairs_per_group):
            sl = slice((g * pairs_per_group + jj) * PAIR, (g * pairs_per_group + jj + 1) * PAIR)
            yg_ref[0, :, sl] = gated[jj] * scale * ng_ref[:, sl]


def _ssd_step(zx, dt_raw, state_conv, state_ssm, conv_w, conv_b, dt_bias, a_log, d_skip, norm_g,
              d_inner, n_state):
    bsz = zx.shape[0]
    conv_dim = conv_w.shape[1]
    kern = functools.partial(_ssd_step_kernel, d_inner=d_inner, n_state=n_state)
    const = lambda b: (0, 0)
    per_b = lambda b: (b, 0, 0)
    return pl.pallas_call(
        kern,
        grid=(bsz,),
        in_specs=[pl.BlockSpec((1, 1, d_inner + conv_dim), per_b),
                  pl.BlockSpec((1, 1, LANES), per_b),
                  pl.BlockSpec((1, CONV_TAPS - 1, conv_dim), per_b),
                  pl.BlockSpec((1, d_inner, n_state), per_b),
                  pl.BlockSpec((CONV_TAPS, conv_dim), const),
                  pl.BlockSpec((1, conv_dim), const),
                  pl.BlockSpec((1, LANES), const),
                  pl.BlockSpec((1, LANES), const),
                  pl.BlockSpec((1, d_inner), const),
                  pl.BlockSpec((1, d_inner), const)],
        out_specs=[pl.BlockSpec((1, 1, d_inner), per_b),
                   pl.BlockSpec((1, CONV_TAPS - 1, conv_dim), per_b),
                   pl.BlockSpec((1, d_inner, n_state), per_b)],
        out_shape=[jax.ShapeDtypeStruct((bsz, 1, d_inner), F32),
                   jax.ShapeDtypeStruct((bsz, CONV_TAPS - 1, conv_dim), F32),
                   jax.ShapeDtypeStruct((bsz, d_inner, n_state), F32)],
        compiler_params=_params("parallel"),
        name="ssd_step",
    )(zx.reshape(bsz, 1, -1), dt_raw.reshape(bsz, 1, LANES), state_conv,
      state_ssm.reshape(bsz, d_inner, n_state), conv_w, conv_b, dt_bias, a_log, d_skip, norm_g)


def _linear_res_kernel(a_ref, w_ref, r_ref, o_ref):
    o_ref[...] = r_ref[...] + jnp.dot(a_ref[...].astype(BF16), w_ref[...], preferred_element_type=F32)


def _linear_res(a, w, res):
    m, k = a.shape
    n = w.shape[1]
    tm = _row_tile(m, 512)
    return pl.pallas_call(
        _linear_res_kernel,
        grid=(m // tm,),
        in_specs=[pl.BlockSpec((tm, k), lambda i: (i, 0)),
                  pl.BlockSpec((k, n), lambda i: (0, 0)),
                  pl.BlockSpec((tm, n), lambda i: (i, 0))],
        out_specs=pl.BlockSpec((tm, n), lambda i: (i, 0)),
        out_shape=jax.ShapeDtypeStruct((m, n), F32),
        compiler_params=_params("parallel"),
        name="linear_res",
    )(a, w, res)


def _ffn_kernel(x_ref, g_ref, wg_ref, wu_ref, wd_ref, o_ref, xn_ref, *, tf):
    @pl.when(pl.program_id(1) == 0)
    def _():
        x = x_ref[...]
        xn_ref[...] = (_rms_scale(x) * g_ref[...]).astype(BF16)
        o_ref[...] = x

    xn = xn_ref[...]
    acts = []
    for c0 in range(0, tf, MXU_WIDTH):
        cols = slice(c0, min(c0 + MXU_WIDTH, tf))
        gate = jnp.dot(xn, wg_ref[:, cols], preferred_element_type=F32)
        up = jnp.dot(xn, wu_ref[:, cols], preferred_element_type=F32)
        acts.append((jax.nn.silu(gate) * up).astype(BF16))
    o_ref[...] += jnp.dot(jnp.concatenate(acts, axis=1), wd_ref[...], preferred_element_type=F32)


def _ffn(x, g, w_gate_up, w_down):
    m, d = x.shape
    d_ff = w_down.shape[0]
    tm = _row_tile(m, 1024)
    tf = d_ff // 2 if d_ff % (2 * LANES) == 0 else d_ff
    nf = d_ff // tf
    return pl.pallas_call(
        functools.partial(_ffn_kernel, tf=tf),
        grid=(m // tm, nf),
        in_specs=[pl.BlockSpec((tm, d), lambda i, j: (i, 0)),
                  pl.BlockSpec((1, d), lambda i, j: (0, 0)),
                  pl.BlockSpec((d, tf), lambda i, j: (0, j)),
                  pl.BlockSpec((d, tf), lambda i, j: (0, nf + j)),
                  pl.BlockSpec((tf, d), lambda i, j: (j, 0))],
        out_specs=pl.BlockSpec((tm, d), lambda i, j: (i, 0)),
        out_shape=jax.ShapeDtypeStruct((m, d), F32),
        scratch_shapes=[pltpu.VMEM((tm, d), BF16)],
        compiler_params=_params("parallel", "arbitrary"),
        name="ffn",
    )(x, g, w_gate_up, w_gate_up, w_down)


def _qkv_kernel(x_ref, gq_ref, gkv_ref, w_ref, qn_ref, kn_ref, knc_ref, q_ref, k_ref, v4_ref, vd_ref,
                *, n_heads, k_transposed):
    lo = lax.broadcasted_iota(jnp.int32, (1, PAIR), 1) < HEAD_DIM

    def qk_norm(y, gain_ref, out_ref, post):
        for h in range(n_heads):
            yh = y[:, h * PAIR:(h + 1) * PAIR]
            sq = yh * yh
            s_lo = jnp.sum(jnp.where(lo, sq, 0.0), axis=-1, keepdims=True)
            s_hi = jnp.sum(jnp.where(lo, 0.0, sq), axis=-1, keepdims=True)
            r = jnp.where(lo, lax.rsqrt(s_lo / HEAD_DIM + EPS), lax.rsqrt(s_hi / HEAD_DIM + EPS))
            out_ref[:, h * PAIR:(h + 1) * PAIR] = (yh * r * gain_ref[...] * post).astype(out_ref.dtype)

    xh = _rms_scale(x_ref[...])
    xq = (xh * gq_ref[...]).astype(BF16)
    xkv = (xh * gkv_ref[...]).astype(BF16)
    qk_norm(jnp.dot(xq, w_ref[0], preferred_element_type=F32), qn_ref, q_ref, HEAD_DIM ** -0.5 * LOG2E)

    if k_transposed:
        kt = _nt_dot(w_ref[1], xkv)
        for g in range(2 * n_heads):
            rows = slice(g * HEAD_DIM, (g + 1) * HEAD_DIM)
            blk = kt[rows, :]
            r = lax.rsqrt(jnp.mean(blk * blk, axis=0, keepdims=True) + EPS)
            k_ref[0, rows, :] = blk * r * knc_ref[...]
    else:
        qk_norm(jnp.dot(xkv, w_ref[1], preferred_element_type=F32), kn_ref, k_ref, 1.0)

    v = jnp.dot(xkv, w_ref[2], preferred_element_type=F32)
    vd_ref[...] = v.astype(BF16)
    for h in range(n_heads):
        v4_ref[:, h, :] = v[:, h * PAIR:(h + 1) * PAIR]


def _qkv(x, g_q, g_kv, w_q, w_k, w_v, q_norm, k_norm, seq=None):
    m, d = x.shape
    n_heads = d // PAIR
    k_transposed = seq is not None
    tm = _row_tile(m if seq is None else seq, 512)
    w = jnp.stack([w_q, w_k.T if k_transposed else w_k, w_v]).astype(BF16)
    pair_gain = jnp.tile(k_norm.astype(F32), 2).reshape(1, PAIR)
    kern = functools.partial(_qkv_kernel, n_heads=n_heads, k_transposed=k_transposed)
    const = lambda i: (0, 0)
    rowblk = lambda i: (i, 0)
    if k_transposed:
        per_seq = seq // tm
        k_spec = pl.BlockSpec((1, d, tm), lambda i: (i // per_seq, 0, i % per_seq))
        k_shape = jax.ShapeDtypeStruct((m // seq, d, seq), F32)
    else:
        k_spec = pl.BlockSpec((tm, d), rowblk)
        k_shape = jax.ShapeDtypeStruct((m, d), F32)
    return pl.pallas_call(
        kern,
        grid=(m // tm,),
        in_specs=[pl.BlockSpec((tm, d), rowblk),
                  pl.BlockSpec((1, d), const),
                  pl.BlockSpec((1, d), const),
                  pl.BlockSpec((3, d, d), lambda i: (0, 0, 0)),
                  pl.BlockSpec((1, PAIR), const),
                  pl.BlockSpec((1, PAIR), const),
                  pl.BlockSpec((HEAD_DIM, 1), const)],
        out_specs=[pl.BlockSpec((tm, d), rowblk), k_spec,
                   pl.BlockSpec((tm, n_heads, PAIR), lambda i: (i, 0, 0)),
                   pl.BlockSpec((tm, d), rowblk)],
        out_shape=[jax.ShapeDtypeStruct((m, d), BF16), k_shape,
                   jax.ShapeDtypeStruct((m, n_heads, PAIR), F32),
                   jax.ShapeDtypeStruct((m, d), BF16)],
        compiler_params=_params("parallel"),
        name="qkv_proj",
    )(x, g_q, g_kv, w, jnp.tile(q_norm.astype(F32), 2).reshape(1, PAIR), pair_gain,
      k_norm.astype(F32).reshape(HEAD_DIM, 1))


def _bias_tiles_kernel(rb_ref, o_ref, *, t):
    h = pl.program_id(0)
    i = lax.broadcasted_iota(jnp.int32, (t, t), 0)
    j = lax.broadcasted_iota(jnp.int32, (t, t), 1)
    for off in range(2):
        n = i - j + off * t
        bucket = _bucket(n)
        tile = jnp.zeros((t, t), F32)
        for b in range(N_BUCKETS):
            tile = jnp.where(bucket == b, rb_ref[b, h], tile)
        o_ref[0, off] = jnp.where(n >= 0, (tile - rb_ref[N_BUCKETS - 1, h]) * LOG2E, -jnp.inf)


def _bias_tiles(rel_bias, t):
    n_heads = rel_bias.shape[1]
    return pl.pallas_call(
        functools.partial(_bias_tiles_kernel, t=t),
        grid=(n_heads,),
        in_specs=[pl.BlockSpec(memory_space=pltpu.SMEM)],
        out_specs=pl.BlockSpec((1, 2, t, t), lambda h: (h, 0, 0, 0)),
        out_shape=jax.ShapeDtypeStruct((n_heads, 2, t, t), F32),
        compiler_params=_params("parallel"),
        name="bias_tiles",
    )(rel_bias)


def _lambda_value(lam_ref, lambda_init):
    s1 = jnp.sum(lam_ref[0:1, :] * lam_ref[1:2, :], axis=-1, keepdims=True)
    s2 = jnp.sum(lam_ref[2:3, :] * lam_ref[3:4, :], axis=-1, keepdims=True)
    return jnp.exp(s1) - jnp.exp(s2) + lambda_init


def _attn_kernel(q_ref, k_ref, v_ref, bias_ref, rb_ref, lam_ref, sg_ref, o_ref,
                 kb_ref, kb2_ref, q12_ref, m_ref, l_ref, acc_ref, *, t, rc_far, rc_near, nq, lambda_init):
    h = pl.program_id(1)
    qi = pl.program_id(2)

    @pl.when(qi == 0)
    def _():
        for ki in range(nq):
            kb_ref[ki] = k_ref[0, :, ki * t:(ki + 1) * t].astype(BF16)
        for kp in range(nq // 2):
            kb2_ref[kp] = k_ref[0, :, kp * 2 * t:(kp + 1) * 2 * t].astype(BF16)

    q = q_ref[0]
    lo = lax.broadcasted_iota(jnp.int32, (t, PAIR), 1) < HEAD_DIM
    zero = jnp.zeros_like(q)
    q12_ref[0:t, :] = jnp.where(lo, q, zero)
    q12_ref[t:2 * t, :] = jnp.where(lo, zero, q)

    m_ref[...] = jnp.full(m_ref.shape, -jnp.inf, F32)
    l_ref[...] = jnp.zeros(l_ref.shape, F32)
    acc_ref[...] = jnp.zeros(acc_ref.shape, F32)

    far_bias = rb_ref[N_BUCKETS - 1, h] * LOG2E

    def block(rc, keys_of, base, cols_of, near_of=None):
        for c in range(2 * t // rc):
            rows = slice(c * rc, (c + 1) * rc)
            r0 = (c * rc) % t
            ncols = cols_of(r0)
            s = jnp.dot(q12_ref[rows, :], keys_of(ncols), preferred_element_type=F32)
            tiles = []
            for k in range(ncols // LANES):
                sk = s[:, k * LANES:(k + 1) * LANES]
                near = None if near_of is None else near_of(k)
                if near is not None and near[0] * t + r0 - (near[1] * LANES + LANES - 1) < MAX_DISTANCE:
                    sk = sk + bias_ref[0, near[0], r0:r0 + rc, near[1] * LANES:(near[1] + 1) * LANES]
                tiles.append(sk)
            m_prev = m_ref[rows, :]
            part = tiles[0]
            for sk in tiles[1:]:
                part = jnp.maximum(part, sk)
            m_new = jnp.maximum(m_prev, jnp.max(part, axis=-1, keepdims=True) + far_bias)
            alpha = jnp.exp2(m_prev - m_new)
            m_sub = m_new - far_bias
            ps = [jnp.exp2(sk - m_sub) for sk in tiles]
            psum = ps[0]
            for pk in ps[1:]:
                psum = psum + pk
            l_ref[rows, :] = alpha * l_ref[rows, :] + psum
            pv = jnp.dot(jnp.concatenate(ps, axis=1).astype(BF16), v_ref[0, pl.ds(base, ncols), :],
                         preferred_element_type=F32)
            acc_ref[rows, :] = alpha * acc_ref[rows, :] + pv
            m_ref[rows, :] = m_new

    n_far = jnp.maximum(qi - 1, 0)

    def far_pair(kp, carry):
        block(rc_far, lambda n: kb2_ref[kp, :, 0:n], pl.multiple_of(kp * 2 * t, 2 * t), lambda r0: 2 * t)
        return carry

    lax.fori_loop(0, n_far // 2, far_pair, 0)

    @pl.when(n_far % 2 == 1)
    def _():
        ki = n_far - 1
        block(rc_far, lambda n: kb_ref[ki, :, 0:n], pl.multiple_of(ki * t, t), lambda r0: t)

    nlt = t // LANES

    @pl.when(qi >= 1)
    def _():
        ki = qi - 1
        block(rc_near, lambda n: jnp.concatenate([kb_ref[ki], kb_ref[qi, :, 0:n - t]], axis=1),
              pl.multiple_of(ki * t, t), lambda r0: t + r0 + rc_near,
              near_of=lambda k: (1, k) if k < nlt else (0, k - nlt))

    @pl.when(qi == 0)
    def _():
        block(rc_near, lambda n: kb_ref[0, :, 0:n], 0, lambda r0: r0 + rc_near, near_of=lambda k: (0, k))

    l = jnp.sum(l_ref[...], axis=-1, keepdims=True)
    o1 = acc_ref[0:t, :] / l[0:t, :]
    o2 = acc_ref[t:2 * t, :] / l[t:2 * t, :]
    o = o1 - _lambda_value(lam_ref, lambda_init) * o2
    o_ref[0] = (_rms_scale(o) * sg_ref[...] * (1.0 - lambda_init)).astype(o_ref.dtype)


def _prompt_attention(q, k_t, v, rel_bias, lam_rows, subln, lambda_init, bsz, seq):
    d = q.shape[-1]
    n_heads = d // PAIR
    t = min(seq, 512)
    assert seq % t == 0 and t >= MAX_DISTANCE
    nq = seq // t
    tiles = _bias_tiles(rel_bias, t)
    rc_far, rc_near = t, t
    assert t % rc_near == 0 and rc_near % LANES == 0
    kern = functools.partial(_attn_kernel, t=t, rc_far=rc_far, rc_near=rc_near, nq=nq, lambda_init=lambda_init)
    const = lambda b, h, i: (0, 0)
    return pl.pallas_call(
        kern,
        grid=(bsz, n_heads, nq),
        in_specs=[pl.BlockSpec((1, t, PAIR), lambda b, h, i: (b, i, h)),
                  pl.BlockSpec((1, PAIR, seq), lambda b, h, i: (b, h, 0)),
                  pl.BlockSpec((1, seq, PAIR), lambda b, h, i: (b, 0, h)),
                  pl.BlockSpec((1, 2, t, t), lambda b, h, i: (h, 0, 0, 0)),
                  pl.BlockSpec(memory_space=pltpu.SMEM),
                  pl.BlockSpec((4, LANES), const),
                  pl.BlockSpec((1, PAIR), const)],
        out_specs=pl.BlockSpec((1, t, PAIR), lambda b, h, i: (b, i, h)),
        out_shape=jax.ShapeDtypeStruct((bsz, seq, d), BF16),
        scratch_shapes=[pltpu.VMEM((nq, PAIR, t), BF16), pltpu.VMEM((max(nq // 2, 1), PAIR, 2 * t), BF16),
                        pltpu.VMEM((2 * t, PAIR), BF16),
                        pltpu.VMEM((2 * t, LANES), F32), pltpu.VMEM((2 * t, LANES), F32),
                        pltpu.VMEM((2 * t, PAIR), F32)],
        compiler_params=_params("parallel", "parallel", "arbitrary"),
        name="prompt_attention",
    )(q.reshape(bsz, seq, d), k_t, v.reshape(bsz, seq, d), tiles, rel_bias, lam_rows, subln)


def _sample_attn_kernel(pt_ref, q_ref, kn_ref, vn_ref, *rest, pages_per_step, past, n_heads, lambda_init):
    k_refs = rest[0:pages_per_step]
    v_refs = rest[pages_per_step:2 * pages_per_step]
    rb_ref, lam_ref, sg_ref, o_ref, qb_ref, bias_ref, m_ref, l_ref, acc_ref = rest[2 * pages_per_step:]
    step_id = pl.program_id(1)
    last = pl.num_programs(1) - 1
    nrow = 2 * n_heads
    d = n_heads * PAIR
    keys = pages_per_step * PAGE_SIZE

    @pl.when(step_id == 0)
    def _():
        bias_ref[...] = jnp.broadcast_to(rb_ref[:, N_BUCKETS - 1:N_BUCKETS] * LOG2E, (nrow, keys))

    @pl.when(step_id == last)
    def _():
        key = lax.broadcasted_iota(jnp.int32, (nrow, keys), 1)
        bucket = _bucket(past - (step_id * keys + key))
        bias = jnp.zeros((nrow, keys), F32)
        for b in range(N_BUCKETS):
            bias = jnp.where(bucket == b, rb_ref[:, b:b + 1], bias)
        bias_ref[...] = bias * LOG2E

    @pl.when(step_id == 0)
    def _():
        q = q_ref[0].astype(F32)
        row = lax.broadcasted_iota(jnp.int32, (nrow, d), 0)
        col = lax.broadcasted_iota(jnp.int32, (nrow, d), 1)
        qblk = jnp.where(jnp.right_shift(col, 6) == row, jnp.broadcast_to(q, (nrow, d)), 0.0)
        qb_ref[...] = qblk.astype(BF16)
        m_ref[...] = jnp.sum(qblk * kn_ref[0], axis=-1, keepdims=True) + rb_ref[:, 0:1] * LOG2E
        l_ref[...] = jnp.ones(l_ref.shape, F32)
        vn = vn_ref[0].astype(F32)
        for h in range(n_heads):
            acc_ref[2 * h:2 * h + 2, :] = jnp.broadcast_to(vn[:, h * PAIR:(h + 1) * PAIR], (2, PAIR))

    qb = qb_ref[...]
    s = jnp.concatenate([jnp.dot(qb, k_refs[g][0].astype(BF16), preferred_element_type=F32)
                         for g in range(pages_per_step)], axis=1) + bias_ref[...]
    m_prev = m_ref[...]
    m_new = jnp.maximum(m_prev, jnp.max(s, axis=-1, keepdims=True))
    alpha = jnp.exp2(m_prev - m_new)
    p = jnp.exp2(s - m_new)
    l_ref[...] = alpha * l_ref[...] + jnp.sum(p, axis=-1, keepdims=True)
    pb = p.astype(BF16)
    row_head = jnp.right_shift(lax.broadcasted_iota(jnp.int32, (nrow, PAIR), 0), 1)
    pv = jnp.zeros((nrow, PAIR), F32)
    for h in range(n_heads):
        v_h = jnp.concatenate([v_refs[g][0, pl.ds(h, PAGE_SIZE, stride=n_heads), :].astype(BF16)
                               for g in range(pages_per_step)], axis=0)
        pv = jnp.where(row_head == h, jnp.dot(pb, v_h, preferred_element_type=F32), pv)
    acc_ref[...] = alpha * acc_ref[...] + pv
    m_ref[...] = m_new

    @pl.when(step_id == last)
    def _():
        rowc = lax.broadcasted_iota(jnp.int32, (nrow, 1), 0)
        lam = _lambda_value(lam_ref, lambda_init)
        coef = jnp.where(jnp.bitwise_and(rowc, 1) == 0, 1.0, -lam) / l_ref[...]
        scaled = acc_ref[...] * coef
        o = scaled + pltpu.roll(scaled, nrow - 1, 0)
        on = _rms_scale(o) * sg_ref[...] * (1.0 - lambda_init)
        for h in range(n_heads):
            o_ref[0, :, h * PAIR:(h + 1) * PAIR] = on[2 * h:2 * h + 1, :]


def _sample_attention(q, k_new, v_new, cache_k, cache_v, page_table, rel_rows, lam_rows, subln, lambda_init):
    bsz, d = q.shape
    n_heads = d // PAIR
    n_pages = page_table.shape[1]
    past = n_pages * PAGE_SIZE
    pages_per_step = max(g for g in (16, 8, 4, 2, 1) if n_pages % g == 0)
    n_pool = cache_k.shape[0]
    kern = functools.partial(_sample_attn_kernel, pages_per_step=pages_per_step, past=past,
                             n_heads=n_heads, lambda_init=lambda_init)
    per_b = lambda b, s, pt: (b, 0, 0)
    const = lambda b, s, pt: (0, 0)

    def k_spec(g):
        return pl.BlockSpec((1, d, PAGE_SIZE), lambda b, s, pt: (pt[b, s * pages_per_step + g], 0, 0))

    def v_spec(g):
        return pl.BlockSpec((1, PAGE_SIZE * n_heads, PAIR),
                            lambda b, s, pt: (pt[b, s * pages_per_step + g], 0, 0))

    grid_spec = pltpu.PrefetchScalarGridSpec(
        num_scalar_prefetch=1,
        grid=(bsz, n_pages // pages_per_step),
        in_specs=([pl.BlockSpec((1, 1, d), per_b)] * 3
                  + [k_spec(g) for g in range(pages_per_step)]
                  + [v_spec(g) for g in range(pages_per_step)]
                  + [pl.BlockSpec((2 * n_heads, N_BUCKETS), const),
                     pl.BlockSpec((4, LANES), const),
                     pl.BlockSpec((1, PAIR), const)]),
        out_specs=pl.BlockSpec((1, 1, d), per_b),
        scratch_shapes=[pltpu.VMEM((2 * n_heads, d), BF16),
                        pltpu.VMEM((2 * n_heads, pages_per_step * PAGE_SIZE), F32),
                        pltpu.VMEM((2 * n_heads, 1), F32), pltpu.VMEM((2 * n_heads, 1), F32),
                        pltpu.VMEM((2 * n_heads, PAIR), F32)],
    )
    ck = jnp.transpose(cache_k, (0, 2, 3, 4, 1)).reshape(n_pool, d, PAGE_SIZE)
    cv = cache_v.reshape(n_pool, PAGE_SIZE * n_heads, PAIR)
    return pl.pallas_call(
        kern,
        grid_spec=grid_spec,
        out_shape=jax.ShapeDtypeStruct((bsz, 1, d), F32),
        compiler_params=_params("parallel", "arbitrary"),
        name="sample_attention",
    )(page_table, q.reshape(bsz, 1, d), k_new.reshape(bsz, 1, d), v_new.reshape(bsz, 1, d),
      *([ck] * pages_per_step), *([cv] * pages_per_step), rel_rows, lam_rows, subln).reshape(bsz, d)


def _pad_lanes(v):
    return jnp.pad(v.astype(F32), (0, LANES - v.shape[0])).reshape(1, LANES)


def kernel(x_prompt, x_sample, state_conv, state_ssm, cache_k, cache_v, page_table, norm_mix, norm_ffn, w_in, conv_w, conv_b, dt_bias, a_log, d_skip, ssm_norm, w_out_ssm, norm_kv, w_kv, k_norm, w_q, q_norm, lambda_q1, lambda_k1, lambda_q2, lambda_k2, subln, w_o, rel_bias, w_gate_up, w_down):
    bsz, seq, d = x_prompt.shape
    dec = x_sample.shape[0]
    assert x_sample.shape[1] == 1
    n_ssm_layers, ssm_heads = dt_bias.shape
    depth = norm_mix.shape[0]
    assert n_ssm_layers == 1 and depth == 2
    d_inner = w_out_ssm.shape[1]
    n_state = state_ssm.shape[-1]
    assert d_inner == ssm_heads * HEAD_DIM and n_state == LANES and ssm_heads <= LANES
    n_heads = d // PAIR

    xp = x_prompt.reshape(bsz * seq, d)
    xs = x_sample.reshape(dec, d)
    row = lambda v: v.astype(F32).reshape(1, -1)

    zx_cols = 2 * d_inner + 2 * SSM_GROUPS * n_state
    w_z = w_in[0][:, :d_inner].astype(BF16)
    w_c = w_in[0][:, d_inner:zx_cols].astype(BF16)
    w_dt = jnp.pad(w_in[0][:, zx_cols:], ((0, 0), (0, LANES - ssm_heads))).astype(BF16)
    g_mix0 = row(norm_mix[0])
    cw0, cb0 = conv_w[0], row(conv_b[0])
    ssd_args = (_pad_lanes(dt_bias[0]), _pad_lanes(a_log[0]),
                row(jnp.repeat(d_skip[0], HEAD_DIM)), row(ssm_norm[0]))
    w_out = w_out_ssm[0].astype(BF16)
    w_gu0, w_dn0 = w_gate_up[0].astype(BF16), w_down[0].astype(BF16)

    z_p, xbc_p, dt_p, conv_p = _in_proj(xp, g_mix0, w_z, w_c, w_dt, cw0, cb0, seq=seq)
    yg_p, ssm_p = _ssd_prompt(z_p, xbc_p, dt_p, *ssd_args, bsz, seq, d_inner, n_state)
    xp = _linear_res(yg_p, w_out, xp)
    xp = _ffn(xp, row(norm_ffn[0]), w_gu0, w_dn0)

    z_s, xbc_s, dt_s = _in_proj(xs, g_mix0, w_z, w_c, w_dt)
    zx_s = jnp.concatenate([z_s, xbc_s], axis=1)
    yg_s, conv_s, ssm_s = _ssd_step(zx_s, dt_s, state_conv[0], state_ssm[0], cw0, cb0, *ssd_args,
                                    d_inner, n_state)
    xs = _linear_res(yg_s.reshape(dec, d_inner), w_out, xs)
    xs = _ffn(xs, row(norm_ffn[0]), w_gu0, w_dn0)

    lambda_init = 0.8 - 0.6 * math.exp(-0.3 * 1)
    qkv_args = (row(norm_mix[1]), row(norm_kv), w_q[0], w_kv[:, :d], w_kv[:, d:], q_norm[0], k_norm)
    lam_rows = jnp.pad(jnp.stack([lambda_q1[0], lambda_k1[0], lambda_q2[0], lambda_k2[0]]).astype(F32),
                       ((0, 0), (0, LANES - HEAD_DIM)))
    sg = row(subln[0])
    w_oo = w_o[0].astype(BF16)
    w_gu1, w_dn1 = w_gate_up[1].astype(BF16), w_down[1].astype(BF16)

    q_p, kt_p, v_p, vd_p = _qkv(xp, *qkv_args, seq=seq)
    o_p = _prompt_attention(q_p, kt_p, vd_p, rel_bias.astype(F32), lam_rows, sg, lambda_init, bsz, seq)
    xp = _linear_res(o_p.reshape(bsz * seq, d), w_oo, xp)
    xp = _ffn(xp, row(norm_ffn[1]), w_gu1, w_dn1)

    q_s, k_s, v_s, vd_s = _qkv(xs, *qkv_args)
    rel_rows = jnp.repeat(rel_bias.astype(F32).T, 2, axis=0)
    o_s = _sample_attention(q_s.astype(F32), k_s, vd_s.astype(F32), cache_k, cache_v, page_table,
                            rel_rows, lam_rows, sg, lambda_init)
    xs = _linear_res(o_s, w_oo, xs)
    xs = _ffn(xs, row(norm_ffn[1]), w_gu1, w_dn1)

    k_p = jnp.transpose(kt_p.reshape(bsz, n_heads, 2, HEAD_DIM, seq), (0, 4, 1, 2, 3))
    return (xp.reshape(bsz, seq, d), xs.reshape(dec, 1, d),
            conv_p[None], ssm_p.reshape(1, bsz, ssm_heads, HEAD_DIM, n_state),
            k_p, v_p.reshape(bsz, seq, n_heads, PAIR),
            conv_s[None], ssm_s.reshape(1, dec, ssm_heads, HEAD_DIM, n_state),
            k_s.reshape(dec, 1, n_heads, 2, HEAD_DIM), v_s.reshape(dec, 1, n_heads, PAIR))
```

```python
import functools
import math

import jax
import jax.numpy as jnp
from jax import lax
from jax.experimental import pallas as pl
from jax.experimental.pallas import tpu as pltpu

F32 = jnp.float32
BF16 = jnp.bfloat16

EPS = 1e-6
LANES = 128
HEAD_DIM = 64
PAIR = 2 * HEAD_DIM
SSD_CHUNK = 128
SSM_GROUPS = 4
CONV_TAPS = 4
HALO = 8
N_BUCKETS = 32
MAX_EXACT = N_BUCKETS // 2
MAX_DISTANCE = 128
PAGE_SIZE = 128
LOG2E = math.log2(math.e)
MXU_WIDTH = 256
VMEM_LIMIT = 52 * 1024 * 1024


def _params(*semantics):
    return pltpu.CompilerParams(dimension_semantics=semantics, vmem_limit_bytes=VMEM_LIMIT)


def _row_tile(m, preferred):
    if m <= preferred:
        return m
    t = preferred
    while m % t or t % 16:
        t -= 1
    return t


def _nt_dot(a, b):
    return lax.dot_general(a, b, (((1,), (1,)), ((), ())), preferred_element_type=F32)


def _tn_dot(a, b):
    return lax.dot_general(a, b, (((0,), (0,)), ((), ())), preferred_element_type=F32)


def _rms_scale(x):
    return x * lax.rsqrt(jnp.mean(x * x, axis=-1, keepdims=True) + EPS)


def _split2(x):
    hi = x.astype(BF16).astype(F32)
    lo = (x - hi).astype(BF16).astype(F32)
    return hi, lo


def _silu(x):
    h = 0.5 * x
    return h + h * jnp.tanh(h)


def _log1p(u):
    w = 1.0 + u
    return jnp.where(w == 1.0, u, jnp.log(w) * (u / (w - 1.0)))


def _softplus(x):
    return jnp.maximum(x, 0.0) + _log1p(jnp.exp(-jnp.abs(x)))


def _lane_pair(arr, j, lo):
    return jnp.where(lo, arr[:, 2 * j:2 * j + 1], arr[:, 2 * j + 1:2 * j + 2])


def _bucket(n):
    n = jnp.maximum(n, 0)
    nf = jnp.maximum(n, 1).astype(F32)
    large = MAX_EXACT + (jnp.log(nf / MAX_EXACT) / math.log(MAX_DISTANCE / MAX_EXACT)
                         * (N_BUCKETS - MAX_EXACT)).astype(jnp.int32)
    large = jnp.minimum(large, N_BUCKETS - 1)
    return jnp.where(n < MAX_EXACT, n, large)


def _in_proj_kernel(*refs, tz, tc, tiles_per_seq):
    conv = tiles_per_seq is not None
    if conv:
        (x_ref, g_ref, wz_ref, wc_ref, wdt_ref, cw_ref, cb_ref,
         z_ref, xc_ref, dt_ref, ct_ref, xn_ref, tail_ref) = refs
    else:
        x_ref, g_ref, wz_ref, wc_ref, wdt_ref, z_ref, xc_ref, dt_ref, xn_ref = refs
    i = pl.program_id(0)
    j = pl.program_id(1)
    tm = x_ref.shape[0]
    step = 2 * MXU_WIDTH

    @pl.when(j == 0)
    def _():
        xn_ref[...] = (_rms_scale(x_ref[...]) * g_ref[...]).astype(BF16)
        dt_ref[...] = jnp.dot(xn_ref[...], wdt_ref[...], preferred_element_type=F32)

    xn = xn_ref[...]
    c_slices = [slice(c0, min(c0 + step, tc)) for c0 in range(0, tc, step)]
    z_slices = [slice(c0, min(c0 + step, tz)) for c0 in range(0, tz, step)]
    if not conv:
        for cols in c_slices:
            xc_ref[:, cols] = jnp.dot(xn, wc_ref[:, cols], preferred_element_type=F32)
        for cols in z_slices:
            z_ref[:, cols] = jnp.dot(xn, wz_ref[:, cols], preferred_element_type=F32)
        return

    raws = [jnp.dot(xn, wc_ref[:, cols], preferred_element_type=F32) for cols in c_slices]
    for cols in z_slices:
        z_ref[:, cols] = jnp.dot(xn, wz_ref[:, cols], preferred_element_type=F32)

    seq_start = (i % tiles_per_seq) == 0
    row8 = lax.broadcasted_iota(jnp.int32, (HALO, 1), 0)
    for cols, raw in zip(c_slices, raws):
        tail = jnp.where(seq_start, 0.0, tail_ref[j, :, cols])
        acc = cb_ref[:, cols] + cw_ref[CONV_TAPS - 1:CONV_TAPS, cols] * raw
        for back in range(1, CONV_TAPS):
            shifted = pltpu.roll(raw, back, 0)
            head = jnp.where(row8 < back, pltpu.roll(tail, back, 0), shifted[0:HALO, :])
            shifted = jnp.concatenate([head, shifted[HALO:, :]], axis=0)
            k = CONV_TAPS - 1 - back
            acc = acc + cw_ref[k:k + 1, cols] * shifted
        tail_ref[j, :, cols] = raw[tm - HALO:tm, :]
        ct_ref[0, :, cols] = raw[tm - (CONV_TAPS - 1):tm, :]
        xc_ref[:, cols] = _silu(acc)


def _in_proj(x, g, w_z, w_c, w_dt, conv_w=None, conv_b=None, seq=None):
    m, d = x.shape
    nz, nc = w_z.shape[1], w_c.shape[1]
    conv = seq is not None
    tm = _row_tile(m if seq is None else seq, 1024)
    assert nz % (2 * LANES) == 0 and nc % (2 * LANES) == 0
    tz, tc = nz // 2, nc // 2
    rowblk = lambda i, j: (i, j)
    colblk = lambda i, j: (0, j)
    in_specs = [pl.BlockSpec((tm, d), lambda i, j: (i, 0)),
                pl.BlockSpec((1, d), lambda i, j: (0, 0)),
                pl.BlockSpec((d, tz), colblk),
                pl.BlockSpec((d, tc), colblk),
                pl.BlockSpec((d, LANES), lambda i, j: (0, 0))]
    out_specs = [pl.BlockSpec((tm, tz), rowblk), pl.BlockSpec((tm, tc), rowblk),
                 pl.BlockSpec((tm, LANES), lambda i, j: (i, 0))]
    out_shape = [jax.ShapeDtypeStruct((m, nz), F32), jax.ShapeDtypeStruct((m, nc), F32),
                 jax.ShapeDtypeStruct((m, LANES), F32)]
    scratch = [pltpu.VMEM((tm, d), BF16)]
    args = [x, g, w_z, w_c, w_dt]
    tiles_per_seq = None
    if conv:
        tiles_per_seq = seq // tm
        in_specs += [pl.BlockSpec((CONV_TAPS, tc), colblk), pl.BlockSpec((1, tc), colblk)]
        out_specs.append(pl.BlockSpec((1, CONV_TAPS - 1, tc), lambda i, j: (i, 0, j)))
        out_shape.append(jax.ShapeDtypeStruct((m // tm, CONV_TAPS - 1, nc), F32))
        scratch.append(pltpu.VMEM((2, HALO, tc), F32))
        args += [conv_w, conv_b]
    outs = pl.pallas_call(
        functools.partial(_in_proj_kernel, tz=tz, tc=tc, tiles_per_seq=tiles_per_seq),
        grid=(m // tm, 2),
        in_specs=in_specs,
        out_specs=out_specs,
        out_shape=out_shape,
        scratch_shapes=scratch,
        compiler_params=_params("arbitrary", "arbitrary"),
        name="in_proj",
    )(*args)
    outs = list(outs)
    if conv:
        outs[3] = outs[3][tiles_per_seq - 1::tiles_per_seq]
    return outs


def _ssd_kernel(z_ref, xs_ref, bc_ref, dt_ref, dtb_ref, alog_ref, dsk_ref, ng_ref,
                yg_ref, h_ref, ht_ref, *, cl, d_inner, n_state):
    c = pl.program_id(1)
    n_pairs = d_inner // PAIR
    pairs_per_group = n_pairs // SSM_GROUPS
    gb = SSM_GROUPS * n_state

    @pl.when(c == 0)
    def _():
        ht_ref[...] = jnp.zeros_like(ht_ref)

    def chunk(rows):
        xc = xs_ref[rows, :]
        bcc = bc_ref[rows, :]

        dt = _softplus(dt_ref[rows, :] + dtb_ref[...])
        dta = dt * (-jnp.exp(alog_ref[...]))
        causal = (lax.broadcasted_iota(jnp.int32, (cl, cl), 0)
                  >= lax.broadcasted_iota(jnp.int32, (cl, cl), 1))
        tri = jnp.where(causal, 1.0, 0.0).astype(BF16)
        terms = []
        rest = dta
        for _ in range(3):
            terms.append(rest.astype(BF16))
            rest = rest - terms[-1].astype(F32)
        a_cum = jnp.dot(jnp.concatenate([tri] * 3, axis=1), jnp.concatenate(terms, axis=0),
                        preferred_element_type=F32)
        a_cum = a_cum * LOG2E
        a_cum_t = a_cum.T
        dt_t = dt.T
        dte_t = jnp.exp2(a_cum_t[:, cl - 1:cl] - a_cum_t) * dt_t
        cd_row = jnp.exp2(a_cum[cl - 1:cl, :])

        lo = lax.broadcasted_iota(jnp.int32, (1, PAIR), 1) < HEAD_DIM

        for g in range(SSM_GROUPS):
            b_f = bcc[:, g * n_state:(g + 1) * n_state]
            bt_g = b_f.T
            c_g = bcc[:, gb + g * n_state:gb + (g + 1) * n_state].astype(BF16)
            cb_g = _nt_dot(c_g, b_f.astype(BF16))
            gated = []
            ssq = jnp.zeros((cl, 1), F32)
            for jj in range(pairs_per_group):
                j = g * pairs_per_group + jj
                sl = slice(j * PAIR, (j + 1) * PAIR)
                x_p = xc[:, sl]
                x_half = (jnp.where(lo, x_p, 0.0).astype(BF16), jnp.where(lo, 0.0, x_p).astype(BF16))
                y = None
                st = None
                ea = []
                for half in range(2):
                    r = 2 * j + half
                    col = jnp.broadcast_to(a_cum[:, r:r + 1], (cl, cl))
                    decay = jnp.exp2(jnp.where(causal, col - a_cum_t[r:r + 1, :], -jnp.inf))
                    w = (cb_g * decay * dt_t[r:r + 1, :]).astype(BF16)
                    part = jnp.dot(w, x_half[half], preferred_element_type=F32)
                    y = part if y is None else y + part
                    part = jnp.dot((bt_g * dte_t[r:r + 1, :]).astype(BF16), x_half[half],
                                   preferred_element_type=F32)
                    st = part if st is None else st + part
                    ea.append(jnp.exp2(col))
                ht_p = ht_ref[:, sl]
                y = y + (jnp.dot(c_g, ht_p.astype(BF16), preferred_element_type=F32)
                         * jnp.where(lo, ea[0], ea[1]))
                ht_ref[:, sl] = ht_p * _lane_pair(cd_row, j, lo) + st
                y = y + dsk_ref[:, sl] * x_p
                gy = y * _silu(z_ref[rows, sl])
                ssq = ssq + jnp.sum(gy * gy, axis=-1, keepdims=True)
                gated.append(gy)
            scale = lax.rsqrt(ssq / (pairs_per_group * PAIR) + EPS)
            for jj in range(pairs_per_group):
                sl = slice((g * pairs_per_group + jj) * PAIR, (g * pairs_per_group + jj + 1) * PAIR)
                yg_ref[rows, sl] = (gated[jj] * scale * ng_ref[:, sl]).astype(BF16)

    chunk(slice(0, cl))

    @pl.when(c == pl.num_programs(1) - 1)
    def _():
        for j in range(n_pairs):
            sl = slice(j * PAIR, (j + 1) * PAIR)
            h_ref[0, sl, :] = ht_ref[:, sl].T


def _ssd_prompt(z, xbc, dt_raw, dt_bias, a_log, d_skip, norm_g, bsz, seq, d_inner, n_state):
    cl = SSD_CHUNK
    assert seq % cl == 0
    rows = cl
    nc = seq // rows
    gb = SSM_GROUPS * n_state
    assert d_inner % (2 * gb) == 0
    bc_blk = d_inner // (2 * gb)
    kern = functools.partial(_ssd_kernel, cl=cl, d_inner=d_inner, n_state=n_state)
    row = lambda b, c: b * nc + c
    const = lambda b, c: (0, 0)
    return pl.pallas_call(
        kern,
        grid=(bsz, nc),
        in_specs=[pl.BlockSpec((rows, d_inner), lambda b, c: (row(b, c), 0)),
                  pl.BlockSpec((rows, d_inner), lambda b, c: (row(b, c), 0)),
                  pl.BlockSpec((rows, 2 * gb), lambda b, c: (row(b, c), bc_blk)),
                  pl.BlockSpec((rows, LANES), lambda b, c: (row(b, c), 0)),
                  pl.BlockSpec((1, LANES), const),
                  pl.BlockSpec((1, LANES), const),
                  pl.BlockSpec((1, d_inner), const),
                  pl.BlockSpec((1, d_inner), const)],
        out_specs=[pl.BlockSpec((rows, d_inner), lambda b, c: (row(b, c), 0)),
                   pl.BlockSpec((1, d_inner, n_state), lambda b, c: (b, 0, 0))],
        out_shape=[jax.ShapeDtypeStruct((bsz * seq, d_inner), BF16),
                   jax.ShapeDtypeStruct((bsz, d_inner, n_state), F32)],
        scratch_shapes=[pltpu.VMEM((n_state, d_inner), F32)],
        compiler_params=_params("parallel", "arbitrary"),
        name="ssd_prompt",
    )(z, xbc, xbc, dt_raw, dt_bias, a_log, d_skip, norm_g)


def _ssd_step_kernel(zx_ref, dt_ref, cs_ref, h_ref, cw_ref, cb_ref, dtb_ref, alog_ref, dsk_ref, ng_ref,
                     yg_ref, cso_ref, ho_ref, *, d_inner, n_state):
    n_pairs = d_inner // PAIR
    pairs_per_group = n_pairs // SSM_GROUPS
    gb = SSM_GROUPS * n_state
    zx = zx_ref[0]
    z = zx[:, 0:d_inner]
    xbc = zx[:, d_inner:]
    prev = cs_ref[0]
    acc = cb_ref[...] + cw_ref[CONV_TAPS - 1:CONV_TAPS, :] * xbc
    for k in range(CONV_TAPS - 1):
        acc = acc + cw_ref[k:k + 1, :] * prev[k:k + 1, :]
    cso_ref[0, 0:CONV_TAPS - 2, :] = prev[1:CONV_TAPS - 1, :]
    cso_ref[0, CONV_TAPS - 2:CONV_TAPS - 1, :] = xbc
    act = jax.nn.silu(acc)
    xs = act[:, 0:d_inner]
    dt = _softplus(dt_ref[0] + dtb_ref[...])
    da = jnp.exp(dt * (-jnp.exp(alog_ref[...])))

    lo = lax.broadcasted_iota(jnp.int32, (1, PAIR), 1) < HEAD_DIM
    row_lo = lax.broadcasted_iota(jnp.int32, (PAIR, 1), 0) < HEAD_DIM
    row8 = lax.broadcasted_iota(jnp.int32, (HALO, 1), 0)

    for g in range(SSM_GROUPS):
        b_g = act[:, d_inner + g * n_state:d_inner + (g + 1) * n_state]
        c_g = act[:, d_inner + gb + g * n_state:d_inner + gb + (g + 1) * n_state]
        gated = []
        ssq = jnp.zeros((1, 1), F32)
        for jj in range(pairs_per_group):
            j = g * pairs_per_group + jj
            sl = slice(j * PAIR, (j + 1) * PAIR)
            x_p = xs[:, sl]
            xdt = x_p * _lane_pair(dt, j, lo)
            x_hi, x_lo = _split2(xdt)
            b_hi, b_lo = _split2(b_g)
            lhs = jnp.where(row8 == 0, x_hi, jnp.where(row8 == 1, x_lo, jnp.where(row8 == 2, x_hi,
                            jnp.where(row8 == 3, x_lo, 0.0)))).astype(BF16)
            rhs = jnp.where(row8 == 0, b_hi, jnp.where(row8 == 1, b_hi, jnp.where(row8 == 2, b_lo,
                            jnp.where(row8 == 3, b_lo, 0.0)))).astype(BF16)
            d_col = jnp.where(row_lo, da[:, 2 * j:2 * j + 1], da[:, 2 * j + 1:2 * j + 2])
            h_new = d_col * h_ref[0, sl, :] + _tn_dot(lhs, rhs)
            ho_ref[0, sl, :] = h_new
            c_hi, c_lo = _split2(c_g)
            c2 = jnp.where(row8 == 0, c_hi, jnp.where(row8 == 1, c_lo, 0.0)).astype(BF16)
            h_hi = h_new.astype(BF16)
            h_lo = (h_new - h_hi.astype(F32)).astype(BF16)
            yy = _nt_dot(c2, h_hi) + _nt_dot(c2, h_lo)
            y = yy[0:1, :] + yy[1:2, :]
            y = y + dsk_ref[:, sl] * x_p
            gy = y * jax.nn.silu(z[:, sl])
            ssq = ssq + jnp.sum(gy * gy, axis=-1, keepdims=True)
            gated.append(gy)
        scale = lax.rsqrt(ssq / (pairs_per_group * PAIR) + EPS)
        for jj in range(pairs_per_group):
            sl = slice((g * pairs_per_group + jj) * PAIR, (g * pairs_per_group + jj + 1) * PAIR)
            yg_ref[0, :, sl] = gated[jj] * scale * ng_ref[:, sl]


def _ssd_step(zx, dt_raw, state_conv, state_ssm, conv_w, conv_b, dt_bias, a_log, d_skip, norm_g,
              d_inner, n_state):
    bsz = zx.shape[0]
    conv_dim = conv_w.shape[1]
    kern = functools.partial(_ssd_step_kernel, d_inner=d_inner, n_state=n_state)
    const = lambda b: (0, 0)
    per_b = lambda b: (b, 0, 0)
    return pl.pallas_call(
        kern,
        grid=(bsz,),
        in_specs=[pl.BlockSpec((1, 1, d_inner + conv_dim), per_b),
                  pl.BlockSpec((1, 1, LANES), per_b),
                  pl.BlockSpec((1, CONV_TAPS - 1, conv_dim), per_b),
                  pl.BlockSpec((1, d_inner, n_state), per_b),
                  pl.BlockSpec((CONV_TAPS, conv_dim), const),
                  pl.BlockSpec((1, conv_dim), const),
                  pl.BlockSpec((1, LANES), const),
                  pl.BlockSpec((1, LANES), const),
                  pl.BlockSpec((1, d_inner), const),
                  pl.BlockSpec((1, d_inner), const)],
        out_specs=[pl.BlockSpec((1, 1, d_inner), per_b),
                   pl.BlockSpec((1, CONV_TAPS - 1, conv_dim), per_b),
                   pl.BlockSpec((1, d_inner, n_state), per_b)],
        out_shape=[jax.ShapeDtypeStruct((bsz, 1, d_inner), F32),
                   jax.ShapeDtypeStruct((bsz, CONV_TAPS - 1, conv_dim), F32),
                   jax.ShapeDtypeStruct((bsz, d_inner, n_state), F32)],
        compiler_params=_params("parallel"),
        name="ssd_step",
    )(zx.reshape(bsz, 1, -1), dt_raw.reshape(bsz, 1, LANES), state_conv,
      state_ssm.reshape(bsz, d_inner, n_state), conv_w, conv_b, dt_bias, a_log, d_skip, norm_g)


def _linear_res_kernel(a_ref, w_ref, r_ref, o_ref):
    o_ref[...] = r_ref[...] + jnp.dot(a_ref[...].astype(BF16), w_ref[...], preferred_element_type=F32)


def _linear_res(a, w, res):
    m, k = a.shape
    n = w.shape[1]
    tm = _row_tile(m, 512)
    return pl.pallas_call(
        _linear_res_kernel,
        grid=(m // tm,),
        in_specs=[pl.BlockSpec((tm, k), lambda i: (i, 0)),
                  pl.BlockSpec((k, n), lambda i: (0, 0)),
                  pl.BlockSpec((tm, n), lambda i: (i, 0))],
        out_specs=pl.BlockSpec((tm, n), lambda i: (i, 0)),
        out_shape=jax.ShapeDtypeStruct((m, n), F32),
        compiler_params=_params("parallel"),
        name="linear_res",
    )(a, w, res)


def _ffn_kernel(x_ref, g_ref, wg_ref, wu_ref, wd_ref, o_ref, xn_ref, *, tf):
    @pl.when(pl.program_id(1) == 0)
    def _():
        x = x_ref[...]
        xn_ref[...] = (_rms_scale(x) * g_ref[...]).astype(BF16)
        o_ref[...] = x

    xn = xn_ref[...]
    acts = []
    for c0 in range(0, tf, MXU_WIDTH):
        cols = slice(c0, min(c0 + MXU_WIDTH, tf))
        gate = jnp.dot(xn, wg_ref[:, cols], preferred_element_type=F32)
        up = jnp.dot(xn, wu_ref[:, cols], preferred_element_type=F32)
        acts.append((jax.nn.silu(gate) * up).astype(BF16))
    o_ref[...] += jnp.dot(jnp.concatenate(acts, axis=1), wd_ref[...], preferred_element_type=F32)


def _ffn(x, g, w_gate_up, w_down):
    m, d = x.shape
    d_ff = w_down.shape[0]
    tm = _row_tile(m, 1024)
    tf = d_ff // 2 if d_ff % (2 * LANES) == 0 else d_ff
    nf = d_ff // tf
    return pl.pallas_call(
        functools.partial(_ffn_kernel, tf=tf),
        grid=(m // tm, nf),
        in_specs=[pl.BlockSpec((tm, d), lambda i, j: (i, 0)),
                  pl.BlockSpec((1, d), lambda i, j: (0, 0)),
                  pl.BlockSpec((d, tf), lambda i, j: (0, j)),
                  pl.BlockSpec((d, tf), lambda i, j: (0, nf + j)),
                  pl.BlockSpec((tf, d), lambda i, j: (j, 0))],
        out_specs=pl.BlockSpec((tm, d), lambda i, j: (i, 0)),
        out_shape=jax.ShapeDtypeStruct((m, d), F32),
        scratch_shapes=[pltpu.VMEM((tm, d), BF16)],
        compiler_params=_params("parallel", "arbitrary"),
        name="ffn",
    )(x, g, w_gate_up, w_gate_up, w_down)


def _qkv_kernel(x_ref, gq_ref, gkv_ref, w_ref, qn_ref, kn_ref, knc_ref, q_ref, k_ref, v4_ref, vd_ref,
                *, n_heads, k_transposed):
    lo = lax.broadcasted_iota(jnp.int32, (1, PAIR), 1) < HEAD_DIM

    def qk_norm(y, gain_ref, out_ref, post):
        for h in range(n_heads):
            yh = y[:, h * PAIR:(h + 1) * PAIR]
            sq = yh * yh
            s_lo = jnp.sum(jnp.where(lo, sq, 0.0), axis=-1, keepdims=True)
            s_hi = jnp.sum(jnp.where(lo, 0.0, sq), axis=-1, keepdims=True)
            r = jnp.where(lo, lax.rsqrt(s_lo / HEAD_DIM + EPS), lax.rsqrt(s_hi / HEAD_DIM + EPS))
            out_ref[:, h * PAIR:(h + 1) * PAIR] = (yh * r * gain_ref[...] * post).astype(out_ref.dtype)

    xh = _rms_scale(x_ref[...])
    xq = (xh * gq_ref[...]).astype(BF16)
    xkv = (xh * gkv_ref[...]).astype(BF16)
    qk_norm(jnp.dot(xq, w_ref[0], preferred_element_type=F32), qn_ref, q_ref, HEAD_DIM ** -0.5 * LOG2E)

    if k_transposed:
        kt = _nt_dot(w_ref[1], xkv)
        for g in range(2 * n_heads):
            rows = slice(g * HEAD_DIM, (g + 1) * HEAD_DIM)
            blk = kt[rows, :]
            r = lax.rsqrt(jnp.mean(blk * blk, axis=0, keepdims=True) + EPS)
            k_ref[0, rows, :] = blk * r * knc_ref[...]
    else:
        qk_norm(jnp.dot(xkv, w_ref[1], preferred_element_type=F32), kn_ref, k_ref, 1.0)

    v = jnp.dot(xkv, w_ref[2], preferred_element_type=F32)
    vd_ref[...] = v.astype(BF16)
    for h in range(n_heads):
        v4_ref[:, h, :] = v[:, h * PAIR:(h + 1) * PAIR]


def _qkv(x, g_q, g_kv, w_q, w_k, w_v, q_norm, k_norm, seq=None):
    m, d = x.shape
    n_heads = d // PAIR
    k_transposed = seq is not None
    tm = _row_tile(m if seq is None else seq, 512)
    w = jnp.stack([w_q, w_k.T if k_transposed else w_k, w_v]).astype(BF16)
    pair_gain = jnp.tile(k_norm.astype(F32), 2).reshape(1, PAIR)
    kern = functools.partial(_qkv_kernel, n_heads=n_heads, k_transposed=k_transposed)
    const = lambda i: (0, 0)
    rowblk = lambda i: (i, 0)
    if k_transposed:
        per_seq = seq // tm
        k_spec = pl.BlockSpec((1, d, tm), lambda i: (i // per_seq, 0, i % per_seq))
        k_shape = jax.ShapeDtypeStruct((m // seq, d, seq), F32)
    else:
        k_spec = pl.BlockSpec((tm, d), rowblk)
        k_shape = jax.ShapeDtypeStruct((m, d), F32)
    return pl.pallas_call(
        kern,
        grid=(m // tm,),
        in_specs=[pl.BlockSpec((tm, d), rowblk),
                  pl.BlockSpec((1, d), const),
                  pl.BlockSpec((1, d), const),
                  pl.BlockSpec((3, d, d), lambda i: (0, 0, 0)),
                  pl.BlockSpec((1, PAIR), const),
                  pl.BlockSpec((1, PAIR), const),
                  pl.BlockSpec((HEAD_DIM, 1), const)],
        out_specs=[pl.BlockSpec((tm, d), rowblk), k_spec,
                   pl.BlockSpec((tm, n_heads, PAIR), lambda i: (i, 0, 0)),
                   pl.BlockSpec((tm, d), rowblk)],
        out_shape=[jax.ShapeDtypeStruct((m, d), BF16), k_shape,
                   jax.ShapeDtypeStruct((m, n_heads, PAIR), F32),
                   jax.ShapeDtypeStruct((m, d), BF16)],
        compiler_params=_params("parallel"),
        name="qkv_proj",
    )(x, g_q, g_kv, w, jnp.tile(q_norm.astype(F32), 2).reshape(1, PAIR), pair_gain,
      k_norm.astype(F32).reshape(HEAD_DIM, 1))


def _bias_tiles_kernel(rb_ref, o_ref, *, t):
    h = pl.program_id(0)
    i = lax.broadcasted_iota(jnp.int32, (t, t), 0)
    j = lax.broadcasted_iota(jnp.int32, (t, t), 1)
    for off in range(2):
        n = i - j + off * t
        bucket = _bucket(n)
        tile = jnp.zeros((t, t), F32)
        for b in range(N_BUCKETS):
            tile = jnp.where(bucket == b, rb_ref[b, h], tile)
        o_ref[0, off] = jnp.where(n >= 0, (tile - rb_ref[N_BUCKETS - 1, h]) * LOG2E, -jnp.inf)


def _bias_tiles(rel_bias, t):
    n_heads = rel_bias.shape[1]
    return pl.pallas_call(
        functools.partial(_bias_tiles_kernel, t=t),
        grid=(n_heads,),
        in_specs=[pl.BlockSpec(memory_space=pltpu.SMEM)],
        out_specs=pl.BlockSpec((1, 2, t, t), lambda h: (h, 0, 0, 0)),
        out_shape=jax.ShapeDtypeStruct((n_heads, 2, t, t), F32),
        compiler_params=_params("parallel"),
        name="bias_tiles",
    )(rel_bias)


def _lambda_value(lam_ref, lambda_init):
    s1 = jnp.sum(lam_ref[0:1, :] * lam_ref[1:2, :], axis=-1, keepdims=True)
    s2 = jnp.sum(lam_ref[2:3, :] * lam_ref[3:4, :], axis=-1, keepdims=True)
    return jnp.exp(s1) - jnp.exp(s2) + lambda_init


def _attn_kernel(q_ref, k_ref, v_ref, bias_ref, rb_ref, lam_ref, sg_ref, o_ref,
                 kb_ref, kb2_ref, q12_ref, m_ref, l_ref, acc_ref, *, t, rc_far, rc_near, nq, lambda_init):
    h = pl.program_id(1)
    qi = pl.program_id(2)

    @pl.when(qi == 0)
    def _():
        for ki in range(nq):
            kb_ref[ki] = k_ref[0, :, ki * t:(ki + 1) * t].astype(BF16)
        for kp in range(nq // 2):
            kb2_ref[kp] = k_ref[0, :, kp * 2 * t:(kp + 1) * 2 * t].astype(BF16)

    q = q_ref[0]
    lo = lax.broadcasted_iota(jnp.int32, (t, PAIR), 1) < HEAD_DIM
    zero = jnp.zeros_like(q)
    q12_ref[0:t, :] = jnp.where(lo, q, zero)
    q12_ref[t:2 * t, :] = jnp.where(lo, zero, q)

    m_ref[...] = jnp.full(m_ref.shape, -jnp.inf, F32)
    l_ref[...] = jnp.zeros(l_ref.shape, F32)
    acc_ref[...] = jnp.zeros(acc_ref.shape, F32)

    far_bias = rb_ref[N_BUCKETS - 1, h] * LOG2E

    def block(rc, keys_of, base, cols_of, near_of=None):
        for c in range(2 * t // rc):
            rows = slice(c * rc, (c + 1) * rc)
            r0 = (c * rc) % t
            ncols = cols_of(r0)
            s = jnp.dot(q12_ref[rows, :], keys_of(ncols), preferred_element_type=F32)
            tiles = []
            for k in range(ncols // LANES):
                sk = s[:, k * LANES:(k + 1) * LANES]
                near = None if near_of is None else near_of(k)
                if near is not None and near[0] * t + r0 - (near[1] * LANES + LANES - 1) < MAX_DISTANCE:
                    slab = []
                    for a in range(rc // LANES):
                        rel = (near[0] * t + r0) // LANES + a - near[1]
                        if rel < 0:
                            slab.append(jnp.full((LANES, LANES), -jnp.inf, F32))
                        elif rel <= 1:
                            slab.append(bias_ref[0, rel])
                        else:
                            slab.append(jnp.zeros((LANES, LANES), F32))
                    sk = sk + jnp.concatenate(slab, axis=0)
                tiles.append(sk)
            m_prev = m_ref[rows, :]
            part = tiles[0]
            for sk in tiles[1:]:
                part = jnp.maximum(part, sk)
            m_new = jnp.maximum(m_prev, jnp.max(part, axis=-1, keepdims=True) + far_bias)
            alpha = jnp.exp2(m_prev - m_new)
            m_sub = m_new - far_bias
            ps = [jnp.exp2(sk - m_sub) for sk in tiles]
            psum = ps[0]
            for pk in ps[1:]:
                psum = psum + pk
            l_ref[rows, :] = alpha * l_ref[rows, :] + psum
            pv = jnp.dot(jnp.concatenate(ps, axis=1).astype(BF16), v_ref[0, pl.ds(base, ncols), :],
                         preferred_element_type=F32)
            acc_ref[rows, :] = alpha * acc_ref[rows, :] + pv
            m_ref[rows, :] = m_new

    n_far = jnp.maximum(qi - 1, 0)

    def far_pair(kp, carry):
        block(rc_far, lambda n: kb2_ref[kp, :, 0:n], pl.multiple_of(kp * 2 * t, 2 * t), lambda r0: 2 * t)
        return carry

    lax.fori_loop(0, n_far // 2, far_pair, 0)

    @pl.when(n_far % 2 == 1)
    def _():
        ki = n_far - 1
        block(rc_far, lambda n: kb_ref[ki, :, 0:n], pl.multiple_of(ki * t, t), lambda r0: t)

    nlt = t // LANES

    @pl.when(qi >= 1)
    def _():
        ki = qi - 1
        block(rc_near, lambda n: jnp.concatenate([kb_ref[ki], kb_ref[qi, :, 0:n - t]], axis=1),
              pl.multiple_of(ki * t, t), lambda r0: t + r0 + rc_near,
              near_of=lambda k: (1, k) if k < nlt else (0, k - nlt))

    @pl.when(qi == 0)
    def _():
        block(rc_near, lambda n: kb_ref[0, :, 0:n], 0, lambda r0: r0 + rc_near, near_of=lambda k: (0, k))

    l = jnp.sum(l_ref[...], axis=-1, keepdims=True)
    o1 = acc_ref[0:t, :] / l[0:t, :]
    o2 = acc_ref[t:2 * t, :] / l[t:2 * t, :]
    o = o1 - _lambda_value(lam_ref, lambda_init) * o2
    o_ref[0] = (_rms_scale(o) * sg_ref[...] * (1.0 - lambda_init)).astype(o_ref.dtype)


def _prompt_attention(q, k_t, v, rel_bias, lam_rows, subln, lambda_init, bsz, seq):
    d = q.shape[-1]
    n_heads = d // PAIR
    t = min(seq, 512)
    assert seq % t == 0 and t >= MAX_DISTANCE
    nq = seq // t
    assert MAX_DISTANCE <= LANES
    tiles = _bias_tiles(rel_bias, LANES)
    rc_far, rc_near = t, t
    assert t % rc_near == 0 and rc_near % LANES == 0
    kern = functools.partial(_attn_kernel, t=t, rc_far=rc_far, rc_near=rc_near, nq=nq, lambda_init=lambda_init)
    const = lambda b, h, i: (0, 0)
    return pl.pallas_call(
        kern,
        grid=(bsz, n_heads, nq),
        in_specs=[pl.BlockSpec((1, t, PAIR), lambda b, h, i: (b, i, h)),
                  pl.BlockSpec((1, PAIR, seq), lambda b, h, i: (b, h, 0)),
                  pl.BlockSpec((1, seq, PAIR), lambda b, h, i: (b, 0, h)),
                  pl.BlockSpec((1, 2, LANES, LANES), lambda b, h, i: (h, 0, 0, 0)),
                  pl.BlockSpec(memory_space=pltpu.SMEM),
                  pl.BlockSpec((4, LANES), const),
                  pl.BlockSpec((1, PAIR), const)],
        out_specs=pl.BlockSpec((1, t, PAIR), lambda b, h, i: (b, i, h)),
        out_shape=jax.ShapeDtypeStruct((bsz, seq, d), BF16),
        scratch_shapes=[pltpu.VMEM((nq, PAIR, t), BF16), pltpu.VMEM((max(nq // 2, 1), PAIR, 2 * t), BF16),
                        pltpu.VMEM((2 * t, PAIR), BF16),
                        pltpu.VMEM((2 * t, LANES), F32), pltpu.VMEM((2 * t, LANES), F32),
                        pltpu.VMEM((2 * t, PAIR), F32)],
        compiler_params=_params("parallel", "parallel", "arbitrary"),
        name="prompt_attention",
    )(q.reshape(bsz, seq, d), k_t, v.reshape(bsz, seq, d), tiles, rel_bias, lam_rows, subln)


def _sample_attn_kernel(pt_ref, q_ref, kn_ref, vn_ref, *rest, pages_per_step, past, n_heads, lambda_init):
    k_refs = rest[0:pages_per_step]
    v_refs = rest[pages_per_step:2 * pages_per_step]
    rb_ref, lam_ref, sg_ref, o_ref, qb_ref, bias_ref, m_ref, l_ref, acc_ref = rest[2 * pages_per_step:]
    step_id = pl.program_id(1)
    last = pl.num_programs(1) - 1
    nrow = 2 * n_heads
    d = n_heads * PAIR
    keys = pages_per_step * PAGE_SIZE

    @pl.when(step_id == 0)
    def _():
        bias_ref[...] = jnp.broadcast_to(rb_ref[:, N_BUCKETS - 1:N_BUCKETS] * LOG2E, (nrow, keys))

    @pl.when(step_id == last)
    def _():
        key = lax.broadcasted_iota(jnp.int32, (nrow, keys), 1)
        bucket = _bucket(past - (step_id * keys + key))
        bias = jnp.zeros((nrow, keys), F32)
        for b in range(N_BUCKETS):
            bias = jnp.where(bucket == b, rb_ref[:, b:b + 1], bias)
        bias_ref[...] = bias * LOG2E

    @pl.when(step_id == 0)
    def _():
        q = q_ref[0].astype(F32)
        row = lax.broadcasted_iota(jnp.int32, (nrow, d), 0)
        col = lax.broadcasted_iota(jnp.int32, (nrow, d), 1)
        qblk = jnp.where(jnp.right_shift(col, 6) == row, jnp.broadcast_to(q, (nrow, d)), 0.0)
        qb_ref[...] = qblk.astype(BF16)
        m_ref[...] = jnp.sum(qblk * kn_ref[0], axis=-1, keepdims=True) + rb_ref[:, 0:1] * LOG2E
        l_ref[...] = jnp.ones(l_ref.shape, F32)
        vn = vn_ref[0].astype(F32)
        for h in range(n_heads):
            acc_ref[2 * h:2 * h + 2, :] = jnp.broadcast_to(vn[:, h * PAIR:(h + 1) * PAIR], (2, PAIR))

    qb = qb_ref[...]
    s = jnp.concatenate([jnp.dot(qb, k_refs[g][0].astype(BF16), preferred_element_type=F32)
                         for g in range(pages_per_step)], axis=1) + bias_ref[...]
    m_prev = m_ref[...]
    m_new = jnp.maximum(m_prev, jnp.max(s, axis=-1, keepdims=True))
    alpha = jnp.exp2(m_prev - m_new)
    p = jnp.exp2(s - m_new)
    l_ref[...] = alpha * l_ref[...] + jnp.sum(p, axis=-1, keepdims=True)
    pb = p.astype(BF16)
    row_head = jnp.right_shift(lax.broadcasted_iota(jnp.int32, (nrow, PAIR), 0), 1)
    pv = jnp.zeros((nrow, PAIR), F32)
    for h in range(n_heads):
        v_h = jnp.concatenate([v_refs[g][0, pl.ds(h, PAGE_SIZE, stride=n_heads), :].astype(BF16)
                               for g in range(pages_per_step)], axis=0)
        pv = jnp.where(row_head == h, jnp.dot(pb, v_h, preferred_element_type=F32), pv)
    acc_ref[...] = alpha * acc_ref[...] + pv
    m_ref[...] = m_new

    @pl.when(step_id == last)
    def _():
        rowc = lax.broadcasted_iota(jnp.int32, (nrow, 1), 0)
        lam = _lambda_value(lam_ref, lambda_init)
        coef = jnp.where(jnp.bitwise_and(rowc, 1) == 0, 1.0, -lam) / l_ref[...]
        scaled = acc_ref[...] * coef
        o = scaled + pltpu.roll(scaled, nrow - 1, 0)
        on = _rms_scale(o) * sg_ref[...] * (1.0 - lambda_init)
        for h in range(n_heads):
            o_ref[0, :, h * PAIR:(h + 1) * PAIR] = on[2 * h:2 * h + 1, :]


def _sample_attention(q, k_new, v_new, cache_k, cache_v, page_table, rel_rows, lam_rows, subln, lambda_init):
    bsz, d = q.shape
    n_heads = d // PAIR
    n_pages = page_table.shape[1]
    past = n_pages * PAGE_SIZE
    pages_per_step = max(g for g in (16, 8, 4, 2, 1) if n_pages % g == 0)
    n_pool = cache_k.shape[0]
    kern = functools.partial(_sample_attn_kernel, pages_per_step=pages_per_step, past=past,
                             n_heads=n_heads, lambda_init=lambda_init)
    per_b = lambda b, s, pt: (b, 0, 0)
    const = lambda b, s, pt: (0, 0)

    def k_spec(g):
        return pl.BlockSpec((1, d, PAGE_SIZE), lambda b, s, pt: (pt[b, s * pages_per_step + g], 0, 0))

    def v_spec(g):
        return pl.BlockSpec((1, PAGE_SIZE * n_heads, PAIR),
                            lambda b, s, pt: (pt[b, s * pages_per_step + g], 0, 0))

    grid_spec = pltpu.PrefetchScalarGridSpec(
        num_scalar_prefetch=1,
        grid=(bsz, n_pages // pages_per_step),
        in_specs=([pl.BlockSpec((1, 1, d), per_b)] * 3
                  + [k_spec(g) for g in range(pages_per_step)]
                  + [v_spec(g) for g in range(pages_per_step)]
                  + [pl.BlockSpec((2 * n_heads, N_BUCKETS), const),
                     pl.BlockSpec((4, LANES), const),
                     pl.BlockSpec((1, PAIR), const)]),
        out_specs=pl.BlockSpec((1, 1, d), per_b),
        scratch_shapes=[pltpu.VMEM((2 * n_heads, d), BF16),
                        pltpu.VMEM((2 * n_heads, pages_per_step * PAGE_SIZE), F32),
                        pltpu.VMEM((2 * n_heads, 1), F32), pltpu.VMEM((2 * n_heads, 1), F32),
                        pltpu.VMEM((2 * n_heads, PAIR), F32)],
    )
    ck = jnp.transpose(cache_k, (0, 2, 3, 4, 1)).reshape(n_pool, d, PAGE_SIZE)
    cv = cache_v.reshape(n_pool, PAGE_SIZE * n_heads, PAIR)
    return pl.pallas_call(
        kern,
        grid_spec=grid_spec,
        out_shape=jax.ShapeDtypeStruct((bsz, 1, d), F32),
        compiler_params=_params("parallel", "arbitrary"),
        name="sample_attention",
    )(page_table, q.reshape(bsz, 1, d), k_new.reshape(bsz, 1, d), v_new.reshape(bsz, 1, d),
      *([ck] * pages_per_step), *([cv] * pages_per_step), rel_rows, lam_rows, subln).reshape(bsz, d)


def _pad_lanes(v):
    return jnp.pad(v.astype(F32), (0, LANES - v.shape[0])).reshape(1, LANES)


def kernel(x_prompt, x_sample, state_conv, state_ssm, cache_k, cache_v, page_table, norm_mix, norm_ffn, w_in, conv_w, conv_b, dt_bias, a_log, d_skip, ssm_norm, w_out_ssm, norm_kv, w_kv, k_norm, w_q, q_norm, lambda_q1, lambda_k1, lambda_q2, lambda_k2, subln, w_o, rel_bias, w_gate_up, w_down):
    bsz, seq, d = x_prompt.shape
    dec = x_sample.shape[0]
    assert x_sample.shape[1] == 1
    n_ssm_layers, ssm_heads = dt_bias.shape
    depth = norm_mix.shape[0]
    assert n_ssm_layers == 1 and depth == 2
    d_inner = w_out_ssm.shape[1]
    n_state = state_ssm.shape[-1]
    assert d_inner == ssm_heads * HEAD_DIM and n_state == LANES and ssm_heads <= LANES
    n_heads = d // PAIR

    xp = x_prompt.reshape(bsz * seq, d)
    xs = x_sample.reshape(dec, d)
    row = lambda v: v.astype(F32).reshape(1, -1)

    zx_cols = 2 * d_inner + 2 * SSM_GROUPS * n_state
    w_z = w_in[0][:, :d_inner].astype(BF16)
    w_c = w_in[0][:, d_inner:zx_cols].astype(BF16)
    w_dt = jnp.pad(w_in[0][:, zx_cols:], ((0, 0), (0, LANES - ssm_heads))).astype(BF16)
    g_mix0 = row(norm_mix[0])
    cw0, cb0 = conv_w[0], row(conv_b[0])
    ssd_args = (_pad_lanes(dt_bias[0]), _pad_lanes(a_log[0]),
                row(jnp.repeat(d_skip[0], HEAD_DIM)), row(ssm_norm[0]))
    w_out = w_out_ssm[0].astype(BF16)
    w_gu0, w_dn0 = w_gate_up[0].astype(BF16), w_down[0].astype(BF16)

    z_p, xbc_p, dt_p, conv_p = _in_proj(xp, g_mix0, w_z, w_c, w_dt, cw0, cb0, seq=seq)
    yg_p, ssm_p = _ssd_prompt(z_p, xbc_p, dt_p, *ssd_args, bsz, seq, d_inner, n_state)
    xp = _linear_res(yg_p, w_out, xp)
    xp = _ffn(xp, row(norm_ffn[0]), w_gu0, w_dn0)

    z_s, xbc_s, dt_s = _in_proj(xs, g_mix0, w_z, w_c, w_dt)
    zx_s = jnp.concatenate([z_s, xbc_s], axis=1)
    yg_s, conv_s, ssm_s = _ssd_step(zx_s, dt_s, state_conv[0], state_ssm[0], cw0, cb0, *ssd_args,
                                    d_inner, n_state)
    xs = _linear_res(yg_s.reshape(dec, d_inner), w_out, xs)
    xs = _ffn(xs, row(norm_ffn[0]), w_gu0, w_dn0)

    lambda_init = 0.8 - 0.6 * math.exp(-0.3 * 1)
    qkv_args = (row(norm_mix[1]), row(norm_kv), w_q[0], w_kv[:, :d], w_kv[:, d:], q_norm[0], k_norm)
    lam_rows = jnp.pad(jnp.stack([lambda_q1[0], lambda_k1[0], lambda_q2[0], lambda_k2[0]]).astype(F32),
                       ((0, 0), (0, LANES - HEAD_DIM)))
    sg = row(subln[0])
    w_oo = w_o[0].astype(BF16)
    w_gu1, w_dn1 = w_gate_up[1].astype(BF16), w_down[1].astype(BF16)

    q_p, kt_p, v_p, vd_p = _qkv(xp, *qkv_args, seq=seq)
    o_p = _prompt_attention(q_p, kt_p, vd_p, rel_bias.astype(F32), lam_rows, sg, lambda_init, bsz, seq)
    xp = _linear_res(o_p.reshape(bsz * seq, d), w_oo, xp)
    xp = _ffn(xp, row(norm_ffn[1]), w_gu1, w_dn1)

    q_s, k_s, v_s, vd_s = _qkv(xs, *qkv_args)
    rel_rows = jnp.repeat(rel_bias.astype(F32).T, 2, axis=0)
    o_s = _sample_attention(q_s.astype(F32), k_s, vd_s.astype(F32), cache_k, cache_v, page_table,
                            rel_rows, lam_rows, sg, lambda_init)
    xs = _linear_res(o_s, w_oo, xs)
    xs = _ffn(xs, row(norm_ffn[1]), w_gu1, w_dn1)

    k_p = jnp.transpose(kt_p.reshape(bsz, n_heads, 2, HEAD_DIM, seq), (0, 4, 1, 2, 3))
    return (xp.reshape(bsz, seq, d), xs.reshape(dec, 1, d),
            conv_p[None], ssm_p.reshape(1, bsz, ssm_heads, HEAD_DIM, n_state),
            k_p, v_p.reshape(bsz, seq, n_heads, PAIR),
            conv_s[None], ssm_s.reshape(1, dec, ssm_heads, HEAD_DIM, n_state),
            k_s.reshape(dec, 1, n_heads, 2, HEAD_DIM), v_s.reshape(dec, 1, n_heads, PAIR))
```

```python
import functools
import math

import jax
import jax.numpy as jnp
from jax import lax
from jax.experimental import pallas as pl
from jax.experimental.pallas import tpu as pltpu

F32 = jnp.float32
BF16 = jnp.bfloat16

EPS = 1e-6
LANES = 128
HEAD_DIM = 64
PAIR = 2 * HEAD_DIM
SSD_CHUNK = 128
SSM_GROUPS = 4
CONV_TAPS = 4
HALO = 8
N_BUCKETS = 32
MAX_EXACT = N_BUCKETS // 2
MAX_DISTANCE = 128
PAGE_SIZE = 128
LOG2E = math.log2(math.e)
MXU_WIDTH = 256
VMEM_LIMIT = 52 * 1024 * 1024


def _params(*semantics):
    return pltpu.CompilerParams(dimension_semantics=semantics, vmem_limit_bytes=VMEM_LIMIT)


def _row_tile(m, preferred):
    if m <= preferred:
        return m
    t = preferred
    while m % t or t % 16:
        t -= 1
    return t


def _nt_dot(a, b):
    return lax.dot_general(a, b, (((1,), (1,)), ((), ())), preferred_element_type=F32)


def _tn_dot(a, b):
    return lax.dot_general(a, b, (((0,), (0,)), ((), ())), preferred_element_type=F32)


def _rms_scale(x):
    return x * lax.rsqrt(jnp.mean(x * x, axis=-1, keepdims=True) + EPS)


def _split2(x):
    hi = x.astype(BF16).astype(F32)
    lo = (x - hi).astype(BF16).astype(F32)
    return hi, lo


def _silu(x):
    h = 0.5 * x
    return h + h * jnp.tanh(h)


def _log1p(u):
    w = 1.0 + u
    return jnp.where(w == 1.0, u, jnp.log(w) * (u / (w - 1.0)))


def _softplus(x):
    return jnp.maximum(x, 0.0) + _log1p(jnp.exp(-jnp.abs(x)))


def _lane_pair(arr, j, lo):
    return jnp.where(lo, arr[:, 2 * j:2 * j + 1], arr[:, 2 * j + 1:2 * j + 2])


def _bucket(n):
    n = jnp.maximum(n, 0)
    nf = jnp.maximum(n, 1).astype(F32)
    large = MAX_EXACT + (jnp.log(nf / MAX_EXACT) / math.log(MAX_DISTANCE / MAX_EXACT)
                         * (N_BUCKETS - MAX_EXACT)).astype(jnp.int32)
    large = jnp.minimum(large, N_BUCKETS - 1)
    return jnp.where(n < MAX_EXACT, n, large)


def _in_proj_kernel(*refs, tz, tc, tiles_per_seq):
    conv = tiles_per_seq is not None
    if conv:
        (x_ref, g_ref, wz_ref, wc_ref, wdt_ref, cw_ref, cb_ref,
         z_ref, xc_ref, dt_ref, ct_ref, xn_ref, tail_ref) = refs
    else:
        x_ref, g_ref, wz_ref, wc_ref, wdt_ref, z_ref, xc_ref, dt_ref, xn_ref = refs
    i = pl.program_id(0)
    j = pl.program_id(1)
    tm = x_ref.shape[0]
    step = 2 * MXU_WIDTH

    @pl.when(j == 0)
    def _():
        xn_ref[...] = (_rms_scale(x_ref[...]) * g_ref[...]).astype(BF16)
        dt_ref[...] = jnp.dot(xn_ref[...], wdt_ref[...], preferred_element_type=F32)

    xn = xn_ref[...]
    c_slices = [slice(c0, min(c0 + step, tc)) for c0 in range(0, tc, step)]
    z_slices = [slice(c0, min(c0 + step, tz)) for c0 in range(0, tz, step)]
    if not conv:
        for cols in c_slices:
            xc_ref[:, cols] = jnp.dot(xn, wc_ref[:, cols], preferred_element_type=F32)
        for cols in z_slices:
            z_ref[:, cols] = jnp.dot(xn, wz_ref[:, cols], preferred_element_type=F32)
        return

    seq_start = (i % tiles_per_seq) == 0
    row8 = lax.broadcasted_iota(jnp.int32, (HALO, 1), 0)
    rchunk = min(tm, LANES)
    n_chunks = tm // rchunk
    for cols in c_slices:
        tail = jnp.where(seq_start, 0.0, tail_ref[j, :, cols])
        for rb in range(n_chunks):
            rows = slice(rb * rchunk, (rb + 1) * rchunk)
            raw = jnp.dot(xn_ref[rows, :], wc_ref[:, cols], preferred_element_type=F32)
            acc = cb_ref[:, cols] + cw_ref[CONV_TAPS - 1:CONV_TAPS, cols] * raw
            for back in range(1, CONV_TAPS):
                shifted = pltpu.roll(raw, back, 0)
                head = jnp.where(row8 < back, pltpu.roll(tail, back, 0), shifted[0:HALO, :])
                shifted = jnp.concatenate([head, shifted[HALO:, :]], axis=0)
                k = CONV_TAPS - 1 - back
                acc = acc + cw_ref[k:k + 1, cols] * shifted
            xc_ref[rows, cols] = _silu(acc)
            tail = raw[rchunk - HALO:rchunk, :]
        tail_ref[j, :, cols] = tail
        ct_ref[0, :, cols] = tail[HALO - (CONV_TAPS - 1):HALO, :]
        if z_slices:
            zc = z_slices.pop(0)
            z_ref[:, zc] = jnp.dot(xn, wz_ref[:, zc], preferred_element_type=F32)
    for zc in z_slices:
        z_ref[:, zc] = jnp.dot(xn, wz_ref[:, zc], preferred_element_type=F32)


def _in_proj(x, g, w_z, w_c, w_dt, conv_w=None, conv_b=None, seq=None):
    m, d = x.shape
    nz, nc = w_z.shape[1], w_c.shape[1]
    conv = seq is not None
    tm = _row_tile(m if seq is None else seq, 1024)
    assert nz % (2 * LANES) == 0 and nc % (2 * LANES) == 0
    tz, tc = nz // 2, nc // 2
    rowblk = lambda i, j: (i, j)
    colblk = lambda i, j: (0, j)
    in_specs = [pl.BlockSpec((tm, d), lambda i, j: (i, 0)),
                pl.BlockSpec((1, d), lambda i, j: (0, 0)),
                pl.BlockSpec((d, tz), colblk),
                pl.BlockSpec((d, tc), colblk),
                pl.BlockSpec((d, LANES), lambda i, j: (0, 0))]
    out_specs = [pl.BlockSpec((tm, tz), rowblk), pl.BlockSpec((tm, tc), rowblk),
                 pl.BlockSpec((tm, LANES), lambda i, j: (i, 0))]
    out_shape = [jax.ShapeDtypeStruct((m, nz), F32), jax.ShapeDtypeStruct((m, nc), F32),
                 jax.ShapeDtypeStruct((m, LANES), F32)]
    scratch = [pltpu.VMEM((tm, d), BF16)]
    args = [x, g, w_z, w_c, w_dt]
    tiles_per_seq = None
    if conv:
        tiles_per_seq = seq // tm
        in_specs += [pl.BlockSpec((CONV_TAPS, tc), colblk), pl.BlockSpec((1, tc), colblk)]
        out_specs.append(pl.BlockSpec((1, CONV_TAPS - 1, tc), lambda i, j: (i, 0, j)))
        out_shape.append(jax.ShapeDtypeStruct((m // tm, CONV_TAPS - 1, nc), F32))
        scratch.append(pltpu.VMEM((2, HALO, tc), F32))
        args += [conv_w, conv_b]
    outs = pl.pallas_call(
        functools.partial(_in_proj_kernel, tz=tz, tc=tc, tiles_per_seq=tiles_per_seq),
        grid=(m // tm, 2),
        in_specs=in_specs,
        out_specs=out_specs,
        out_shape=out_shape,
        scratch_shapes=scratch,
        compiler_params=_params("arbitrary", "arbitrary"),
        name="in_proj",
    )(*args)
    outs = list(outs)
    if conv:
        outs[3] = outs[3][tiles_per_seq - 1::tiles_per_seq]
    return outs


def _ssd_kernel(z_ref, xs_ref, bc_ref, dt_ref, dtb_ref, alog_ref, dsk_ref, ng_ref,
                yg_ref, h_ref, ht_ref, *, cl, d_inner, n_state):
    c = pl.program_id(1)
    n_pairs = d_inner // PAIR
    pairs_per_group = n_pairs // SSM_GROUPS
    gb = SSM_GROUPS * n_state

    @pl.when(c == 0)
    def _():
        ht_ref[...] = jnp.zeros_like(ht_ref)

    def chunk(rows):
        xc = xs_ref[rows, :]
        bcc = bc_ref[rows, :]

        dt = _softplus(dt_ref[rows, :] + dtb_ref[...])
        dta = dt * (-jnp.exp(alog_ref[...]))
        causal = (lax.broadcasted_iota(jnp.int32, (cl, cl), 0)
                  >= lax.broadcasted_iota(jnp.int32, (cl, cl), 1))
        tri = jnp.where(causal, 1.0, 0.0).astype(BF16)
        terms = []
        rest = dta
        for _ in range(3):
            terms.append(rest.astype(BF16))
            rest = rest - terms[-1].astype(F32)
        a_cum = jnp.dot(jnp.concatenate([tri] * 3, axis=1), jnp.concatenate(terms, axis=0),
                        preferred_element_type=F32)
        a_cum = a_cum * LOG2E
        a_cum_t = a_cum.T
        dt_t = dt.T
        dte_t = jnp.exp2(a_cum_t[:, cl - 1:cl] - a_cum_t) * dt_t
        cd_row = jnp.exp2(a_cum[cl - 1:cl, :])

        lo = lax.broadcasted_iota(jnp.int32, (1, PAIR), 1) < HEAD_DIM

        for g in range(SSM_GROUPS):
            b_f = bcc[:, g * n_state:(g + 1) * n_state]
            bt_g = b_f.T
            c_g = bcc[:, gb + g * n_state:gb + (g + 1) * n_state].astype(BF16)
            cb_g = _nt_dot(c_g, b_f.astype(BF16))
            gated = []
            ssq = jnp.zeros((cl, 1), F32)
            for jj in range(pairs_per_group):
                j = g * pairs_per_group + jj
                sl = slice(j * PAIR, (j + 1) * PAIR)
                x_p = xc[:, sl]
                x_half = (jnp.where(lo, x_p, 0.0).astype(BF16), jnp.where(lo, 0.0, x_p).astype(BF16))
                y = None
                st = None
                ea = []
                for half in range(2):
                    r = 2 * j + half
                    col = jnp.broadcast_to(a_cum[:, r:r + 1], (cl, cl))
                    decay = jnp.exp2(jnp.where(causal, col - a_cum_t[r:r + 1, :], -jnp.inf))
                    w = (cb_g * decay * dt_t[r:r + 1, :]).astype(BF16)
                    part = jnp.dot(w, x_half[half], preferred_element_type=F32)
                    y = part if y is None else y + part
                    part = jnp.dot((bt_g * dte_t[r:r + 1, :]).astype(BF16), x_half[half],
                                   preferred_element_type=F32)
                    st = part if st is None else st + part
                    ea.append(jnp.exp2(col))
                ht_p = ht_ref[:, sl]
                y = y + (jnp.dot(c_g, ht_p.astype(BF16), preferred_element_type=F32)
                         * jnp.where(lo, ea[0], ea[1]))
                ht_ref[:, sl] = ht_p * _lane_pair(cd_row, j, lo) + st
                y = y + dsk_ref[:, sl] * x_p
                gy = y * _silu(z_ref[rows, sl])
                ssq = ssq + jnp.sum(gy * gy, axis=-1, keepdims=True)
                gated.append(gy)
            scale = lax.rsqrt(ssq / (pairs_per_group * PAIR) + EPS)
            for jj in range(pairs_per_group):
                sl = slice((g * pairs_per_group + jj) * PAIR, (g * pairs_per_group + jj + 1) * PAIR)
                yg_ref[rows, sl] = (gated[jj] * scale * ng_ref[:, sl]).astype(BF16)

    chunk(slice(0, cl))

    @pl.when(c == pl.num_programs(1) - 1)
    def _():
        for j in range(n_pairs):
            sl = slice(j * PAIR, (j + 1) * PAIR)
            h_ref[0, sl, :] = ht_ref[:, sl].T


def _ssd_prompt(z, xbc, dt_raw, dt_bias, a_log, d_skip, norm_g, bsz, seq, d_inner, n_state):
    cl = SSD_CHUNK
    assert seq % cl == 0
    rows = cl
    nc = seq // rows
    gb = SSM_GROUPS * n_state
    assert d_inner % (2 * gb) == 0
    bc_blk = d_inner // (2 * gb)
    kern = functools.partial(_ssd_kernel, cl=cl, d_inner=d_inner, n_state=n_state)
    row = lambda b, c: b * nc + c
    const = lambda b, c: (0, 0)
    return pl.pallas_call(
        kern,
        grid=(bsz, nc),
        in_specs=[pl.BlockSpec((rows, d_inner), lambda b, c: (row(b, c), 0)),
                  pl.BlockSpec((rows, d_inner), lambda b, c: (row(b, c), 0)),
                  pl.BlockSpec((rows, 2 * gb), lambda b, c: (row(b, c), bc_blk)),
                  pl.BlockSpec((rows, LANES), lambda b, c: (row(b, c), 0)),
                  pl.BlockSpec((1, LANES), const),
                  pl.BlockSpec((1, LANES), const),
                  pl.BlockSpec((1, d_inner), const),
                  pl.BlockSpec((1, d_inner), const)],
        out_specs=[pl.BlockSpec((rows, d_inner), lambda b, c: (row(b, c), 0)),
                   pl.BlockSpec((1, d_inner, n_state), lambda b, c: (b, 0, 0))],
        out_shape=[jax.ShapeDtypeStruct((bsz * seq, d_inner), BF16),
                   jax.ShapeDtypeStruct((bsz, d_inner, n_state), F32)],
        scratch_shapes=[pltpu.VMEM((n_state, d_inner), F32)],
        compiler_params=_params("parallel", "arbitrary"),
        name="ssd_prompt",
    )(z, xbc, xbc, dt_raw, dt_bias, a_log, d_skip, norm_g)


def _ssd_step_kernel(zx_ref, dt_ref, cs_ref, h_ref, cw_ref, cb_ref, dtb_ref, alog_ref, dsk_ref, ng_ref,
                     yg_ref, cso_ref, ho_ref, *, d_inner, n_state):
    n_pairs = d_inner // PAIR
    pairs_per_group = n_pairs // SSM_GROUPS
    gb = SSM_GROUPS * n_state
    zx = zx_ref[0]
    z = zx[:, 0:d_inner]
    xbc = zx[:, d_inner:]
    prev = cs_ref[0]
    acc = cb_ref[...] + cw_ref[CONV_TAPS - 1:CONV_TAPS, :] * xbc
    for k in range(CONV_TAPS - 1):
        acc = acc + cw_ref[k:k + 1, :] * prev[k:k + 1, :]
    cso_ref[0, 0:CONV_TAPS - 2, :] = prev[1:CONV_TAPS - 1, :]
    cso_ref[0, CONV_TAPS - 2:CONV_TAPS - 1, :] = xbc
    act = jax.nn.silu(acc)
    xs = act[:, 0:d_inner]
    dt = _softplus(dt_ref[0] + dtb_ref[...])
    da = jnp.exp(dt * (-jnp.exp(alog_ref[...])))

    lo = lax.broadcasted_iota(jnp.int32, (1, PAIR), 1) < HEAD_DIM
    row_lo = lax.broadcasted_iota(jnp.int32, (PAIR, 1), 0) < HEAD_DIM
    row8 = lax.broadcasted_iota(jnp.int32, (HALO, 1), 0)

    for g in range(SSM_GROUPS):
        b_g = act[:, d_inner + g * n_state:d_inner + (g + 1) * n_state]
        c_g = act[:, d_inner + gb + g * n_state:d_inner + gb + (g + 1) * n_state]
        gated = []
        ssq = jnp.zeros((1, 1), F32)
        for jj in range(pairs_per_group):
            j = g * pairs_per_group + jj
            sl = slice(j * PAIR, (j + 1) * PAIR)
            x_p = xs[:, sl]
            xdt = x_p * _lane_pair(dt, j, lo)
            x_hi, x_lo = _split2(xdt)
            b_hi, b_lo = _split2(b_g)
            lhs = jnp.where(row8 == 0, x_hi, jnp.where(row8 == 1, x_lo, jnp.where(row8 == 2, x_hi,
                            jnp.where(row8 == 3, x_lo, 0.0)))).astype(BF16)
            rhs = jnp.where(row8 == 0, b_hi, jnp.where(row8 == 1, b_hi, jnp.where(row8 == 2, b_lo,
                            jnp.where(row8 == 3, b_lo, 0.0)))).astype(BF16)
            d_col = jnp.where(row_lo, da[:, 2 * j:2 * j + 1], da[:, 2 * j + 1:2 * j + 2])
            h_new = d_col * h_ref[0, sl, :] + _tn_dot(lhs, rhs)
            ho_ref[0, sl, :] = h_new
            c_hi, c_lo = _split2(c_g)
            c2 = jnp.where(row8 == 0, c_hi, jnp.where(row8 == 1, c_lo, 0.0)).astype(BF16)
            h_hi = h_new.astype(BF16)
            h_lo = (h_new - h_hi.astype(F32)).astype(BF16)
            yy = _nt_dot(c2, h_hi) + _nt_dot(c2, h_lo)
            y = yy[0:1, :] + yy[1:2, :]
            y = y + dsk_ref[:, sl] * x_p
            gy = y * jax.nn.silu(z[:, sl])
            ssq = ssq + jnp.sum(gy * gy, axis=-1, keepdims=True)
            gated.append(gy)
        scale = lax.rsqrt(ssq / (pairs_per_group * PAIR) + EPS)
        for jj in range(pairs_per_group):
            sl = slice((g * pairs_per_group + jj) * PAIR, (g * pairs_per_group + jj + 1) * PAIR)
            yg_ref[0, :, sl] = gated[jj] * scale * ng_ref[:, sl]


def _ssd_step(zx, dt_raw, state_conv, state_ssm, conv_w, conv_b, dt_bias, a_log, d_skip, norm_g,
              d_inner, n_state):
    bsz = zx.shape[0]
    conv_dim = conv_w.shape[1]
    kern = functools.partial(_ssd_step_kernel, d_inner=d_inner, n_state=n_state)
    const = lambda b: (0, 0)
    per_b = lambda b: (b, 0, 0)
    return pl.pallas_call(
        kern,
        grid=(bsz,),
        in_specs=[pl.BlockSpec((1, 1, d_inner + conv_dim), per_b),
                  pl.BlockSpec((1, 1, LANES), per_b),
                  pl.BlockSpec((1, CONV_TAPS - 1, conv_dim), per_b),
                  pl.BlockSpec((1, d_inner, n_state), per_b),
                  pl.BlockSpec((CONV_TAPS, conv_dim), const),
                  pl.BlockSpec((1, conv_dim), const),
                  pl.BlockSpec((1, LANES), const),
                  pl.BlockSpec((1, LANES), const),
                  pl.BlockSpec((1, d_inner), const),
                  pl.BlockSpec((1, d_inner), const)],
        out_specs=[pl.BlockSpec((1, 1, d_inner), per_b),
                   pl.BlockSpec((1, CONV_TAPS - 1, conv_dim), per_b),
                   pl.BlockSpec((1, d_inner, n_state), per_b)],
        out_shape=[jax.ShapeDtypeStruct((bsz, 1, d_inner), F32),
                   jax.ShapeDtypeStruct((bsz, CONV_TAPS - 1, conv_dim), F32),
                   jax.ShapeDtypeStruct((bsz, d_inner, n_state), F32)],
        compiler_params=_params("parallel"),
        name="ssd_step",
    )(zx.reshape(bsz, 1, -1), dt_raw.reshape(bsz, 1, LANES), state_conv,
      state_ssm.reshape(bsz, d_inner, n_state), conv_w, conv_b, dt_bias, a_log, d_skip, norm_g)


def _linear_res_kernel(a_ref, w_ref, r_ref, o_ref):
    o_ref[...] = r_ref[...] + jnp.dot(a_ref[...].astype(BF16), w_ref[...], preferred_element_type=F32)


def _linear_res(a, w, res):
    m, k = a.shape
    n = w.shape[1]
    tm = _row_tile(m, 512)
    return pl.pallas_call(
        _linear_res_kernel,
        grid=(m // tm,),
        in_specs=[pl.BlockSpec((tm, k), lambda i: (i, 0)),
                  pl.BlockSpec((k, n), lambda i: (0, 0)),
                  pl.BlockSpec((tm, n), lambda i: (i, 0))],
        out_specs=pl.BlockSpec((tm, n), lambda i: (i, 0)),
        out_shape=jax.ShapeDtypeStruct((m, n), F32),
        compiler_params=_params("parallel"),
        name="linear_res",
    )(a, w, res)


def _ffn_kernel(x_ref, g_ref, wg_ref, wu_ref, wd_ref, o_ref, xn_ref, *, tf):
    @pl.when(pl.program_id(1) == 0)
    def _():
        x = x_ref[...]
        xn_ref[...] = (_rms_scale(x) * g_ref[...]).astype(BF16)
        o_ref[...] = x

    xn = xn_ref[...]
    acts = []
    for c0 in range(0, tf, MXU_WIDTH):
        cols = slice(c0, min(c0 + MXU_WIDTH, tf))
        gate = jnp.dot(xn, wg_ref[:, cols], preferred_element_type=F32)
        up = jnp.dot(xn, wu_ref[:, cols], preferred_element_type=F32)
        acts.append((jax.nn.silu(gate) * up).astype(BF16))
    o_ref[...] += jnp.dot(jnp.concatenate(acts, axis=1), wd_ref[...], preferred_element_type=F32)


def _ffn(x, g, w_gate_up, w_down):
    m, d = x.shape
    d_ff = w_down.shape[0]
    tm = _row_tile(m, 1024)
    tf = d_ff // 2 if d_ff % (2 * LANES) == 0 else d_ff
    nf = d_ff // tf
    return pl.pallas_call(
        functools.partial(_ffn_kernel, tf=tf),
        grid=(m // tm, nf),
        in_specs=[pl.BlockSpec((tm, d), lambda i, j: (i, 0)),
                  pl.BlockSpec((1, d), lambda i, j: (0, 0)),
                  pl.BlockSpec((d, tf), lambda i, j: (0, j)),
                  pl.BlockSpec((d, tf), lambda i, j: (0, nf + j)),
                  pl.BlockSpec((tf, d), lambda i, j: (j, 0))],
        out_specs=pl.BlockSpec((tm, d), lambda i, j: (i, 0)),
        out_shape=jax.ShapeDtypeStruct((m, d), F32),
        scratch_shapes=[pltpu.VMEM((tm, d), BF16)],
        compiler_params=_params("parallel", "arbitrary"),
        name="ffn",
    )(x, g, w_gate_up, w_gate_up, w_down)


def _qkv_kernel(x_ref, gq_ref, gkv_ref, w_ref, qn_ref, kn_ref, knc_ref, q_ref, k_ref, v4_ref, vd_ref,
                *, n_heads, k_transposed):
    lo = lax.broadcasted_iota(jnp.int32, (1, PAIR), 1) < HEAD_DIM

    def qk_norm(y, gain_ref, out_ref, post):
        for h in range(n_heads):
            yh = y[:, h * PAIR:(h + 1) * PAIR]
            sq = yh * yh
            s_lo = jnp.sum(jnp.where(lo, sq, 0.0), axis=-1, keepdims=True)
            s_hi = jnp.sum(jnp.where(lo, 0.0, sq), axis=-1, keepdims=True)
            r = jnp.where(lo, lax.rsqrt(s_lo / HEAD_DIM + EPS), lax.rsqrt(s_hi / HEAD_DIM + EPS))
            out_ref[:, h * PAIR:(h + 1) * PAIR] = (yh * r * gain_ref[...] * post).astype(out_ref.dtype)

    xh = _rms_scale(x_ref[...])
    xq = (xh * gq_ref[...]).astype(BF16)
    xkv = (xh * gkv_ref[...]).astype(BF16)
    qk_norm(jnp.dot(xq, w_ref[0], preferred_element_type=F32), qn_ref, q_ref, HEAD_DIM ** -0.5 * LOG2E)

    if k_transposed:
        kt = _nt_dot(w_ref[1], xkv)
        for g in range(2 * n_heads):
            rows = slice(g * HEAD_DIM, (g + 1) * HEAD_DIM)
            blk = kt[rows, :]
            r = lax.rsqrt(jnp.mean(blk * blk, axis=0, keepdims=True) + EPS)
            k_ref[0, rows, :] = blk * r * knc_ref[...]
    else:
        qk_norm(jnp.dot(xkv, w_ref[1], preferred_element_type=F32), kn_ref, k_ref, 1.0)

    v = jnp.dot(xkv, w_ref[2], preferred_element_type=F32)
    vd_ref[...] = v.astype(BF16)
    for h in range(n_heads):
        v4_ref[:, h, :] = v[:, h * PAIR:(h + 1) * PAIR]


def _qkv(x, g_q, g_kv, w_q, w_k, w_v, q_norm, k_norm, seq=None):
    m, d = x.shape
    n_heads = d // PAIR
    k_transposed = seq is not None
    tm = _row_tile(m if seq is None else seq, 512)
    w = jnp.stack([w_q, w_k.T if k_transposed else w_k, w_v]).astype(BF16)
    pair_gain = jnp.tile(k_norm.astype(F32), 2).reshape(1, PAIR)
    kern = functools.partial(_qkv_kernel, n_heads=n_heads, k_transposed=k_transposed)
    const = lambda i: (0, 0)
    rowblk = lambda i: (i, 0)
    if k_transposed:
        per_seq = seq // tm
        k_spec = pl.BlockSpec((1, d, tm), lambda i: (i // per_seq, 0, i % per_seq))
        k_shape = jax.ShapeDtypeStruct((m // seq, d, seq), F32)
    else:
        k_spec = pl.BlockSpec((tm, d), rowblk)
        k_shape = jax.ShapeDtypeStruct((m, d), F32)
    return pl.pallas_call(
        kern,
        grid=(m // tm,),
        in_specs=[pl.BlockSpec((tm, d), rowblk),
                  pl.BlockSpec((1, d), const),
                  pl.BlockSpec((1, d), const),
                  pl.BlockSpec((3, d, d), lambda i: (0, 0, 0)),
                  pl.BlockSpec((1, PAIR), const),
                  pl.BlockSpec((1, PAIR), const),
                  pl.BlockSpec((HEAD_DIM, 1), const)],
        out_specs=[pl.BlockSpec((tm, d), rowblk), k_spec,
                   pl.BlockSpec((tm, n_heads, PAIR), lambda i: (i, 0, 0)),
                   pl.BlockSpec((tm, d), rowblk)],
        out_shape=[jax.ShapeDtypeStruct((m, d), BF16), k_shape,
                   jax.ShapeDtypeStruct((m, n_heads, PAIR), F32),
                   jax.ShapeDtypeStruct((m, d), BF16)],
        compiler_params=_params("parallel"),
        name="qkv_proj",
    )(x, g_q, g_kv, w, jnp.tile(q_norm.astype(F32), 2).reshape(1, PAIR), pair_gain,
      k_norm.astype(F32).reshape(HEAD_DIM, 1))


def _bias_tiles_kernel(rb_ref, o_ref, *, t):
    h = pl.program_id(0)
    i = lax.broadcasted_iota(jnp.int32, (t, t), 0)
    j = lax.broadcasted_iota(jnp.int32, (t, t), 1)
    for off in range(2):
        n = i - j + off * t
        bucket = _bucket(n)
        tile = jnp.zeros((t, t), F32)
        for b in range(N_BUCKETS):
            tile = jnp.where(bucket == b, rb_ref[b, h], tile)
        o_ref[0, off] = jnp.where(n >= 0, (tile - rb_ref[N_BUCKETS - 1, h]) * LOG2E, -jnp.inf)


def _bias_tiles(rel_bias, t):
    n_heads = rel_bias.shape[1]
    return pl.pallas_call(
        functools.partial(_bias_tiles_kernel, t=t),
        grid=(n_heads,),
        in_specs=[pl.BlockSpec(memory_space=pltpu.SMEM)],
        out_specs=pl.BlockSpec((1, 2, t, t), lambda h: (h, 0, 0, 0)),
        out_shape=jax.ShapeDtypeStruct((n_heads, 2, t, t), F32),
        compiler_params=_params("parallel"),
        name="bias_tiles",
    )(rel_bias)


def _lambda_value(lam_ref, lambda_init):
    s1 = jnp.sum(lam_ref[0:1, :] * lam_ref[1:2, :], axis=-1, keepdims=True)
    s2 = jnp.sum(lam_ref[2:3, :] * lam_ref[3:4, :], axis=-1, keepdims=True)
    return jnp.exp(s1) - jnp.exp(s2) + lambda_init


def _attn_kernel(q_ref, k_ref, v_ref, bias_ref, rb_ref, lam_ref, sg_ref, o_ref,
                 kb_ref, kb2_ref, q12_ref, m_ref, l_ref, acc_ref, *, t, rc_far, rc_near, nq, lambda_init):
    h = pl.program_id(1)
    qi = pl.program_id(2)

    @pl.when(qi == 0)
    def _():
        for ki in range(nq):
            kb_ref[ki] = k_ref[0, :, ki * t:(ki + 1) * t].astype(BF16)
        for kp in range(nq // 2):
            kb2_ref[kp] = k_ref[0, :, kp * 2 * t:(kp + 1) * 2 * t].astype(BF16)

    q = q_ref[0]
    lo = lax.broadcasted_iota(jnp.int32, (t, PAIR), 1) < HEAD_DIM
    zero = jnp.zeros_like(q)
    q12_ref[0:t, :] = jnp.where(lo, q, zero)
    q12_ref[t:2 * t, :] = jnp.where(lo, zero, q)

    m_ref[...] = jnp.full(m_ref.shape, -jnp.inf, F32)
    l_ref[...] = jnp.zeros(l_ref.shape, F32)
    acc_ref[...] = jnp.zeros(acc_ref.shape, F32)

    far_bias = rb_ref[N_BUCKETS - 1, h] * LOG2E

    def block(rc, keys_of, base, cols_of, near_of=None):
        for c in range(2 * t // rc):
            rows = slice(c * rc, (c + 1) * rc)
            r0 = (c * rc) % t
            ncols = cols_of(r0)
            s = jnp.dot(q12_ref[rows, :], keys_of(ncols), preferred_element_type=F32)
            tiles = []
            for k in range(ncols // LANES):
                sk = s[:, k * LANES:(k + 1) * LANES]
                near = None if near_of is None else near_of(k)
                if near is not None and near[0] * t + r0 - (near[1] * LANES + LANES - 1) < MAX_DISTANCE:
                    slab = []
                    for a in range(rc // LANES):
                        rel = (near[0] * t + r0) // LANES + a - near[1]
                        if rel < 0:
                            slab.append(jnp.full((LANES, LANES), -jnp.inf, F32))
                        elif rel <= 1:
                            slab.append(bias_ref[0, rel])
                        else:
                            slab.append(jnp.zeros((LANES, LANES), F32))
                    sk = sk + jnp.concatenate(slab, axis=0)
                tiles.append(sk)
            m_prev = m_ref[rows, :]
            part = tiles[0]
            for sk in tiles[1:]:
                part = jnp.maximum(part, sk)
            m_new = jnp.maximum(m_prev, jnp.max(part, axis=-1, keepdims=True) + far_bias)
            alpha = jnp.exp2(m_prev - m_new)
            m_sub = m_new - far_bias
            ps = [jnp.exp2(sk - m_sub) for sk in tiles]
            psum = ps[0]
            for pk in ps[1:]:
                psum = psum + pk
            l_ref[rows, :] = alpha * l_ref[rows, :] + psum
            pv = jnp.dot(jnp.concatenate(ps, axis=1).astype(BF16), v_ref[0, pl.ds(base, ncols), :],
                         preferred_element_type=F32)
            acc_ref[rows, :] = alpha * acc_ref[rows, :] + pv
            m_ref[rows, :] = m_new

    n_far = jnp.maximum(qi - 1, 0)

    def far_pair(kp, carry):
        block(rc_far, lambda n: kb2_ref[kp, :, 0:n], pl.multiple_of(kp * 2 * t, 2 * t), lambda r0: 2 * t)
        return carry

    lax.fori_loop(0, n_far // 2, far_pair, 0)

    @pl.when(n_far % 2 == 1)
    def _():
        ki = n_far - 1
        block(rc_far, lambda n: kb_ref[ki, :, 0:n], pl.multiple_of(ki * t, t), lambda r0: t)

    nlt = t // LANES

    @pl.when(qi >= 1)
    def _():
        ki = qi - 1
        block(rc_near, lambda n: jnp.concatenate([kb_ref[ki], kb_ref[qi, :, 0:n - t]], axis=1),
              pl.multiple_of(ki * t, t), lambda r0: t + r0 + rc_near,
              near_of=lambda k: (1, k) if k < nlt else (0, k - nlt))

    @pl.when(qi == 0)
    def _():
        block(rc_near, lambda n: kb_ref[0, :, 0:n], 0, lambda r0: r0 + rc_near, near_of=lambda k: (0, k))

    l = jnp.sum(l_ref[...], axis=-1, keepdims=True)
    o1 = acc_ref[0:t, :] / l[0:t, :]
    o2 = acc_ref[t:2 * t, :] / l[t:2 * t, :]
    o = o1 - _lambda_value(lam_ref, lambda_init) * o2
    o_ref[0] = (_rms_scale(o) * sg_ref[...] * (1.0 - lambda_init)).astype(o_ref.dtype)


def _prompt_attention(q, k_t, v, rel_bias, lam_rows, subln, lambda_init, bsz, seq):
    d = q.shape[-1]
    n_heads = d // PAIR
    t = min(seq, 512)
    assert seq % t == 0 and t >= MAX_DISTANCE
    nq = seq // t
    assert MAX_DISTANCE <= LANES
    tiles = _bias_tiles(rel_bias, LANES)
    rc_far, rc_near = t, t
    assert t % rc_near == 0 and rc_near % LANES == 0
    kern = functools.partial(_attn_kernel, t=t, rc_far=rc_far, rc_near=rc_near, nq=nq, lambda_init=lambda_init)
    const = lambda b, h, i: (0, 0)
    return pl.pallas_call(
        kern,
        grid=(bsz, n_heads, nq),
        in_specs=[pl.BlockSpec((1, t, PAIR), lambda b, h, i: (b, i, h)),
                  pl.BlockSpec((1, PAIR, seq), lambda b, h, i: (b, h, 0)),
                  pl.BlockSpec((1, seq, PAIR), lambda b, h, i: (b, 0, h)),
                  pl.BlockSpec((1, 2, LANES, LANES), lambda b, h, i: (h, 0, 0, 0)),
                  pl.BlockSpec(memory_space=pltpu.SMEM),
                  pl.BlockSpec((4, LANES), const),
                  pl.BlockSpec((1, PAIR), const)],
        out_specs=pl.BlockSpec((1, t, PAIR), lambda b, h, i: (b, i, h)),
        out_shape=jax.ShapeDtypeStruct((bsz, seq, d), BF16),
        scratch_shapes=[pltpu.VMEM((nq, PAIR, t), BF16), pltpu.VMEM((max(nq // 2, 1), PAIR, 2 * t), BF16),
                        pltpu.VMEM((2 * t, PAIR), BF16),
                        pltpu.VMEM((2 * t, LANES), F32), pltpu.VMEM((2 * t, LANES), F32),
                        pltpu.VMEM((2 * t, PAIR), F32)],
        compiler_params=_params("parallel", "parallel", "arbitrary"),
        name="prompt_attention",
    )(q.reshape(bsz, seq, d), k_t, v.reshape(bsz, seq, d), tiles, rel_bias, lam_rows, subln)


def _sample_attn_kernel(pt_ref, q_ref, kn_ref, vn_ref, *rest, pages_per_step, past, n_heads, lambda_init):
    k_refs = rest[0:pages_per_step]
    v_refs = rest[pages_per_step:2 * pages_per_step]
    rb_ref, lam_ref, sg_ref, o_ref, qb_ref, bias_ref, m_ref, l_ref, acc_ref = rest[2 * pages_per_step:]
    step_id = pl.program_id(1)
    last = pl.num_programs(1) - 1
    nrow = 2 * n_heads
    d = n_heads * PAIR
    keys = pages_per_step * PAGE_SIZE

    @pl.when(step_id == 0)
    def _():
        bias_ref[...] = jnp.broadcast_to(rb_ref[:, N_BUCKETS - 1:N_BUCKETS] * LOG2E, (nrow, keys))

    @pl.when(step_id == last)
    def _():
        key = lax.broadcasted_iota(jnp.int32, (nrow, keys), 1)
        bucket = _bucket(past - (step_id * keys + key))
        bias = jnp.zeros((nrow, keys), F32)
        for b in range(N_BUCKETS):
            bias = jnp.where(bucket == b, rb_ref[:, b:b + 1], bias)
        bias_ref[...] = bias * LOG2E

    @pl.when(step_id == 0)
    def _():
        q = q_ref[0].astype(F32)
        row = lax.broadcasted_iota(jnp.int32, (nrow, d), 0)
        col = lax.broadcasted_iota(jnp.int32, (nrow, d), 1)
        qblk = jnp.where(jnp.right_shift(col, 6) == row, jnp.broadcast_to(q, (nrow, d)), 0.0)
        qb_ref[...] = qblk.astype(BF16)
        m_ref[...] = jnp.sum(qblk * kn_ref[0], axis=-1, keepdims=True) + rb_ref[:, 0:1] * LOG2E
        l_ref[...] = jnp.ones(l_ref.shape, F32)
        vn = vn_ref[0].astype(F32)
        for h in range(n_heads):
            acc_ref[2 * h:2 * h + 2, :] = jnp.broadcast_to(vn[:, h * PAIR:(h + 1) * PAIR], (2, PAIR))

    qb = qb_ref[...]
    s = jnp.concatenate([jnp.dot(qb, k_refs[g][0].astype(BF16), preferred_element_type=F32)
                         for g in range(pages_per_step)], axis=1) + bias_ref[...]
    m_prev = m_ref[...]
    m_new = jnp.maximum(m_prev, jnp.max(s, axis=-1, keepdims=True))
    alpha = jnp.exp2(m_prev - m_new)
    p = jnp.exp2(s - m_new)
    l_ref[...] = alpha * l_ref[...] + jnp.sum(p, axis=-1, keepdims=True)
    pb = p.astype(BF16)
    row_head = jnp.right_shift(lax.broadcasted_iota(jnp.int32, (nrow, PAIR), 0), 1)
    pv = jnp.zeros((nrow, PAIR), F32)
    for h in range(n_heads):
        v_h = jnp.concatenate([v_refs[g][0, pl.ds(h, PAGE_SIZE, stride=n_heads), :].astype(BF16)
                               for g in range(pages_per_step)], axis=0)
        pv = jnp.where(row_head == h, jnp.dot(pb, v_h, preferred_element_type=F32), pv)
    acc_ref[...] = alpha * acc_ref[...] + pv
    m_ref[...] = m_new

    @pl.when(step_id == last)
    def _():
        rowc = lax.broadcasted_iota(jnp.int32, (nrow, 1), 0)
        lam = _lambda_value(lam_ref, lambda_init)
        coef = jnp.where(jnp.bitwise_and(rowc, 1) == 0, 1.0, -lam) / l_ref[...]
        scaled = acc_ref[...] * coef
        o = scaled + pltpu.roll(scaled, nrow - 1, 0)
        on = _rms_scale(o) * sg_ref[...] * (1.0 - lambda_init)
        for h in range(n_heads):
            o_ref[0, :, h * PAIR:(h + 1) * PAIR] = on[2 * h:2 * h + 1, :]


def _sample_attention(q, k_new, v_new, cache_k, cache_v, page_table, rel_rows, lam_rows, subln, lambda_init):
    bsz, d = q.shape
    n_heads = d // PAIR
    n_pages = page_table.shape[1]
    past = n_pages * PAGE_SIZE
    pages_per_step = max(g for g in (16, 8, 4, 2, 1) if n_pages % g == 0)
    n_pool = cache_k.shape[0]
    kern = functools.partial(_sample_attn_kernel, pages_per_step=pages_per_step, past=past,
                             n_heads=n_heads, lambda_init=lambda_init)
    per_b = lambda b, s, pt: (b, 0, 0)
    const = lambda b, s, pt: (0, 0)

    def k_spec(g):
        return pl.BlockSpec((1, d, PAGE_SIZE), lambda b, s, pt: (pt[b, s * pages_per_step + g], 0, 0))

    def v_spec(g):
        return pl.BlockSpec((1, PAGE_SIZE * n_heads, PAIR),
                            lambda b, s, pt: (pt[b, s * pages_per_step + g], 0, 0))

    grid_spec = pltpu.PrefetchScalarGridSpec(
        num_scalar_prefetch=1,
        grid=(bsz, n_pages // pages_per_step),
        in_specs=([pl.BlockSpec((1, 1, d), per_b)] * 3
                  + [k_spec(g) for g in range(pages_per_step)]
                  + [v_spec(g) for g in range(pages_per_step)]
                  + [pl.BlockSpec((2 * n_heads, N_BUCKETS), const),
                     pl.BlockSpec((4, LANES), const),
                     pl.BlockSpec((1, PAIR), const)]),
        out_specs=pl.BlockSpec((1, 1, d), per_b),
        scratch_shapes=[pltpu.VMEM((2 * n_heads, d), BF16),
                        pltpu.VMEM((2 * n_heads, pages_per_step * PAGE_SIZE), F32),
                        pltpu.VMEM((2 * n_heads, 1), F32), pltpu.VMEM((2 * n_heads, 1), F32),
                        pltpu.VMEM((2 * n_heads, PAIR), F32)],
    )
    ck = jnp.transpose(cache_k, (0, 2, 3, 4, 1)).reshape(n_pool, d, PAGE_SIZE)
    cv = cache_v.reshape(n_pool, PAGE_SIZE * n_heads, PAIR)
    return pl.pallas_call(
        kern,
        grid_spec=grid_spec,
        out_shape=jax.ShapeDtypeStruct((bsz, 1, d), F32),
        compiler_params=_params("parallel", "arbitrary"),
        name="sample_attention",
    )(page_table, q.reshape(bsz, 1, d), k_new.reshape(bsz, 1, d), v_new.reshape(bsz, 1, d),
      *([ck] * pages_per_step), *([cv] * pages_per_step), rel_rows, lam_rows, subln).reshape(bsz, d)


def _pad_lanes(v):
    return jnp.pad(v.astype(F32), (0, LANES - v.shape[0])).reshape(1, LANES)


def kernel(x_prompt, x_sample, state_conv, state_ssm, cache_k, cache_v, page_table, norm_mix, norm_ffn, w_in, conv_w, conv_b, dt_bias, a_log, d_skip, ssm_norm, w_out_ssm, norm_kv, w_kv, k_norm, w_q, q_norm, lambda_q1, lambda_k1, lambda_q2, lambda_k2, subln, w_o, rel_bias, w_gate_up, w_down):
    bsz, seq, d = x_prompt.shape
    dec = x_sample.shape[0]
    assert x_sample.shape[1] == 1
    n_ssm_layers, ssm_heads = dt_bias.shape
    depth = norm_mix.shape[0]
    assert n_ssm_layers == 1 and depth == 2
    d_inner = w_out_ssm.shape[1]
    n_state = state_ssm.shape[-1]
    assert d_inner == ssm_heads * HEAD_DIM and n_state == LANES and ssm_heads <= LANES
    n_heads = d // PAIR

    xp = x_prompt.reshape(bsz * seq, d)
    xs = x_sample.reshape(dec, d)
    row = lambda v: v.astype(F32).reshape(1, -1)

    zx_cols = 2 * d_inner + 2 * SSM_GROUPS * n_state
    w_z = w_in[0][:, :d_inner].astype(BF16)
    w_c = w_in[0][:, d_inner:zx_cols].astype(BF16)
    w_dt = jnp.pad(w_in[0][:, zx_cols:], ((0, 0), (0, LANES - ssm_heads))).astype(BF16)
    g_mix0 = row(norm_mix[0])
    cw0, cb0 = conv_w[0], row(conv_b[0])
    ssd_args = (_pad_lanes(dt_bias[0]), _pad_lanes(a_log[0]),
                row(jnp.repeat(d_skip[0], HEAD_DIM)), row(ssm_norm[0]))
    w_out = w_out_ssm[0].astype(BF16)
    w_gu0, w_dn0 = w_gate_up[0].astype(BF16), w_down[0].astype(BF16)

    z_p, xbc_p, dt_p, conv_p = _in_proj(xp, g_mix0, w_z, w_c, w_dt, cw0, cb0, seq=seq)
    yg_p, ssm_p = _ssd_prompt(z_p, xbc_p, dt_p, *ssd_args, bsz, seq, d_inner, n_state)
    xp = _linear_res(yg_p, w_out, xp)
    xp = _ffn(xp, row(norm_ffn[0]), w_gu0, w_dn0)

    z_s, xbc_s, dt_s = _in_proj(xs, g_mix0, w_z, w_c, w_dt)
    zx_s = jnp.concatenate([z_s, xbc_s], axis=1)
    yg_s, conv_s, ssm_s = _ssd_step(zx_s, dt_s, state_conv[0], state_ssm[0], cw0, cb0, *ssd_args,
                                    d_inner, n_state)
    xs = _linear_res(yg_s.reshape(dec, d_inner), w_out, xs)
    xs = _ffn(xs, row(norm_ffn[0]), w_gu0, w_dn0)

    lambda_init = 0.8 - 0.6 * math.exp(-0.3 * 1)
    qkv_args = (row(norm_mix[1]), row(norm_kv), w_q[0], w_kv[:, :d], w_kv[:, d:], q_norm[0], k_norm)
    lam_rows = jnp.pad(jnp.stack([lambda_q1[0], lambda_k1[0], lambda_q2[0], lambda_k2[0]]).astype(F32),
                       ((0, 0), (0, LANES - HEAD_DIM)))
    sg = row(subln[0])
    w_oo = w_o[0].astype(BF16)
    w_gu1, w_dn1 = w_gate_up[1].astype(BF16), w_down[1].astype(BF16)

    q_p, kt_p, v_p, vd_p = _qkv(xp, *qkv_args, seq=seq)
    o_p = _prompt_attention(q_p, kt_p, vd_p, rel_bias.astype(F32), lam_rows, sg, lambda_init, bsz, seq)
    xp = _linear_res(o_p.reshape(bsz * seq, d), w_oo, xp)
    xp = _ffn(xp, row(norm_ffn[1]), w_gu1, w_dn1)

    q_s, k_s, v_s, vd_s = _qkv(xs, *qkv_args)
    rel_rows = jnp.repeat(rel_bias.astype(F32).T, 2, axis=0)
    o_s = _sample_attention(q_s.astype(F32), k_s, vd_s.astype(F32), cache_k, cache_v, page_table,
                            rel_rows, lam_rows, sg, lambda_init)
    xs = _linear_res(o_s, w_oo, xs)
    xs = _ffn(xs, row(norm_ffn[1]), w_gu1, w_dn1)

    k_p = jnp.transpose(kt_p.reshape(bsz, n_heads, 2, HEAD_DIM, seq), (0, 4, 1, 2, 3))
    return (xp.reshape(bsz, seq, d), xs.reshape(dec, 1, d),
            conv_p[None], ssm_p.reshape(1, bsz, ssm_heads, HEAD_DIM, n_state),
            k_p, v_p.reshape(bsz, seq, n_heads, PAIR),
            conv_s[None], ssm_s.reshape(1, dec, ssm_heads, HEAD_DIM, n_state),
            k_s.reshape(dec, 1, n_heads, 2, HEAD_DIM), v_s.reshape(dec, 1, n_heads, PAIR))
```

```python
import functools
import math

import jax
import jax.numpy as jnp
from jax import lax
from jax.experimental import pallas as pl
from jax.experimental.pallas import tpu as pltpu

F32 = jnp.float32
BF16 = jnp.bfloat16

EPS = 1e-6
LANES = 128
HEAD_DIM = 64
PAIR = 2 * HEAD_DIM
SSD_CHUNK = 128
SSM_GROUPS = 4
CONV_TAPS = 4
HALO = 8
N_BUCKETS = 32
MAX_EXACT = N_BUCKETS // 2
MAX_DISTANCE = 128
PAGE_SIZE = 128
LOG2E = math.log2(math.e)
MXU_WIDTH = 256
VMEM_LIMIT = 52 * 1024 * 1024


def _params(*semantics):
    return pltpu.CompilerParams(dimension_semantics=semantics, vmem_limit_bytes=VMEM_LIMIT)


def _row_tile(m, preferred):
    if m <= preferred:
        return m
    t = preferred
    while m % t or t % 16:
        t -= 1
    return t


def _nt_dot(a, b):
    return lax.dot_general(a, b, (((1,), (1,)), ((), ())), preferred_element_type=F32)


def _tn_dot(a, b):
    return lax.dot_general(a, b, (((0,), (0,)), ((), ())), preferred_element_type=F32)


def _rms_scale(x):
    return x * lax.rsqrt(jnp.mean(x * x, axis=-1, keepdims=True) + EPS)


def _split2(x):
    hi = x.astype(BF16).astype(F32)
    lo = (x - hi).astype(BF16).astype(F32)
    return hi, lo


def _silu(x):
    h = 0.5 * x
    return h + h * jnp.tanh(h)


def _log1p(u):
    w = 1.0 + u
    return jnp.where(w == 1.0, u, jnp.log(w) * (u / (w - 1.0)))


def _softplus(x):
    return jnp.maximum(x, 0.0) + _log1p(jnp.exp(-jnp.abs(x)))


def _lane_pair(arr, j, lo):
    return jnp.where(lo, arr[:, 2 * j:2 * j + 1], arr[:, 2 * j + 1:2 * j + 2])


def _bucket(n):
    n = jnp.maximum(n, 0)
    nf = jnp.maximum(n, 1).astype(F32)
    large = MAX_EXACT + (jnp.log(nf / MAX_EXACT) / math.log(MAX_DISTANCE / MAX_EXACT)
                         * (N_BUCKETS - MAX_EXACT)).astype(jnp.int32)
    large = jnp.minimum(large, N_BUCKETS - 1)
    return jnp.where(n < MAX_EXACT, n, large)


def _in_proj_kernel(*refs, tz, tc, tiles_per_seq):
    conv = tiles_per_seq is not None
    if conv:
        (x_ref, g_ref, wz_ref, wc_ref, wdt_ref, cw_ref, cb_ref,
         z_ref, xc_ref, dt_ref, ct_ref, xn_ref, tail_ref) = refs
    else:
        x_ref, g_ref, wz_ref, wc_ref, wdt_ref, z_ref, xc_ref, dt_ref, xn_ref = refs
    i = pl.program_id(0)
    j = pl.program_id(1)
    tm = x_ref.shape[0]
    step = 2 * MXU_WIDTH

    @pl.when(j == 0)
    def _():
        xn_ref[...] = (_rms_scale(x_ref[...]) * g_ref[...]).astype(BF16)
        dt_ref[...] = jnp.dot(xn_ref[...], wdt_ref[...], preferred_element_type=F32)

    xn = xn_ref[...]
    c_slices = [slice(c0, min(c0 + step, tc)) for c0 in range(0, tc, step)]
    z_slices = [slice(c0, min(c0 + step, tz)) for c0 in range(0, tz, step)]
    if not conv:
        for cols in c_slices:
            xc_ref[:, cols] = jnp.dot(xn, wc_ref[:, cols], preferred_element_type=F32)
        for cols in z_slices:
            z_ref[:, cols] = jnp.dot(xn, wz_ref[:, cols], preferred_element_type=F32)
        return

    raws = [jnp.dot(xn, wc_ref[:, cols], preferred_element_type=F32) for cols in c_slices]
    for cols in z_slices:
        z_ref[:, cols] = jnp.dot(xn, wz_ref[:, cols], preferred_element_type=F32)

    seq_start = (i % tiles_per_seq) == 0
    row8 = lax.broadcasted_iota(jnp.int32, (HALO, 1), 0)
    for cols, raw in zip(c_slices, raws):
        tail = jnp.where(seq_start, 0.0, tail_ref[j, :, cols])
        acc = cb_ref[:, cols] + cw_ref[CONV_TAPS - 1:CONV_TAPS, cols] * raw
        for back in range(1, CONV_TAPS):
            shifted = pltpu.roll(raw, back, 0)
            head = jnp.where(row8 < back, pltpu.roll(tail, back, 0), shifted[0:HALO, :])
            shifted = jnp.concatenate([head, shifted[HALO:, :]], axis=0)
            k = CONV_TAPS - 1 - back
            acc = acc + cw_ref[k:k + 1, cols] * shifted
        tail_ref[j, :, cols] = raw[tm - HALO:tm, :]
        ct_ref[0, :, cols] = raw[tm - (CONV_TAPS - 1):tm, :]
        xc_ref[:, cols] = _silu(acc)


def _in_proj(x, g, w_z, w_c, w_dt, conv_w=None, conv_b=None, seq=None):
    m, d = x.shape
    nz, nc = w_z.shape[1], w_c.shape[1]
    conv = seq is not None
    tm = _row_tile(m if seq is None else seq, 1024)
    assert nz % (2 * LANES) == 0 and nc % (2 * LANES) == 0
    tz, tc = nz // 2, nc // 2
    rowblk = lambda i, j: (i, j)
    colblk = lambda i, j: (0, j)
    in_specs = [pl.BlockSpec((tm, d), lambda i, j: (i, 0)),
                pl.BlockSpec((1, d), lambda i, j: (0, 0)),
                pl.BlockSpec((d, tz), colblk),
                pl.BlockSpec((d, tc), colblk),
                pl.BlockSpec((d, LANES), lambda i, j: (0, 0))]
    out_specs = [pl.BlockSpec((tm, tz), rowblk), pl.BlockSpec((tm, tc), rowblk),
                 pl.BlockSpec((tm, LANES), lambda i, j: (i, 0))]
    out_shape = [jax.ShapeDtypeStruct((m, nz), F32), jax.ShapeDtypeStruct((m, nc), F32),
                 jax.ShapeDtypeStruct((m, LANES), F32)]
    scratch = [pltpu.VMEM((tm, d), BF16)]
    args = [x, g, w_z, w_c, w_dt]
    tiles_per_seq = None
    if conv:
        tiles_per_seq = seq // tm
        in_specs += [pl.BlockSpec((CONV_TAPS, tc), colblk), pl.BlockSpec((1, tc), colblk)]
        out_specs.append(pl.BlockSpec((1, CONV_TAPS - 1, tc), lambda i, j: (i, 0, j)))
        out_shape.append(jax.ShapeDtypeStruct((m // tm, CONV_TAPS - 1, nc), F32))
        scratch.append(pltpu.VMEM((2, HALO, tc), F32))
        args += [conv_w, conv_b]
    outs = pl.pallas_call(
        functools.partial(_in_proj_kernel, tz=tz, tc=tc, tiles_per_seq=tiles_per_seq),
        grid=(m // tm, 2),
        in_specs=in_specs,
        out_specs=out_specs,
        out_shape=out_shape,
        scratch_shapes=scratch,
        compiler_params=_params("arbitrary", "arbitrary"),
        name="in_proj",
    )(*args)
    outs = list(outs)
    if conv:
        outs[3] = outs[3][tiles_per_seq - 1::tiles_per_seq]
    return outs


def _ssd_kernel(z_ref, xs_ref, bc_ref, dt_ref, dtb_ref, alog_ref, dsk_ref, ng_ref,
                yg_ref, h_ref, ht_ref, *, cl, d_inner, n_state):
    c = pl.program_id(1)
    n_pairs = d_inner // PAIR
    pairs_per_group = n_pairs // SSM_GROUPS
    gb = SSM_GROUPS * n_state

    @pl.when(c == 0)
    def _():
        ht_ref[...] = jnp.zeros_like(ht_ref)

    def chunk(rows):
        xc = xs_ref[rows, :]
        bcc = bc_ref[rows, :]

        dt = _softplus(dt_ref[rows, :] + dtb_ref[...])
        dta = dt * (-jnp.exp(alog_ref[...]))
        causal = (lax.broadcasted_iota(jnp.int32, (cl, cl), 0)
                  >= lax.broadcasted_iota(jnp.int32, (cl, cl), 1))
        tri = jnp.where(causal, 1.0, 0.0).astype(BF16)
        terms = []
        rest = dta
        for _ in range(3):
            terms.append(rest.astype(BF16))
            rest = rest - terms[-1].astype(F32)
        a_cum = jnp.dot(jnp.concatenate([tri] * 3, axis=1), jnp.concatenate(terms, axis=0),
                        preferred_element_type=F32)
        a_cum = a_cum * LOG2E
        a_cum_t = a_cum.T
        dt_t = dt.T
        dte_t = jnp.exp2(a_cum_t[:, cl - 1:cl] - a_cum_t) * dt_t
        cd_row = jnp.exp2(a_cum[cl - 1:cl, :])

        lo = lax.broadcasted_iota(jnp.int32, (1, PAIR), 1) < HEAD_DIM

        for g in range(SSM_GROUPS):
            b_f = bcc[:, g * n_state:(g + 1) * n_state]
            bt_g = b_f.T
            c_g = bcc[:, gb + g * n_state:gb + (g + 1) * n_state].astype(BF16)
            cb_g = _nt_dot(c_g, b_f.astype(BF16))
            gated = []
            ssq = jnp.zeros((cl, 1), F32)
            for jj in range(pairs_per_group):
                j = g * pairs_per_group + jj
                sl = slice(j * PAIR, (j + 1) * PAIR)
                x_p = xc[:, sl]
                x_half = (jnp.where(lo, x_p, 0.0).astype(BF16), jnp.where(lo, 0.0, x_p).astype(BF16))
                y = None
                st = None
                ea = []
                for half in range(2):
                    r = 2 * j + half
                    col = jnp.broadcast_to(a_cum[:, r:r + 1], (cl, cl))
                    decay = jnp.exp2(jnp.where(causal, col - a_cum_t[r:r + 1, :], -jnp.inf))
                    w = (cb_g * decay * dt_t[r:r + 1, :]).astype(BF16)
                    part = jnp.dot(w, x_half[half], preferred_element_type=F32)
                    y = part if y is None else y + part
                    part = jnp.dot((bt_g * dte_t[r:r + 1, :]).astype(BF16), x_half[half],
                                   preferred_element_type=F32)
                    st = part if st is None else st + part
                    ea.append(jnp.exp2(col))
                ht_p = ht_ref[:, sl]
                y = y + (jnp.dot(c_g, ht_p.astype(BF16), preferred_element_type=F32)
                         * jnp.where(lo, ea[0], ea[1]))
                ht_ref[:, sl] = ht_p * _lane_pair(cd_row, j, lo) + st
                y = y + dsk_ref[:, sl] * x_p
                gy = y * _silu(z_ref[rows, sl])
                ssq = ssq + jnp.sum(gy * gy, axis=-1, keepdims=True)
                gated.append(gy)
            scale = lax.rsqrt(ssq / (pairs_per_group * PAIR) + EPS)
            for jj in range(pairs_per_group):
                sl = slice((g * pairs_per_group + jj) * PAIR, (g * pairs_per_group + jj + 1) * PAIR)
                yg_ref[rows, sl] = (gated[jj] * scale * ng_ref[:, sl]).astype(BF16)

    chunk(slice(0, cl))

    @pl.when(c == pl.num_programs(1) - 1)
    def _():
        for j in range(n_pairs):
            sl = slice(j * PAIR, (j + 1) * PAIR)
            h_ref[0, sl, :] = ht_ref[:, sl].T


def _ssd_prompt(z, xbc, dt_raw, dt_bias, a_log, d_skip, norm_g, bsz, seq, d_inner, n_state):
    cl = SSD_CHUNK
    assert seq % cl == 0
    rows = cl
    nc = seq // rows
    gb = SSM_GROUPS * n_state
    assert d_inner % (2 * gb) == 0
    bc_blk = d_inner // (2 * gb)
    kern = functools.partial(_ssd_kernel, cl=cl, d_inner=d_inner, n_state=n_state)
    row = lambda b, c: b * nc + c
    const = lambda b, c: (0, 0)
    return pl.pallas_call(
        kern,
        grid=(bsz, nc),
        in_specs=[pl.BlockSpec((rows, d_inner), lambda b, c: (row(b, c), 0)),
                  pl.BlockSpec((rows, d_inner), lambda b, c: (row(b, c), 0)),
                  pl.BlockSpec((rows, 2 * gb), lambda b, c: (row(b, c), bc_blk)),
                  pl.BlockSpec((rows, LANES), lambda b, c: (row(b, c), 0)),
                  pl.BlockSpec((1, LANES), const),
                  pl.BlockSpec((1, LANES), const),
                  pl.BlockSpec((1, d_inner), const),
                  pl.BlockSpec((1, d_inner), const)],
        out_specs=[pl.BlockSpec((rows, d_inner), lambda b, c: (row(b, c), 0)),
                   pl.BlockSpec((1, d_inner, n_state), lambda b, c: (b, 0, 0))],
        out_shape=[jax.ShapeDtypeStruct((bsz * seq, d_inner), BF16),
                   jax.ShapeDtypeStruct((bsz, d_inner, n_state), F32)],
        scratch_shapes=[pltpu.VMEM((n_state, d_inner), F32)],
        compiler_params=_params("parallel", "arbitrary"),
        name="ssd_prompt",
    )(z, xbc, xbc, dt_raw, dt_bias, a_log, d_skip, norm_g)


def _ssd_step_kernel(zx_ref, dt_ref, cs_ref, h_ref, cw_ref, cb_ref, dtb_ref, alog_ref, dsk_ref, ng_ref,
                     yg_ref, cso_ref, ho_ref, *, d_inner, n_state):
    n_pairs = d_inner // PAIR
    pairs_per_group = n_pairs // SSM_GROUPS
    gb = SSM_GROUPS * n_state
    zx = zx_ref[0]
    z = zx[:, 0:d_inner]
    xbc = zx[:, d_inner:]
    prev = cs_ref[0]
    acc = cb_ref[...] + cw_ref[CONV_TAPS - 1:CONV_TAPS, :] * xbc
    for k in range(CONV_TAPS - 1):
        acc = acc + cw_ref[k:k + 1, :] * prev[k:k + 1, :]
    cso_ref[0, 0:CONV_TAPS - 2, :] = prev[1:CONV_TAPS - 1, :]
    cso_ref[0, CONV_TAPS - 2:CONV_TAPS - 1, :] = xbc
    act = jax.nn.silu(acc)
    xs = act[:, 0:d_inner]
    dt = _softplus(dt_ref[0] + dtb_ref[...])
    da = jnp.exp(dt * (-jnp.exp(alog_ref[...])))

    lo = lax.broadcasted_iota(jnp.int32, (1, PAIR), 1) < HEAD_DIM
    row_lo = lax.broadcasted_iota(jnp.int32, (PAIR, 1), 0) < HEAD_DIM
    row8 = lax.broadcasted_iota(jnp.int32, (HALO, 1), 0)

    for g in range(SSM_GROUPS):
        b_g = act[:, d_inner + g * n_state:d_inner + (g + 1) * n_state]
        c_g = act[:, d_inner + gb + g * n_state:d_inner + gb + (g + 1) * n_state]
        gated = []
        ssq = jnp.zeros((1, 1), F32)
        for jj in range(pairs_per_group):
            j = g * pairs_per_group + jj
            sl = slice(j * PAIR, (j + 1) * PAIR)
            x_p = xs[:, sl]
            xdt = x_p * _lane_pair(dt, j, lo)
            x_hi, x_lo = _split2(xdt)
            b_hi, b_lo = _split2(b_g)
            lhs = jnp.where(row8 == 0, x_hi, jnp.where(row8 == 1, x_lo, jnp.where(row8 == 2, x_hi,
                            jnp.where(row8 == 3, x_lo, 0.0)))).astype(BF16)
            rhs = jnp.where(row8 == 0, b_hi, jnp.where(row8 == 1, b_hi, jnp.where(row8 == 2, b_lo,
                            jnp.where(row8 == 3, b_lo, 0.0)))).astype(BF16)
            d_col = jnp.where(row_lo, da[:, 2 * j:2 * j + 1], da[:, 2 * j + 1:2 * j + 2])
            h_new = d_col * h_ref[0, sl, :] + _tn_dot(lhs, rhs)
            ho_ref[0, sl, :] = h_new
            c_hi, c_lo = _split2(c_g)
            c2 = jnp.where(row8 == 0, c_hi, jnp.where(row8 == 1, c_lo, 0.0)).astype(BF16)
            h_hi = h_new.astype(BF16)
            h_lo = (h_new - h_hi.astype(F32)).astype(BF16)
            yy = _nt_dot(c2, h_hi) + _nt_dot(c2, h_lo)
            y = yy[0:1, :] + yy[1:2, :]
            y = y + dsk_ref[:, sl] * x_p
            gy = y * jax.nn.silu(z[:, sl])
            ssq = ssq + jnp.sum(gy * gy, axis=-1, keepdims=True)
            gated.append(gy)
        scale = lax.rsqrt(ssq / (pairs_per_group * PAIR) + EPS)
        for jj in range(pairs_per_group):
            sl = slice((g * pairs_per_group + jj) * PAIR, (g * pairs_per_group + jj + 1) * PAIR)
            yg_ref[0, :, sl] = gated[jj] * scale * ng_ref[:, sl]


def _ssd_step(zx, dt_raw, state_conv, state_ssm, conv_w, conv_b, dt_bias, a_log, d_skip, norm_g,
              d_inner, n_state):
    bsz = zx.shape[0]
    conv_dim = conv_w.shape[1]
    kern = functools.partial(_ssd_step_kernel, d_inner=d_inner, n_state=n_state)
    const = lambda b: (0, 0)
    per_b = lambda b: (b, 0, 0)
    return pl.pallas_call(
        kern,
        grid=(bsz,),
        in_specs=[pl.BlockSpec((1, 1, d_inner + conv_dim), per_b),
                  pl.BlockSpec((1, 1, LANES), per_b),
                  pl.BlockSpec((1, CONV_TAPS - 1, conv_dim), per_b),
                  pl.BlockSpec((1, d_inner, n_state), per_b),
                  pl.BlockSpec((CONV_TAPS, conv_dim), const),
                  pl.BlockSpec((1, conv_dim), const),
                  pl.BlockSpec((1, LANES), const),
                  pl.BlockSpec((1, LANES), const),
                  pl.BlockSpec((1, d_inner), const),
                  pl.BlockSpec((1, d_inner), const)],
        out_specs=[pl.BlockSpec((1, 1, d_inner), per_b),
                   pl.BlockSpec((1, CONV_TAPS - 1, conv_dim), per_b),
                   pl.BlockSpec((1, d_inner, n_state), per_b)],
        out_shape=[jax.ShapeDtypeStruct((bsz, 1, d_inner), F32),
                   jax.ShapeDtypeStruct((bsz, CONV_TAPS - 1, conv_dim), F32),
                   jax.ShapeDtypeStruct((bsz, d_inner, n_state), F32)],
        compiler_params=_params("parallel"),
        name="ssd_step",
    )(zx.reshape(bsz, 1, -1), dt_raw.reshape(bsz, 1, LANES), state_conv,
      state_ssm.reshape(bsz, d_inner, n_state), conv_w, conv_b, dt_bias, a_log, d_skip, norm_g)


def _linear_res_kernel(a_ref, w_ref, r_ref, o_ref):
    o_ref[...] = r_ref[...] + jnp.dot(a_ref[...].astype(BF16), w_ref[...], preferred_element_type=F32)


def _linear_res(a, w, res):
    m, k = a.shape
    n = w.shape[1]
    tm = _row_tile(m, 1024)
    return pl.pallas_call(
        _linear_res_kernel,
        grid=(m // tm,),
        in_specs=[pl.BlockSpec((tm, k), lambda i: (i, 0)),
                  pl.BlockSpec((k, n), lambda i: (0, 0), pipeline_mode=pl.Buffered(1)),
                  pl.BlockSpec((tm, n), lambda i: (i, 0))],
        out_specs=pl.BlockSpec((tm, n), lambda i: (i, 0)),
        out_shape=jax.ShapeDtypeStruct((m, n), F32),
        compiler_params=_params("parallel"),
        name="linear_res",
    )(a, w, res)


def _ffn_kernel(x_ref, g_ref, wg_ref, wu_ref, wd_ref, o_ref, xn_ref, *, tf):
    @pl.when(pl.program_id(1) == 0)
    def _():
        x = x_ref[...]
        xn_ref[...] = (_rms_scale(x) * g_ref[...]).astype(BF16)
        o_ref[...] = x

    xn = xn_ref[...]
    acts = []
    for c0 in range(0, tf, MXU_WIDTH):
        cols = slice(c0, min(c0 + MXU_WIDTH, tf))
        gate = jnp.dot(xn, wg_ref[:, cols], preferred_element_type=F32)
        up = jnp.dot(xn, wu_ref[:, cols], preferred_element_type=F32)
        acts.append((jax.nn.silu(gate) * up).astype(BF16))
    o_ref[...] += jnp.dot(jnp.concatenate(acts, axis=1), wd_ref[...], preferred_element_type=F32)


def _ffn(x, g, w_gate_up, w_down):
    m, d = x.shape
    d_ff = w_down.shape[0]
    tm = _row_tile(m, 1024)
    tf = d_ff // 2 if d_ff % (2 * LANES) == 0 else d_ff
    nf = d_ff // tf
    return pl.pallas_call(
        functools.partial(_ffn_kernel, tf=tf),
        grid=(m // tm, nf),
        in_specs=[pl.BlockSpec((tm, d), lambda i, j: (i, 0)),
                  pl.BlockSpec((1, d), lambda i, j: (0, 0)),
                  pl.BlockSpec((d, tf), lambda i, j: (0, j)),
                  pl.BlockSpec((d, tf), lambda i, j: (0, nf + j)),
                  pl.BlockSpec((tf, d), lambda i, j: (j, 0))],
        out_specs=pl.BlockSpec((tm, d), lambda i, j: (i, 0)),
        out_shape=jax.ShapeDtypeStruct((m, d), F32),
        scratch_shapes=[pltpu.VMEM((tm, d), BF16)],
        compiler_params=_params("parallel", "arbitrary"),
        name="ffn",
    )(x, g, w_gate_up, w_gate_up, w_down)


def _qkv_kernel(x_ref, gq_ref, gkv_ref, w_ref, qn_ref, kn_ref, knc_ref, q_ref, k_ref, v4_ref, vd_ref,
                *, n_heads, k_transposed):
    lo = lax.broadcasted_iota(jnp.int32, (1, PAIR), 1) < HEAD_DIM

    def qk_norm(y, gain_ref, out_ref, post):
        for h in range(n_heads):
            yh = y[:, h * PAIR:(h + 1) * PAIR]
            sq = yh * yh
            s_lo = jnp.sum(jnp.where(lo, sq, 0.0), axis=-1, keepdims=True)
            s_hi = jnp.sum(jnp.where(lo, 0.0, sq), axis=-1, keepdims=True)
            r = jnp.where(lo, lax.rsqrt(s_lo / HEAD_DIM + EPS), lax.rsqrt(s_hi / HEAD_DIM + EPS))
            out_ref[:, h * PAIR:(h + 1) * PAIR] = (yh * r * gain_ref[...] * post).astype(out_ref.dtype)

    xh = _rms_scale(x_ref[...])
    xq = (xh * gq_ref[...]).astype(BF16)
    xkv = (xh * gkv_ref[...]).astype(BF16)
    qk_norm(jnp.dot(xq, w_ref[0], preferred_element_type=F32), qn_ref, q_ref, HEAD_DIM ** -0.5 * LOG2E)

    if k_transposed:
        kt = _nt_dot(w_ref[1], xkv)
        for g in range(2 * n_heads):
            rows = slice(g * HEAD_DIM, (g + 1) * HEAD_DIM)
            blk = kt[rows, :]
            r = lax.rsqrt(jnp.mean(blk * blk, axis=0, keepdims=True) + EPS)
            k_ref[0, rows, :] = blk * r * knc_ref[...]
    else:
        qk_norm(jnp.dot(xkv, w_ref[1], preferred_element_type=F32), kn_ref, k_ref, 1.0)

    v = jnp.dot(xkv, w_ref[2], preferred_element_type=F32)
    vd_ref[...] = v.astype(BF16)
    for h in range(n_heads):
        v4_ref[:, h, :] = v[:, h * PAIR:(h + 1) * PAIR]


def _qkv(x, g_q, g_kv, w_q, w_k, w_v, q_norm, k_norm, seq=None):
    m, d = x.shape
    n_heads = d // PAIR
    k_transposed = seq is not None
    tm = _row_tile(m if seq is None else seq, 512)
    w = jnp.stack([w_q, w_k.T if k_transposed else w_k, w_v]).astype(BF16)
    pair_gain = jnp.tile(k_norm.astype(F32), 2).reshape(1, PAIR)
    kern = functools.partial(_qkv_kernel, n_heads=n_heads, k_transposed=k_transposed)
    const = lambda i: (0, 0)
    rowblk = lambda i: (i, 0)
    if k_transposed:
        per_seq = seq // tm
        k_spec = pl.BlockSpec((1, d, tm), lambda i: (i // per_seq, 0, i % per_seq))
        k_shape = jax.ShapeDtypeStruct((m // seq, d, seq), F32)
    else:
        k_spec = pl.BlockSpec((tm, d), rowblk)
        k_shape = jax.ShapeDtypeStruct((m, d), F32)
    return pl.pallas_call(
        kern,
        grid=(m // tm,),
        in_specs=[pl.BlockSpec((tm, d), rowblk),
                  pl.BlockSpec((1, d), const),
                  pl.BlockSpec((1, d), const),
                  pl.BlockSpec((3, d, d), lambda i: (0, 0, 0)),
                  pl.BlockSpec((1, PAIR), const),
                  pl.BlockSpec((1, PAIR), const),
                  pl.BlockSpec((HEAD_DIM, 1), const)],
        out_specs=[pl.BlockSpec((tm, d), rowblk), k_spec,
                   pl.BlockSpec((tm, n_heads, PAIR), lambda i: (i, 0, 0)),
                   pl.BlockSpec((tm, d), rowblk)],
        out_shape=[jax.ShapeDtypeStruct((m, d), BF16), k_shape,
                   jax.ShapeDtypeStruct((m, n_heads, PAIR), F32),
                   jax.ShapeDtypeStruct((m, d), BF16)],
        compiler_params=_params("parallel"),
        name="qkv_proj",
    )(x, g_q, g_kv, w, jnp.tile(q_norm.astype(F32), 2).reshape(1, PAIR), pair_gain,
      k_norm.astype(F32).reshape(HEAD_DIM, 1))


def _bias_tiles_kernel(rb_ref, o_ref, *, t):
    h = pl.program_id(0)
    i = lax.broadcasted_iota(jnp.int32, (t, t), 0)
    j = lax.broadcasted_iota(jnp.int32, (t, t), 1)
    for off in range(2):
        n = i - j + off * t
        bucket = _bucket(n)
        tile = jnp.zeros((t, t), F32)
        for b in range(N_BUCKETS):
            tile = jnp.where(bucket == b, rb_ref[b, h], tile)
        o_ref[0, off] = jnp.where(n >= 0, (tile - rb_ref[N_BUCKETS - 1, h]) * LOG2E, -jnp.inf)


def _bias_tiles(rel_bias, t):
    n_heads = rel_bias.shape[1]
    return pl.pallas_call(
        functools.partial(_bias_tiles_kernel, t=t),
        grid=(n_heads,),
        in_specs=[pl.BlockSpec(memory_space=pltpu.SMEM)],
        out_specs=pl.BlockSpec((1, 2, t, t), lambda h: (h, 0, 0, 0)),
        out_shape=jax.ShapeDtypeStruct((n_heads, 2, t, t), F32),
        compiler_params=_params("parallel"),
        name="bias_tiles",
    )(rel_bias)


def _lambda_value(lam_ref, lambda_init):
    s1 = jnp.sum(lam_ref[0:1, :] * lam_ref[1:2, :], axis=-1, keepdims=True)
    s2 = jnp.sum(lam_ref[2:3, :] * lam_ref[3:4, :], axis=-1, keepdims=True)
    return jnp.exp(s1) - jnp.exp(s2) + lambda_init


def _attn_kernel(q_ref, k_ref, v_ref, bias_ref, rb_ref, lam_ref, sg_ref, o_ref,
                 kb_ref, kb2_ref, q12_ref, m_ref, l_ref, acc_ref, *, t, rc_far, rc_near, nq, lambda_init):
    h = pl.program_id(1)
    qi = pl.program_id(2)

    @pl.when(qi == 0)
    def _():
        for ki in range(nq):
            kb_ref[ki] = k_ref[0, :, ki * t:(ki + 1) * t].astype(BF16)
        for kp in range(nq // 2):
            kb2_ref[kp] = k_ref[0, :, kp * 2 * t:(kp + 1) * 2 * t].astype(BF16)

    q = q_ref[0]
    lo = lax.broadcasted_iota(jnp.int32, (t, PAIR), 1) < HEAD_DIM
    zero = jnp.zeros_like(q)
    q12_ref[0:t, :] = jnp.where(lo, q, zero)
    q12_ref[t:2 * t, :] = jnp.where(lo, zero, q)

    m_ref[...] = jnp.full(m_ref.shape, -jnp.inf, F32)
    l_ref[...] = jnp.zeros(l_ref.shape, F32)
    acc_ref[...] = jnp.zeros(acc_ref.shape, F32)

    far_bias = rb_ref[N_BUCKETS - 1, h] * LOG2E

    def block(rc, keys_of, base, cols_of, near_of=None):
        for c in range(2 * t // rc):
            rows = slice(c * rc, (c + 1) * rc)
            r0 = (c * rc) % t
            ncols = cols_of(r0)
            s = jnp.dot(q12_ref[rows, :], keys_of(ncols), preferred_element_type=F32)
            tiles = []
            for k in range(ncols // LANES):
                sk = s[:, k * LANES:(k + 1) * LANES]
                near = None if near_of is None else near_of(k)
                if near is not None and near[0] * t + r0 - (near[1] * LANES + LANES - 1) < MAX_DISTANCE:
                    slab = []
                    for a in range(rc // LANES):
                        rel = (near[0] * t + r0) // LANES + a - near[1]
                        if rel < 0:
                            slab.append(jnp.full((LANES, LANES), -jnp.inf, F32))
                        elif rel <= 1:
                            slab.append(bias_ref[0, rel])
                        else:
                            slab.append(jnp.zeros((LANES, LANES), F32))
                    sk = sk + jnp.concatenate(slab, axis=0)
                tiles.append(sk)
            m_prev = m_ref[rows, :]
            part = tiles[0]
            for sk in tiles[1:]:
                part = jnp.maximum(part, sk)
            m_new = jnp.maximum(m_prev, jnp.max(part, axis=-1, keepdims=True) + far_bias)
            alpha = jnp.exp2(m_prev - m_new)
            m_sub = m_new - far_bias
            ps = [jnp.exp2(sk - m_sub) for sk in tiles]
            psum = ps[0]
            for pk in ps[1:]:
                psum = psum + pk
            l_ref[rows, :] = alpha * l_ref[rows, :] + psum
            pv = jnp.dot(jnp.concatenate(ps, axis=1).astype(BF16), v_ref[0, pl.ds(base, ncols), :],
                         preferred_element_type=F32)
            acc_ref[rows, :] = alpha * acc_ref[rows, :] + pv
            m_ref[rows, :] = m_new

    n_far = jnp.maximum(qi - 1, 0)

    def far_pair(kp, carry):
        block(rc_far, lambda n: kb2_ref[kp, :, 0:n], pl.multiple_of(kp * 2 * t, 2 * t), lambda r0: 2 * t)
        return carry

    lax.fori_loop(0, n_far // 2, far_pair, 0)

    @pl.when(n_far % 2 == 1)
    def _():
        ki = n_far - 1
        block(rc_far, lambda n: kb_ref[ki, :, 0:n], pl.multiple_of(ki * t, t), lambda r0: t)

    nlt = t // LANES

    @pl.when(qi >= 1)
    def _():
        ki = qi - 1
        block(rc_near, lambda n: jnp.concatenate([kb_ref[ki], kb_ref[qi, :, 0:n - t]], axis=1),
              pl.multiple_of(ki * t, t), lambda r0: t + r0 + rc_near,
              near_of=lambda k: (1, k) if k < nlt else (0, k - nlt))

    @pl.when(qi == 0)
    def _():
        block(rc_near, lambda n: kb_ref[0, :, 0:n], 0, lambda r0: r0 + rc_near, near_of=lambda k: (0, k))

    l = jnp.sum(l_ref[...], axis=-1, keepdims=True)
    o1 = acc_ref[0:t, :] / l[0:t, :]
    o2 = acc_ref[t:2 * t, :] / l[t:2 * t, :]
    o = o1 - _lambda_value(lam_ref, lambda_init) * o2
    o_ref[0] = (_rms_scale(o) * sg_ref[...] * (1.0 - lambda_init)).astype(o_ref.dtype)


def _prompt_attention(q, k_t, v, rel_bias, lam_rows, subln, lambda_init, bsz, seq):
    d = q.shape[-1]
    n_heads = d // PAIR
    t = min(seq, 512)
    assert seq % t == 0 and t >= MAX_DISTANCE
    nq = seq // t
    assert MAX_DISTANCE <= LANES
    tiles = _bias_tiles(rel_bias, LANES)
    rc_far, rc_near = t, t
    assert t % rc_near == 0 and rc_near % LANES == 0
    kern = functools.partial(_attn_kernel, t=t, rc_far=rc_far, rc_near=rc_near, nq=nq, lambda_init=lambda_init)
    const = lambda b, h, i: (0, 0)
    return pl.pallas_call(
        kern,
        grid=(bsz, n_heads, nq),
        in_specs=[pl.BlockSpec((1, t, PAIR), lambda b, h, i: (b, i, h)),
                  pl.BlockSpec((1, PAIR, seq), lambda b, h, i: (b, h, 0)),
                  pl.BlockSpec((1, seq, PAIR), lambda b, h, i: (b, 0, h)),
                  pl.BlockSpec((1, 2, LANES, LANES), lambda b, h, i: (h, 0, 0, 0)),
                  pl.BlockSpec(memory_space=pltpu.SMEM),
                  pl.BlockSpec((4, LANES), const),
                  pl.BlockSpec((1, PAIR), const)],
        out_specs=pl.BlockSpec((1, t, PAIR), lambda b, h, i: (b, i, h)),
        out_shape=jax.ShapeDtypeStruct((bsz, seq, d), BF16),
        scratch_shapes=[pltpu.VMEM((nq, PAIR, t), BF16), pltpu.VMEM((max(nq // 2, 1), PAIR, 2 * t), BF16),
                        pltpu.VMEM((2 * t, PAIR), BF16),
                        pltpu.VMEM((2 * t, LANES), F32), pltpu.VMEM((2 * t, LANES), F32),
                        pltpu.VMEM((2 * t, PAIR), F32)],
        compiler_params=_params("parallel", "parallel", "arbitrary"),
        name="prompt_attention",
    )(q.reshape(bsz, seq, d), k_t, v.reshape(bsz, seq, d), tiles, rel_bias, lam_rows, subln)


def _sample_attn_kernel(pt_ref, q_ref, kn_ref, vn_ref, *rest, pages_per_step, past, n_heads, lambda_init):
    k_refs = rest[0:pages_per_step]
    v_refs = rest[pages_per_step:2 * pages_per_step]
    rb_ref, lam_ref, sg_ref, o_ref, qb_ref, bias_ref, m_ref, l_ref, acc_ref = rest[2 * pages_per_step:]
    step_id = pl.program_id(1)
    last = pl.num_programs(1) - 1
    nrow = 2 * n_heads
    d = n_heads * PAIR
    keys = pages_per_step * PAGE_SIZE

    @pl.when(step_id == 0)
    def _():
        bias_ref[...] = jnp.broadcast_to(rb_ref[:, N_BUCKETS - 1:N_BUCKETS] * LOG2E, (nrow, keys))

    @pl.when(step_id == last)
    def _():
        key = lax.broadcasted_iota(jnp.int32, (nrow, keys), 1)
        bucket = _bucket(past - (step_id * keys + key))
        bias = jnp.zeros((nrow, keys), F32)
        for b in range(N_BUCKETS):
            bias = jnp.where(bucket == b, rb_ref[:, b:b + 1], bias)
        bias_ref[...] = bias * LOG2E

    @pl.when(step_id == 0)
    def _():
        q = q_ref[0].astype(F32)
        row = lax.broadcasted_iota(jnp.int32, (nrow, d), 0)
        col = lax.broadcasted_iota(jnp.int32, (nrow, d), 1)
        qblk = jnp.where(jnp.right_shift(col, 6) == row, jnp.broadcast_to(q, (nrow, d)), 0.0)
        qb_ref[...] = qblk.astype(BF16)
        m_ref[...] = jnp.sum(qblk * kn_ref[0], axis=-1, keepdims=True) + rb_ref[:, 0:1] * LOG2E
        l_ref[...] = jnp.ones(l_ref.shape, F32)
        vn = vn_ref[0].astype(F32)
        for h in range(n_heads):
            acc_ref[2 * h:2 * h + 2, :] = jnp.broadcast_to(vn[:, h * PAIR:(h + 1) * PAIR], (2, PAIR))

    qb = qb_ref[...]
    s = jnp.concatenate([jnp.dot(qb, k_refs[g][0].astype(BF16), preferred_element_type=F32)
                         for g in range(pages_per_step)], axis=1) + bias_ref[...]
    m_prev = m_ref[...]
    m_new = jnp.maximum(m_prev, jnp.max(s, axis=-1, keepdims=True))
    alpha = jnp.exp2(m_prev - m_new)
    p = jnp.exp2(s - m_new)
    l_ref[...] = alpha * l_ref[...] + jnp.sum(p, axis=-1, keepdims=True)
    pb = p.astype(BF16)
    row_head = jnp.right_shift(lax.broadcasted_iota(jnp.int32, (nrow, PAIR), 0), 1)
    pv = jnp.zeros((nrow, PAIR), F32)
    for h in range(n_heads):
        v_h = jnp.concatenate([v_refs[g][0, pl.ds(h, PAGE_SIZE, stride=n_heads), :].astype(BF16)
                               for g in range(pages_per_step)], axis=0)
        pv = jnp.where(row_head == h, jnp.dot(pb, v_h, preferred_element_type=F32), pv)
    acc_ref[...] = alpha * acc_ref[...] + pv
    m_ref[...] = m_new

    @pl.when(step_id == last)
    def _():
        rowc = lax.broadcasted_iota(jnp.int32, (nrow, 1), 0)
        lam = _lambda_value(lam_ref, lambda_init)
        coef = jnp.where(jnp.bitwise_and(rowc, 1) == 0, 1.0, -lam) / l_ref[...]
        scaled = acc_ref[...] * coef
        o = scaled + pltpu.roll(scaled, nrow - 1, 0)
        on = _rms_scale(o) * sg_ref[...] * (1.0 - lambda_init)
        for h in range(n_heads):
            o_ref[0, :, h * PAIR:(h + 1) * PAIR] = on[2 * h:2 * h + 1, :]


def _sample_attention(q, k_new, v_new, cache_k, cache_v, page_table, rel_rows, lam_rows, subln, lambda_init):
    bsz, d = q.shape
    n_heads = d // PAIR
    n_pages = page_table.shape[1]
    past = n_pages * PAGE_SIZE
    pages_per_step = max(g for g in (16, 8, 4, 2, 1) if n_pages % g == 0)
    n_pool = cache_k.shape[0]
    kern = functools.partial(_sample_attn_kernel, pages_per_step=pages_per_step, past=past,
                             n_heads=n_heads, lambda_init=lambda_init)
    per_b = lambda b, s, pt: (b, 0, 0)
    const = lambda b, s, pt: (0, 0)

    def k_spec(g):
        return pl.BlockSpec((1, d, PAGE_SIZE), lambda b, s, pt: (pt[b, s * pages_per_step + g], 0, 0))

    def v_spec(g):
        return pl.BlockSpec((1, PAGE_SIZE * n_heads, PAIR),
                            lambda b, s, pt: (pt[b, s * pages_per_step + g], 0, 0))

    grid_spec = pltpu.PrefetchScalarGridSpec(
        num_scalar_prefetch=1,
        grid=(bsz, n_pages // pages_per_step),
        in_specs=([pl.BlockSpec((1, 1, d), per_b)] * 3
                  + [k_spec(g) for g in range(pages_per_step)]
                  + [v_spec(g) for g in range(pages_per_step)]
                  + [pl.BlockSpec((2 * n_heads, N_BUCKETS), const),
                     pl.BlockSpec((4, LANES), const),
                     pl.BlockSpec((1, PAIR), const)]),
        out_specs=pl.BlockSpec((1, 1, d), per_b),
        scratch_shapes=[pltpu.VMEM((2 * n_heads, d), BF16),
                        pltpu.VMEM((2 * n_heads, pages_per_step * PAGE_SIZE), F32),
                        pltpu.VMEM((2 * n_heads, 1), F32), pltpu.VMEM((2 * n_heads, 1), F32),
                        pltpu.VMEM((2 * n_heads, PAIR), F32)],
    )
    ck = jnp.transpose(cache_k, (0, 2, 3, 4, 1)).reshape(n_pool, d, PAGE_SIZE)
    cv = cache_v.reshape(n_pool, PAGE_SIZE * n_heads, PAIR)
    return pl.pallas_call(
        kern,
        grid_spec=grid_spec,
        out_shape=jax.ShapeDtypeStruct((bsz, 1, d), F32),
        compiler_params=_params("parallel", "arbitrary"),
        name="sample_attention",
    )(page_table, q.reshape(bsz, 1, d), k_new.reshape(bsz, 1, d), v_new.reshape(bsz, 1, d),
      *([ck] * pages_per_step), *([cv] * pages_per_step), rel_rows, lam_rows, subln).reshape(bsz, d)


def _pad_lanes(v):
    return jnp.pad(v.astype(F32), (0, LANES - v.shape[0])).reshape(1, LANES)


def kernel(x_prompt, x_sample, state_conv, state_ssm, cache_k, cache_v, page_table, norm_mix, norm_ffn, w_in, conv_w, conv_b, dt_bias, a_log, d_skip, ssm_norm, w_out_ssm, norm_kv, w_kv, k_norm, w_q, q_norm, lambda_q1, lambda_k1, lambda_q2, lambda_k2, subln, w_o, rel_bias, w_gate_up, w_down):
    bsz, seq, d = x_prompt.shape
    dec = x_sample.shape[0]
    assert x_sample.shape[1] == 1
    n_ssm_layers, ssm_heads = dt_bias.shape
    depth = norm_mix.shape[0]
    assert n_ssm_layers == 1 and depth == 2
    d_inner = w_out_ssm.shape[1]
    n_state = state_ssm.shape[-1]
    assert d_inner == ssm_heads * HEAD_DIM and n_state == LANES and ssm_heads <= LANES
    n_heads = d // PAIR

    xp = x_prompt.reshape(bsz * seq, d)
    xs = x_sample.reshape(dec, d)
    row = lambda v: v.astype(F32).reshape(1, -1)

    zx_cols = 2 * d_inner + 2 * SSM_GROUPS * n_state
    w_z = w_in[0][:, :d_inner].astype(BF16)
    w_c = w_in[0][:, d_inner:zx_cols].astype(BF16)
    w_dt = jnp.pad(w_in[0][:, zx_cols:], ((0, 0), (0, LANES - ssm_heads))).astype(BF16)
    g_mix0 = row(norm_mix[0])
    cw0, cb0 = conv_w[0], row(conv_b[0])
    ssd_args = (_pad_lanes(dt_bias[0]), _pad_lanes(a_log[0]),
                row(jnp.repeat(d_skip[0], HEAD_DIM)), row(ssm_norm[0]))
    w_out = w_out_ssm[0].astype(BF16)
    w_gu0, w_dn0 = w_gate_up[0].astype(BF16), w_down[0].astype(BF16)

    z_p, xbc_p, dt_p, conv_p = _in_proj(xp, g_mix0, w_z, w_c, w_dt, cw0, cb0, seq=seq)
    yg_p, ssm_p = _ssd_prompt(z_p, xbc_p, dt_p, *ssd_args, bsz, seq, d_inner, n_state)
    xp = _linear_res(yg_p, w_out, xp)
    xp = _ffn(xp, row(norm_ffn[0]), w_gu0, w_dn0)

    z_s, xbc_s, dt_s = _in_proj(xs, g_mix0, w_z, w_c, w_dt)
    zx_s = jnp.concatenate([z_s, xbc_s], axis=1)
    yg_s, conv_s, ssm_s = _ssd_step(zx_s, dt_s, state_conv[0], state_ssm[0], cw0, cb0, *ssd_args,
                                    d_inner, n_state)
    xs = _linear_res(yg_s.reshape(dec, d_inner), w_out, xs)
    xs = _ffn(xs, row(norm_ffn[0]), w_gu0, w_dn0)

    lambda_init = 0.8 - 0.6 * math.exp(-0.3 * 1)
    qkv_args = (row(norm_mix[1]), row(norm_kv), w_q[0], w_kv[:, :d], w_kv[:, d:], q_norm[0], k_norm)
    lam_rows = jnp.pad(jnp.stack([lambda_q1[0], lambda_k1[0], lambda_q2[0], lambda_k2[0]]).astype(F32),
                       ((0, 0), (0, LANES - HEAD_DIM)))
    sg = row(subln[0])
    w_oo = w_o[0].astype(BF16)
    w_gu1, w_dn1 = w_gate_up[1].astype(BF16), w_down[1].astype(BF16)

    q_p, kt_p, v_p, vd_p = _qkv(xp, *qkv_args, seq=seq)
    o_p = _prompt_attention(q_p, kt_p, vd_p, rel_bias.astype(F32), lam_rows, sg, lambda_init, bsz, seq)
    xp = _linear_res(o_p.reshape(bsz * seq, d), w_oo, xp)
    xp = _ffn(xp, row(norm_ffn[1]), w_gu1, w_dn1)

    q_s, k_s, v_s, vd_s = _qkv(xs, *qkv_args)
    rel_rows = jnp.repeat(rel_bias.astype(F32).T, 2, axis=0)
    o_s = _sample_attention(q_s.astype(F32), k_s, vd_s.astype(F32), cache_k, cache_v, page_table,
                            rel_rows, lam_rows, sg, lambda_init)
    xs = _linear_res(o_s, w_oo, xs)
    xs = _ffn(xs, row(norm_ffn[1]), w_gu1, w_dn1)

    k_p = jnp.transpose(kt_p.reshape(bsz, n_heads, 2, HEAD_DIM, seq), (0, 4, 1, 2, 3))
    return (xp.reshape(bsz, seq, d), xs.reshape(dec, 1, d),
            conv_p[None], ssm_p.reshape(1, bsz, ssm_heads, HEAD_DIM, n_state),
            k_p, v_p.reshape(bsz, seq, n_heads, PAIR),
            conv_s[None], ssm_s.reshape(1, dec, ssm_heads, HEAD_DIM, n_state),
            k_s.reshape(dec, 1, n_heads, 2, HEAD_DIM), v_s.reshape(dec, 1, n_heads, PAIR))
```
